```python
import math
import jax, jax.numpy as jnp
from jax import lax
import numpy as np

D_MODEL = 1024
BATCH = 8
SEQ = 2048
DEPTH = 1
DEC_BATCH = 8
DEC_SEQ = 8192
PAST_LEN = 128

D_SSM = D_MODEL // 2
SSM_GROUP = 16
N_GROUPS = D_SSM // SSM_GROUP
STATE = 64
D_ATTN = D_MODEL // 2
N_HEADS = 4
HEAD_DIM = D_ATTN // (2 * N_HEADS)
Q_BLOCK = 128
ROPE_THETA = 10000.0
NORM_EPS = 1e-6
SUBLN_EPS = 1e-5
DT_MIN = 1e-3
DT_MAX = 1e-1
N_IN = 2 * D_SSM + 4 * D_ATTN + 2 * D_MODEL

kernel_name = "hybrid_s5_diffattn_gated_encoder"


def _rmsnorm(x, g, eps=NORM_EPS):
    xf = x.astype(jnp.float32)
    return xf * lax.rsqrt(jnp.mean(xf * xf, axis=-1, keepdims=True) + eps) * g.astype(jnp.float32)


def _rotary(x):
    L = x.shape[1]
    half = HEAD_DIM // 2
    inv_freq = 1.0 / (ROPE_THETA ** (jnp.arange(0, half, dtype=jnp.float32) * 2.0 / HEAD_DIM))
    ang = jnp.arange(L, dtype=jnp.float32)[:, None] * inv_freq[None, :]
    cos = jnp.cos(ang)[None, :, None, None, :]
    sin = jnp.sin(ang)[None, :, None, None, :]
    x1, x2 = x[..., :half], x[..., half:]
    return jnp.concatenate([x1 * cos - x2 * sin, x2 * cos + x1 * sin], axis=-1)


def _scan_combine(e1, e2):
    a1, b1 = e1
    a2, b2 = e2
    return (a2 * a1, a2 * b1 + b2)


def _s5_bidirectional(u, a_re, a_im, log_dt, b_re, b_im, c_re, c_im, d_skip):
    f32 = jnp.float32
    Bsz, L, _ = u.shape
    A = lax.complex(a_re.astype(f32), a_im.astype(f32))
    dt = jnp.exp(log_dt.astype(f32))[..., None]
    A_bar = jnp.exp(A * dt)
    B_c = lax.complex(b_re.astype(f32), b_im.astype(f32))
    B_bar = ((A_bar - 1.0) / A)[..., None] * B_c
    C_c = lax.complex(c_re.astype(f32), c_im.astype(f32))
    ug = u.reshape(Bsz, L, N_GROUPS, SSM_GROUP)

    def per_sequence(us):
        bu = jnp.einsum('ngpc,lgc->nlgp', B_bar, us.astype(jnp.complex64))
        a = jnp.broadcast_to(A_bar[:, None], bu.shape)
        _, h_f = lax.associative_scan(_scan_combine, (a[0], bu[0]), axis=0)
        _, h_b = lax.associative_scan(_scan_combine, (a[1], bu[1]), axis=0, reverse=True)
        y = jnp.einsum('gcp,lgp->lgc', C_c[0], h_f) + jnp.einsum('gcp,lgp->lgc', C_c[1], h_b)
        return jnp.real(y)

    y = lax.map(per_sequence, ug).reshape(Bsz, L, D_SSM)
    return y + d_skip.astype(f32) * u


def _diff_attention(q, k, v, lam):
    Bsz, L = q.shape[0], q.shape[1]
    nb = L // Q_BLOCK
    scale = 1.0 / math.sqrt(HEAD_DIM)
    qb = q.reshape(Bsz, nb, Q_BLOCK, N_HEADS, 2, HEAD_DIM).transpose(1, 0, 2, 3, 4, 5)

    def block(qi):
        s = jnp.einsum('bqhtd,bkhtd->bthqk', qi, k).astype(jnp.float32) * scale
        p = jax.nn.softmax(s, axis=-1)
        w = p[:, 0] - lam * p[:, 1]
        return jnp.einsum('bhqk,bkhe->bqhe', w, v)

    o = lax.map(block, qb)
    return o.transpose(1, 0, 2, 3, 4).reshape(Bsz, L, N_HEADS, 2 * HEAD_DIM)


def _layer(x, li, norm_g, w_in, ssm_a_re, ssm_a_im, ssm_log_dt, ssm_b_re, ssm_b_im,
           ssm_c_re, ssm_c_im, ssm_d, w_glu, b_glu, lambda_q1, lambda_k1, lambda_q2,
           lambda_k2, subln_g, w_branch, w_out):
    f32 = jnp.float32
    Bsz, L, _ = x.shape
    h = _rmsnorm(x, norm_g)
    proj = jnp.einsum('bld,dn->bln', h, w_in.astype(f32))
    o = 0
    xs = proj[..., o:o + D_SSM]; o += D_SSM
    zs = proj[..., o:o + D_SSM]; o += D_SSM
    q = proj[..., o:o + D_ATTN]; o += D_ATTN
    k = proj[..., o:o + D_ATTN]; o += D_ATTN
    v = proj[..., o:o + D_ATTN]; o += D_ATTN
    za = proj[..., o:o + D_ATTN]; o += D_ATTN
    gs = proj[..., o:o + D_MODEL]; o += D_MODEL
    ga = proj[..., o:o + D_MODEL]

    ys = _s5_bidirectional(xs, ssm_a_re, ssm_a_im, ssm_log_dt, ssm_b_re, ssm_b_im,
                           ssm_c_re, ssm_c_im, ssm_d)
    ys = jax.nn.gelu(ys)
    ys = ys * jax.nn.sigmoid(jnp.einsum('bld,de->ble', ys, w_glu.astype(f32)) + b_glu.astype(f32))
    ys = ys * jax.nn.silu(zs)

    q = _rotary(q.reshape(Bsz, L, N_HEADS, 2, HEAD_DIM))
    k = _rotary(k.reshape(Bsz, L, N_HEADS, 2, HEAD_DIM))
    v = v.reshape(Bsz, L, N_HEADS, 2 * HEAD_DIM)
    lam_init = 0.8 - 0.6 * math.exp(-0.3 * li)
    lam = (jnp.exp(jnp.sum(lambda_q1.astype(f32) * lambda_k1.astype(f32)))
           - jnp.exp(jnp.sum(lambda_q2.astype(f32) * lambda_k2.astype(f32))) + lam_init)
    ya = _diff_attention(q, k, v, lam)
    ya = _rmsnorm(ya, subln_g, eps=SUBLN_EPS) * (1.0 - lam_init)
    ya = ya.reshape(Bsz, L, D_ATTN) * jax.nn.silu(za)

    wb = w_branch.astype(f32)
    ps = jnp.einsum('bld,de->ble', ys, wb[0])
    pa = jnp.einsum('bld,de->ble', ya, wb[1])
    merged = jax.nn.sigmoid(gs) * ps + jax.nn.sigmoid(ga) * pa
    return x.astype(f32) + jnp.einsum('bld,de->ble', merged, w_out.astype(f32))


def _trunk(x, norm_g, w_in, ssm_a_re, ssm_a_im, ssm_log_dt, ssm_b_re, ssm_b_im,
           ssm_c_re, ssm_c_im, ssm_d, w_glu, b_glu, lambda_q1, lambda_k1, lambda_q2,
           lambda_k2, subln_g, w_branch, w_out, final_g):
    dtype = x.dtype
    h = x.astype(jnp.float32)
    for li in range(DEPTH):
        h = _layer(h, li, norm_g[li], w_in[li], ssm_a_re[li], ssm_a_im[li], ssm_log_dt[li],
                   ssm_b_re[li], ssm_b_im[li], ssm_c_re[li], ssm_c_im[li], ssm_d[li],
                   w_glu[li], b_glu[li], lambda_q1[li], lambda_k1[li], lambda_q2[li],
                   lambda_k2[li], subln_g[li], w_branch[li], w_out[li])
    return _rmsnorm(h, final_g).astype(dtype)


def setup_inputs(seed: int = 0) -> dict:
    key = jax.random.key(seed)
    ks = jax.random.split(key, 24)
    f32 = jnp.float32
    nrm = lambda k, s: jax.random.normal(k, s, f32)
    n_idx = jnp.arange(STATE, dtype=f32)
    a_re = -0.5 + 0.01 * nrm(ks[4], (DEPTH, 2, N_GROUPS, STATE))
    a_im = jnp.pi * n_idx + 0.01 * nrm(ks[5], (DEPTH, 2, N_GROUPS, STATE))
    log_dt = jax.random.uniform(ks[6], (DEPTH, 2, N_GROUPS), f32,
                                math.log(DT_MIN), math.log(DT_MAX))
    b_scale = (2.0 * SSM_GROUP) ** -0.5
    c_scale = (2.0 * STATE) ** -0.5
    return {
        "x_prompt": nrm(ks[0], (BATCH, SEQ, D_MODEL)),
        "x_sample": nrm(ks[1], (DEC_BATCH, DEC_SEQ, D_MODEL)),
        "norm_g": 1.0 + 0.01 * nrm(ks[2], (DEPTH, D_MODEL)),
        "w_in": nrm(ks[3], (DEPTH, D_MODEL, N_IN)) * D_MODEL ** -0.5,
        "ssm_a_re": a_re,
        "ssm_a_im": a_im,
        "ssm_log_dt": log_dt,
        "ssm_b_re": nrm(ks[7], (DEPTH, 2, N_GROUPS, STATE, SSM_GROUP)) * b_scale,
        "ssm_b_im": nrm(ks[8], (DEPTH, 2, N_GROUPS, STATE, SSM_GROUP)) * b_scale,
        "ssm_c_re": nrm(ks[9], (DEPTH, 2, N_GROUPS, SSM_GROUP, STATE)) * c_scale,
        "ssm_c_im": nrm(ks[10], (DEPTH, 2, N_GROUPS, SSM_GROUP, STATE)) * c_scale,
        "ssm_d": nrm(ks[11], (DEPTH, D_SSM)),
        "w_glu": nrm(ks[12], (DEPTH, D_SSM, D_SSM)) * D_SSM ** -0.5,
        "b_glu": 0.01 * nrm(ks[13], (DEPTH, D_SSM)),
        "lambda_q1": 0.1 * nrm(ks[14], (DEPTH, HEAD_DIM)),
        "lambda_k1": 0.1 * nrm(ks[15], (DEPTH, HEAD_DIM)),
        "lambda_q2": 0.1 * nrm(ks[16], (DEPTH, HEAD_DIM)),
        "lambda_k2": 0.1 * nrm(ks[17], (DEPTH, HEAD_DIM)),
        "subln_g": 1.0 + 0.01 * nrm(ks[18], (DEPTH, 2 * HEAD_DIM)),
        "w_branch": nrm(ks[19], (DEPTH, 2, D_SSM, D_MODEL)) * D_SSM ** -0.5,
        "w_out": nrm(ks[20], (DEPTH, D_MODEL, D_MODEL)) * D_MODEL ** -0.5,
        "final_g": 1.0 + 0.01 * nrm(ks[21], (D_MODEL,)),
    }


def reference(x_prompt, x_sample, norm_g, w_in, ssm_a_re, ssm_a_im, ssm_log_dt, ssm_b_re,
              ssm_b_im, ssm_c_re, ssm_c_im, ssm_d, w_glu, b_glu, lambda_q1, lambda_k1,
              lambda_q2, lambda_k2, subln_g, w_branch, w_out, final_g):
    y_prompt = _trunk(x_prompt, norm_g, w_in, ssm_a_re, ssm_a_im, ssm_log_dt, ssm_b_re,
                      ssm_b_im, ssm_c_re, ssm_c_im, ssm_d, w_glu, b_glu, lambda_q1,
                      lambda_k1, lambda_q2, lambda_k2, subln_g, w_branch, w_out, final_g)
    y_sample = _trunk(x_sample, norm_g, w_in, ssm_a_re, ssm_a_im, ssm_log_dt, ssm_b_re,
                      ssm_b_im, ssm_c_re, ssm_c_im, ssm_d, w_glu, b_glu, lambda_q1,
                      lambda_k1, lambda_q2, lambda_k2, subln_g, w_branch, w_out, final_g)
    return (y_prompt, y_sample)
```

```python
import functools
import math

import jax
import jax.numpy as jnp
from jax import lax
from jax.experimental import pallas as pl
from jax.experimental.pallas import tpu as pltpu

D_MODEL = 1024
D_SSM = 512
SSM_GROUP = 16
N_GROUPS = 32
STATE = 64
D_ATTN = 512
N_HEADS = 4
HEAD_DIM = 64
ROPE_THETA = 10000.0
NORM_EPS = 1e-6
SUBLN_EPS = 1e-5
LAM_INIT = 0.8 - 0.6 * math.exp(-0.3 * 0)

CHUNK = 16
CHUNK_LANES = CHUNK * SSM_GROUP
LANES = 128
VMEM_LIMIT = 56 * 1024 * 1024

TOKEN_BLOCK = 512
ATTN_BQ = 512
ATTN_BK = 256
SSM_SEG_CHUNKS = 16
SSM_GROUP_BLOCK = 4
SSM_OUT_ROWS = 256

f32 = jnp.float32
bf16 = jnp.bfloat16


def _params(*sem):
    return pltpu.CompilerParams(dimension_semantics=sem, vmem_limit_bytes=VMEM_LIMIT)


def _dot(a, b):
    return jnp.dot(a, b, preferred_element_type=f32)


def _dot_nt(a, b):
    return lax.dot_general(a, b, (((1,), (1,)), ((), ())), preferred_element_type=f32)


def _in_proj_kernel(x_ref, g_ref, wa_ref, wbt_ref, ck_ref, sk_ref, cq_ref, sq_ref,
                    xs_ref, zs_ref, k_ref, za_ref, qt_ref, vt_ref, *, bk):
    x = x_ref[0]
    r = lax.rsqrt(jnp.mean(x * x, axis=-1, keepdims=True) + NORM_EPS)
    h = (x * r * g_ref[...]).astype(bf16)

    xs_ref[0] = _dot(h, wa_ref[:, 0:512]).astype(bf16)
    zs_ref[0] = _dot(h, wa_ref[:, 512:1024]).astype(bf16)
    kk = _dot(h, wa_ref[:, 1024:1536])
    kr = _dot(h, wa_ref[:, 1536:2048])
    ck = ck_ref[...]
    sk = sk_ref[...]
    for j in range(D_ATTN // LANES):
        sl = slice(j * LANES, (j + 1) * LANES)
        k_ref[0, :, sl] = (kk[:, sl] * ck + kr[:, sl] * sk).astype(bf16)
    za_ref[0] = _dot(h, wa_ref[:, 2048:2560]).astype(bf16)

    pq = _dot_nt(wbt_ref[0:512, :], h)
    cq = cq_ref[...]
    sq = sq_ref[...]
    half = HEAD_DIM // 2
    for hb in range(D_ATTN // HEAD_DIM):
        x1 = pq[hb * HEAD_DIM:hb * HEAD_DIM + half]
        x2 = pq[hb * HEAD_DIM + half:(hb + 1) * HEAD_DIM]
        qt_ref[0, hb * HEAD_DIM:hb * HEAD_DIM + half, :] = (x1 * cq - x2 * sq).astype(bf16)
        qt_ref[0, hb * HEAD_DIM + half:(hb + 1) * HEAD_DIM, :] = (x2 * cq + x1 * sq).astype(bf16)
    pv = _dot_nt(wbt_ref[512:1024, :], h).astype(bf16)
    for c in range(pv.shape[1] // bk):
        vt_ref[0, c] = pv[:, c * bk:(c + 1) * bk]


def _in_proj(x, norm_g, wa, wbt, ck, sk, cq, sq, *, tm, bk):
    B, L, _ = x.shape
    tok = lambda b, i: (b, i, 0)
    cst = lambda b, i: (0, 0)
    out_tok = jax.ShapeDtypeStruct((B, L, 512), bf16)
    return pl.pallas_call(
        functools.partial(_in_proj_kernel, bk=bk),
        grid=(B, L // tm),
        in_specs=[
            pl.BlockSpec((1, tm, D_MODEL), tok),
            pl.BlockSpec((1, D_MODEL), cst),
            pl.BlockSpec(wa.shape, cst),
            pl.BlockSpec(wbt.shape, cst),
            pl.BlockSpec((tm, LANES), lambda b, i: (i, 0)),
            pl.BlockSpec((tm, LANES), lambda b, i: (i, 0)),
            pl.BlockSpec((HEAD_DIM // 2, tm), lambda b, i: (0, i)),
            pl.BlockSpec((HEAD_DIM // 2, tm), lambda b, i: (0, i)),
        ],
        out_specs=[
            pl.BlockSpec((1, tm, 512), tok),
            pl.BlockSpec((1, tm, 512), tok),
            pl.BlockSpec((1, tm, 512), tok),
            pl.BlockSpec((1, tm, 512), tok),
            pl.BlockSpec((1, 512, tm), lambda b, i: (b, 0, i)),
            pl.BlockSpec((1, tm // bk, 512, bk), lambda b, i: (b, i, 0, 0)),
        ],
        out_shape=[out_tok, out_tok, out_tok, out_tok,
                   jax.ShapeDtypeStruct((B, 512, L), bf16),
                   jax.ShapeDtypeStruct((B, L // bk, 512, bk), bf16)],
        compiler_params=_params("parallel", "parallel"),
        name="in_proj",
    )(x, norm_g, wa, wbt, ck, sk, cq, sq)


def _ssm_state_kernel(uf_ref, ub_ref, bf_ref, bb_ref, coef_ref, hf_ref, gb_ref, s_ref, st_ref,
                      *, nc, nb, gblk):
    @pl.when(pl.program_id(0) == 0)
    def _():
        st_ref[...] = jnp.zeros_like(st_ref)

    for g0 in range(0, N_GROUPS, gblk):
        for gi in range(gblk):
            s_ref[0, gi] = _dot(uf_ref[g0 + gi], bf_ref[g0 + gi])
            s_ref[1, gi] = _dot(ub_ref[g0 + gi], bb_ref[g0 + gi])
        gs = slice(g0, g0 + gblk)
        af1, af2, af3 = coef_ref[0, gs, 0], coef_ref[0, gs, 1], coef_ref[0, gs, 2]
        ab1, ab2, ab3 = coef_ref[1, gs, 0], coef_ref[1, gs, 1], coef_ref[1, gs, 2]

        def body(i, carry):
            hf, wf, hb, wb = carry
            rf = pl.multiple_of(i * nb, nb)
            rb = pl.multiple_of((nc - 1 - i) * nb, nb)
            hf_ref[gs, pl.ds(rf, nb), :] = hf
            gb_ref[gs, pl.ds(rb, nb), :] = hb
            sf = s_ref[0, :, pl.ds(rf, nb), :]
            sb = s_ref[1, :, pl.ds(rb, nb), :]
            hf2 = af1 * hf + af2 * wf + sf[..., :LANES]
            wf2 = af1 * wf + af3 * hf + sf[..., LANES:]
            hb2 = ab1 * hb + ab2 * wb + sb[..., :LANES]
            wb2 = ab1 * wb + ab3 * hb + sb[..., LANES:]
            return hf2, wf2, hb2, wb2

        init = (st_ref[0, 0, gs], st_ref[0, 1, gs], st_ref[1, 0, gs], st_ref[1, 1, gs])
        hf, wf, hb, wb = lax.fori_loop(0, nc, body, init)
        st_ref[0, 0, gs] = hf
        st_ref[0, 1, gs] = wf
        st_ref[1, 0, gs] = hb
        st_ref[1, 1, gs] = wb


def _ssm_state(u, bst_f, bst_b, coef, *, nb):
    G, rows, _ = u.shape
    nc = SSM_SEG_CHUNKS
    seg_rows = nc * nb
    nseg = rows // seg_rows
    gblk = SSM_GROUP_BLOCK
    fwd = lambda i: (0, i, 0)
    bwd = lambda i: (0, nseg - 1 - i, 0)
    cst3 = lambda i: (0, 0, 0)
    return pl.pallas_call(
        functools.partial(_ssm_state_kernel, nc=nc, nb=nb, gblk=gblk),
        grid=(nseg,),
        in_specs=[
            pl.BlockSpec((G, seg_rows, CHUNK_LANES), fwd),
            pl.BlockSpec((G, seg_rows, CHUNK_LANES), bwd),
            pl.BlockSpec(bst_f.shape, cst3),
            pl.BlockSpec(bst_b.shape, cst3),
            pl.BlockSpec(coef.shape, lambda i: (0, 0, 0, 0, 0)),
        ],
        out_specs=[
            pl.BlockSpec((G, seg_rows, LANES), fwd),
            pl.BlockSpec((G, seg_rows, LANES), bwd),
        ],
        out_shape=[jax.ShapeDtypeStruct((G, rows, LANES), f32),
                   jax.ShapeDtypeStruct((G, rows, LANES), f32)],
        scratch_shapes=[
            pltpu.VMEM((2, gblk, seg_rows, CHUNK_LANES), f32),
            pltpu.VMEM((2, 2, G, nb, LANES), f32),
        ],
        compiler_params=_params("arbitrary"),
        name="ssm_state",
    )(u, u, bst_f, bst_b, coef)


def _ssm_out_kernel(u_ref, hf_ref, gb_ref, m_ref, cf_ref, cb_ref, y_ref):
    for g in range(N_GROUPS):
        y = _dot(u_ref[g], m_ref[g])
        y = y + _dot(hf_ref[g].astype(bf16), cf_ref[g])
        y = y + _dot(gb_ref[g].astype(bf16), cb_ref[g])
        y_ref[g] = y.astype(bf16)


def _ssm_out(u, hf, gb, m, cst_f, cst_b):
    G, rows, _ = u.shape
    rb = min(SSM_OUT_ROWS, rows)
    blk = lambda i: (0, i, 0)
    cst3 = lambda i: (0, 0, 0)
    return pl.pallas_call(
        _ssm_out_kernel,
        grid=(rows // rb,),
        in_specs=[
            pl.BlockSpec((G, rb, CHUNK_LANES), blk),
            pl.BlockSpec((G, rb, LANES), blk),
            pl.BlockSpec((G, rb, LANES), blk),
            pl.BlockSpec(m.shape, cst3),
            pl.BlockSpec(cst_f.shape, cst3),
            pl.BlockSpec(cst_b.shape, cst3),
        ],
        out_specs=pl.BlockSpec((G, rb, CHUNK_LANES), blk),
        out_shape=jax.ShapeDtypeStruct((G, rows, CHUNK_LANES), bf16),
        compiler_params=_params("parallel"),
        name="ssm_out",
    )(u, hf, gb, m, cst_f, cst_b)


def _ssm_tables(a_re, a_im, log_dt, b_re, b_im, c_re, c_im, d_skip):
    T, G, P, C = CHUNK, N_GROUPS, STATE, SSM_GROUP
    a_re, a_im, log_dt = a_re.astype(f32), a_im.astype(f32), log_dt.astype(f32)
    dt = jnp.exp(log_dt)[..., None]
    ks = jnp.arange(T + 1, dtype=f32)[:, None, None, None]
    mag = jnp.exp(ks * (a_re * dt))
    pw_re = mag * jnp.cos(ks * (a_im * dt))
    pw_im = mag * jnp.sin(ks * (a_im * dt))
    n_re, n_im = pw_re[1] - 1.0, pw_im[1]
    den = a_re * a_re + a_im * a_im
    co_re = (n_re * a_re + n_im * a_im) / den
    co_im = (n_im * a_re - n_re * a_im) / den
    b_re, b_im = b_re.astype(f32), b_im.astype(f32)
    bb_re = co_re[..., None] * b_re - co_im[..., None] * b_im
    bb_im = co_re[..., None] * b_im + co_im[..., None] * b_re
    c_re, c_im = c_re.astype(f32), c_im.astype(f32)
    cp_re = c_re[None] * pw_re[:, :, :, None, :] - c_im[None] * pw_im[:, :, :, None, :]
    cp_im = c_re[None] * pw_im[:, :, :, None, :] + c_im[None] * pw_re[:, :, :, None, :]
    kmat = (jnp.einsum('kngcp,ngpd->kngcd', cp_re[:T], bb_re)
            - jnp.einsum('kngcp,ngpd->kngcd', cp_im[:T], bb_im))
    j = jnp.arange(T)[:, None]
    t = jnp.arange(T)[None, :]
    lag = t - j
    kf = jnp.where((lag >= 0)[:, :, None, None, None], kmat[jnp.clip(lag, 0, T - 1), 0], 0.0)
    kb = jnp.where((lag <= 0)[:, :, None, None, None], kmat[jnp.clip(-lag, 0, T - 1), 1], 0.0)
    m5 = (kf + kb).transpose(2, 0, 4, 1, 3)
    eye = (jnp.eye(T, dtype=f32)[:, None, :, None] * jnp.eye(C, dtype=f32)[None, :, None, :])
    m5 = m5 + eye[None] * d_skip.astype(f32).reshape(G, 1, 1, 1, C)
    m = m5.reshape(G, T * C, T * C)

    def bst(n, powers):
        pr, pi = pw_re[powers, n], pw_im[powers, n]
        re = pr[..., None] * bb_re[n][None] - pi[..., None] * bb_im[n][None]
        im = pr[..., None] * bb_im[n][None] + pi[..., None] * bb_re[n][None]
        re = re.transpose(1, 0, 3, 2).reshape(G, T * C, P)
        im = im.transpose(1, 0, 3, 2).reshape(G, T * C, P)
        return jnp.concatenate([re, im, im, re], axis=-1)

    bst_f = bst(0, T - 1 - jnp.arange(T))
    bst_b = bst(1, jnp.arange(T))

    def cst(n, powers):
        re = cp_re[powers, n]
        im = cp_im[powers, n]
        re = re.transpose(1, 3, 0, 2).reshape(G, P, T * C)
        im = im.transpose(1, 3, 0, 2).reshape(G, P, T * C)
        return jnp.concatenate([re, -im], axis=1)

    cst_f = cst(0, jnp.arange(T) + 1)
    cst_b = cst(1, T - jnp.arange(T))
    ar, ai = pw_re[T], pw_im[T]
    coef = jnp.stack([jnp.concatenate([ar, ar], -1),
                      jnp.concatenate([-ai, ai], -1),
                      jnp.concatenate([ai, -ai], -1)], axis=2)
    return m.astype(bf16), bst_f.astype(bf16), bst_b.astype(bf16), cst_f.astype(bf16), cst_b.astype(bf16), coef


def _attn_kernel(lam_ref, qt_ref, k_ref, vt_ref, g_ref, za_ref, o_ref, qbd_ref, acc_ref, *, nkv, bq, bk):
    lam = lam_ref[0]
    half = HEAD_DIM
    for st in range(bq // LANES):
        qs = qt_ref[0, :, st * LANES:(st + 1) * LANES]
        zero = jnp.zeros((half, LANES), bf16)
        qbd_ref[0:half, 0:LANES] = qs[0:half]
        qbd_ref[0:half, LANES:2 * LANES] = zero
        qbd_ref[half:2 * half, 0:LANES] = zero
        qbd_ref[half:2 * half, LANES:2 * LANES] = qs[half:2 * half]
        acc_ref[...] = jnp.zeros_like(acc_ref)

        def body(j, carry):
            m, l = carry
            off = pl.multiple_of(j * bk, bk)
            s = _dot(k_ref[0, pl.ds(off, bk), :], qbd_ref[...])
            m_new = jnp.maximum(m, jnp.max(s, axis=0, keepdims=True))
            p = jnp.exp(s - m_new)
            alpha = jnp.exp(m - m_new)
            l_new = alpha * l + jnp.sum(p, axis=0, keepdims=True)
            acc_ref[...] = alpha * acc_ref[...] + _dot(vt_ref[0, j], p.astype(bf16))
            return m_new, l_new

        m0 = jnp.full((1, 2 * LANES), -jnp.inf, f32)
        l0 = jnp.zeros((1, 2 * LANES), f32)
        _, l = lax.fori_loop(0, nkv, body, (m0, l0))

        acc = acc_ref[...]
        inv = 1.0 / l
        ot = acc[:, :LANES] * inv[:, :LANES] - lam * (acc[:, LANES:] * inv[:, LANES:])
        o = ot.T
        y = o * lax.rsqrt(jnp.mean(o * o, axis=-1, keepdims=True) + SUBLN_EPS) * g_ref[...] * (1.0 - LAM_INIT)
        za = za_ref[0, st * LANES:(st + 1) * LANES, :].astype(f32)
        o_ref[0, st * LANES:(st + 1) * LANES, :] = (y * (za * jax.nn.sigmoid(za))).astype(bf16)


def _attention(lam, qt, k, vt, subln_g, za, *, bq, bk):
    B, L, _ = k.shape
    nkv = L // bk
    return pl.pallas_call(
        functools.partial(_attn_kernel, nkv=nkv, bq=bq, bk=bk),
        grid=(B, N_HEADS, L // bq),
        in_specs=[
            pl.BlockSpec(memory_space=pltpu.SMEM),
            pl.BlockSpec((1, 2 * HEAD_DIM, bq), lambda b, h, i: (b, h, i)),
            pl.BlockSpec((1, L, 2 * HEAD_DIM), lambda b, h, i: (b, 0, h)),
            pl.BlockSpec((1, nkv, 2 * HEAD_DIM, bk), lambda b, h, i: (b, 0, h, 0)),
            pl.BlockSpec((1, 2 * HEAD_DIM), lambda b, h, i: (0, 0)),
            pl.BlockSpec((1, bq, 2 * HEAD_DIM), lambda b, h, i: (b, i, h)),
        ],
        out_specs=pl.BlockSpec((1, bq, 2 * HEAD_DIM), lambda b, h, i: (b, i, h)),
        out_shape=jax.ShapeDtypeStruct((B, L, D_ATTN), bf16),
        scratch_shapes=[
            pltpu.VMEM((2 * HEAD_DIM, 2 * LANES), bf16),
            pltpu.VMEM((2 * HEAD_DIM, 2 * LANES), f32),
        ],
        compiler_params=_params("parallel", "parallel", "arbitrary"),
        name="attention",
    )(lam, qt, k, vt, subln_g, za)


def _out_proj_kernel(x_ref, yssm_ref, zs_ref, ya_ref, g_ref, wg_ref, wglu_ref, bglu_ref,
                     wb_ref, wout_ref, fg_ref, o_ref):
    x = x_ref[0]
    r = lax.rsqrt(jnp.mean(x * x, axis=-1, keepdims=True) + NORM_EPS)
    h = (x * r * g_ref[...]).astype(bf16)

    ys = jax.nn.gelu(yssm_ref[0].astype(f32))
    ys = ys * jax.nn.sigmoid(_dot(ys.astype(bf16), wglu_ref[...]) + bglu_ref[...])
    zs = zs_ref[0].astype(f32)
    ys = ys * (zs * jax.nn.sigmoid(zs))

    ps = _dot(ys.astype(bf16), wb_ref[0])
    merged = jax.nn.sigmoid(_dot(h, wg_ref[:, 0:D_MODEL])) * ps
    pa = _dot(ya_ref[0], wb_ref[1])
    merged = merged + jax.nn.sigmoid(_dot(h, wg_ref[:, D_MODEL:2 * D_MODEL])) * pa
    out = x + _dot(merged.astype(bf16), wout_ref[...])
    o_ref[0] = out * lax.rsqrt(jnp.mean(out * out, axis=-1, keepdims=True) + NORM_EPS) * fg_ref[...]


def _out_proj(x, yssm, zs, ya, norm_g, wg, wglu, bglu, wb, wout, final_g, *, tm):
    B, L, _ = x.shape
    tok = lambda b, i: (b, i, 0)
    cst = lambda b, i: (0, 0)
    return pl.pallas_call(
        _out_proj_kernel,
        grid=(B, L // tm),
        in_specs=[
            pl.BlockSpec((1, tm, D_MODEL), tok),
            pl.BlockSpec((1, tm, 512), tok),
            pl.BlockSpec((1, tm, 512), tok),
            pl.BlockSpec((1, tm, 512), tok),
            pl.BlockSpec((1, D_MODEL), cst),
            pl.BlockSpec(wg.shape, cst),
            pl.BlockSpec(wglu.shape, cst),
            pl.BlockSpec((1, D_SSM), cst),
            pl.BlockSpec(wb.shape, lambda b, i: (0, 0, 0)),
            pl.BlockSpec(wout.shape, cst),
            pl.BlockSpec((1, D_MODEL), cst),
        ],
        out_specs=pl.BlockSpec((1, tm, D_MODEL), tok),
        out_shape=jax.ShapeDtypeStruct((B, L, D_MODEL), x.dtype),
        compiler_params=_params("parallel", "parallel"),
        name="out_proj",
    )(x, yssm, zs, ya, norm_g, wg, wglu, bglu, wb, wout, final_g)


def _rotary_tables(L):
    half = HEAD_DIM // 2
    inv_freq = 1.0 / (ROPE_THETA ** (jnp.arange(0, half, dtype=f32) * 2.0 / HEAD_DIM))
    ang = jnp.arange(L, dtype=f32)[:, None] * inv_freq[None, :]
    cos, sin = jnp.cos(ang), jnp.sin(ang)
    ck = jnp.tile(cos, (1, LANES // half))
    sk = jnp.tile(sin, (1, LANES // half))
    scale = 1.0 / math.sqrt(HEAD_DIM)
    return ck, sk, cos.T * scale, sin.T * scale


def _trunk(x, w):
    B, L, _ = x.shape
    tm = min(TOKEN_BLOCK, L)
    bk = min(ATTN_BK, L)
    bq = min(ATTN_BQ, L)
    ck, sk, cq, sq = _rotary_tables(L)
    xs, zs, k, za, qt, vt = _in_proj(x, w["norm_g"], w["wa"], w["wbt"], ck, sk, cq, sq, tm=tm, bk=bk)

    nck = L // CHUNK
    u = xs.reshape(B, nck, CHUNK, N_GROUPS, SSM_GROUP).transpose(3, 1, 0, 2, 4).reshape(N_GROUPS, nck * B, CHUNK_LANES)
    hf, gb = _ssm_state(u, w["bst_f"], w["bst_b"], w["coef"], nb=B)
    y = _ssm_out(u, hf, gb, w["m"], w["cst_f"], w["cst_b"])
    yssm = y.reshape(N_GROUPS, nck, B, CHUNK, SSM_GROUP).transpose(2, 1, 3, 0, 4).reshape(B, L, D_SSM)

    ya = _attention(w["lam"], qt, k, vt, w["subln_g"], za, bq=bq, bk=bk)
    return _out_proj(x, yssm, zs, ya, w["norm_g"], w["wg"], w["wglu"], w["bglu"], w["wb"], w["wout"],
                     w["final_g"], tm=tm)


def _rotate_half_columns(wk):
    d = wk.shape[0]
    w4 = wk.reshape(d, D_ATTN // HEAD_DIM, 2, HEAD_DIM // 2)
    return jnp.stack([-w4[:, :, 1], w4[:, :, 0]], axis=2).reshape(d, D_ATTN)


def kernel(x_prompt, x_sample, norm_g, w_in, ssm_a_re, ssm_a_im, ssm_log_dt, ssm_b_re, ssm_b_im, ssm_c_re, ssm_c_im, ssm_d, w_glu, b_glu, lambda_q1, lambda_k1, lambda_q2, lambda_k2, subln_g, w_branch, w_out, final_g):
    li = 0
    wi = w_in[li].astype(f32)
    w_xs, w_zs = wi[:, 0:512], wi[:, 512:1024]
    w_q, w_k, w_v, w_za = wi[:, 1024:1536], wi[:, 1536:2048], wi[:, 2048:2560], wi[:, 2560:3072]
    m, bst_f, bst_b, cst_f, cst_b, coef = _ssm_tables(
        ssm_a_re[li], ssm_a_im[li], ssm_log_dt[li], ssm_b_re[li], ssm_b_im[li],
        ssm_c_re[li], ssm_c_im[li], ssm_d[li])
    nb = x_prompt.shape[0]
    lam = (jnp.exp(jnp.sum(lambda_q1[li].astype(f32) * lambda_k1[li].astype(f32)))
           - jnp.exp(jnp.sum(lambda_q2[li].astype(f32) * lambda_k2[li].astype(f32))) + LAM_INIT)
    w = dict(
        norm_g=norm_g[li].astype(f32).reshape(1, D_MODEL),
        wa=jnp.concatenate([w_xs, w_zs, w_k, _rotate_half_columns(w_k), w_za], axis=1).astype(bf16),
        wbt=jnp.concatenate([w_q, w_v], axis=1).T.astype(bf16),
        wg=wi[:, 3072:5120].astype(bf16),
        m=m, bst_f=bst_f, bst_b=bst_b, cst_f=cst_f, cst_b=cst_b,
        coef=jnp.broadcast_to(coef[:, :, :, None, :], (2, N_GROUPS, 3, nb, LANES)),
        lam=lam.reshape(1).astype(f32),
        subln_g=subln_g[li].astype(f32).reshape(1, 2 * HEAD_DIM),
        wglu=w_glu[li].astype(bf16),
        bglu=b_glu[li].astype(f32).reshape(1, D_SSM),
        wb=w_branch[li].astype(bf16),
        wout=w_out[li].astype(bf16),
        final_g=final_g.astype(f32).reshape(1, D_MODEL),
    )
    return (_trunk(x_prompt, w), _trunk(x_sample, w))
```

```python
import functools
import math

import jax
import jax.numpy as jnp
from jax import lax
from jax.experimental import pallas as pl
from jax.experimental.pallas import tpu as pltpu

D_MODEL = 1024
D_SSM = 512
SSM_GROUP = 16
N_GROUPS = 32
STATE = 64
D_ATTN = 512
N_HEADS = 4
HEAD_DIM = 64
ROPE_THETA = 10000.0
NORM_EPS = 1e-6
SUBLN_EPS = 1e-5
LAM_INIT = 0.8 - 0.6 * math.exp(-0.3 * 0)

CHUNK = 16
CHUNK_LANES = CHUNK * SSM_GROUP
LANES = 128
VMEM_LIMIT = 56 * 1024 * 1024

TOKEN_BLOCK = 512
ATTN_BQ = 1024
ATTN_BK = 128
SSM_SEG_CHUNKS = 16
SSM_GROUP_BLOCK = 4
SSM_OUT_ROWS = 256

f32 = jnp.float32
bf16 = jnp.bfloat16


def _params(*sem):
    return pltpu.CompilerParams(dimension_semantics=sem, vmem_limit_bytes=VMEM_LIMIT)


def _dot(a, b):
    return jnp.dot(a, b, preferred_element_type=f32)


def _dot_nt(a, b):
    return lax.dot_general(a, b, (((1,), (1,)), ((), ())), preferred_element_type=f32)


def _in_proj_kernel(x_ref, g_ref, wa_ref, wbt_ref, ck_ref, sk_ref, cq_ref, sq_ref,
                    xs_ref, zs_ref, k_ref, za_ref, qt_ref, vt_ref, *, bk):
    x = x_ref[0]
    r = lax.rsqrt(jnp.mean(x * x, axis=-1, keepdims=True) + NORM_EPS)
    h = (x * r * g_ref[...]).astype(bf16)

    xs_ref[0] = _dot(h, wa_ref[:, 0:512]).astype(bf16)
    zs_ref[0] = _dot(h, wa_ref[:, 512:1024]).astype(bf16)
    kk = _dot(h, wa_ref[:, 1024:1536])
    kr = _dot(h, wa_ref[:, 1536:2048])
    ck = ck_ref[...]
    sk = sk_ref[...]
    for j in range(D_ATTN // LANES):
        sl = slice(j * LANES, (j + 1) * LANES)
        k_ref[0, :, sl] = (kk[:, sl] * ck + kr[:, sl] * sk).astype(bf16)
    za_ref[0] = _dot(h, wa_ref[:, 2048:2560]).astype(bf16)

    pq = _dot_nt(wbt_ref[0:512, :], h)
    cq = cq_ref[...]
    sq = sq_ref[...]
    half = HEAD_DIM // 2
    for hb in range(D_ATTN // HEAD_DIM):
        x1 = pq[hb * HEAD_DIM:hb * HEAD_DIM + half]
        x2 = pq[hb * HEAD_DIM + half:(hb + 1) * HEAD_DIM]
        qt_ref[0, hb * HEAD_DIM:hb * HEAD_DIM + half, :] = (x1 * cq - x2 * sq).astype(bf16)
        qt_ref[0, hb * HEAD_DIM + half:(hb + 1) * HEAD_DIM, :] = (x2 * cq + x1 * sq).astype(bf16)
    pv = _dot_nt(wbt_ref[512:1024, :], h).astype(bf16)
    for c in range(pv.shape[1] // bk):
        vt_ref[0, c] = pv[:, c * bk:(c + 1) * bk]


def _in_proj(x, norm_g, wa, wbt, ck, sk, cq, sq, *, tm, bk):
    B, L, _ = x.shape
    tok = lambda b, i: (b, i, 0)
    cst = lambda b, i: (0, 0)
    out_tok = jax.ShapeDtypeStruct((B, L, 512), bf16)
    return pl.pallas_call(
        functools.partial(_in_proj_kernel, bk=bk),
        grid=(B, L // tm),
        in_specs=[
            pl.BlockSpec((1, tm, D_MODEL), tok),
            pl.BlockSpec((1, D_MODEL), cst),
            pl.BlockSpec(wa.shape, cst),
            pl.BlockSpec(wbt.shape, cst),
            pl.BlockSpec((tm, LANES), lambda b, i: (i, 0)),
            pl.BlockSpec((tm, LANES), lambda b, i: (i, 0)),
            pl.BlockSpec((HEAD_DIM // 2, tm), lambda b, i: (0, i)),
            pl.BlockSpec((HEAD_DIM // 2, tm), lambda b, i: (0, i)),
        ],
        out_specs=[
            pl.BlockSpec((1, tm, 512), tok),
            pl.BlockSpec((1, tm, 512), tok),
            pl.BlockSpec((1, tm, 512), tok),
            pl.BlockSpec((1, tm, 512), tok),
            pl.BlockSpec((1, 512, tm), lambda b, i: (b, 0, i)),
            pl.BlockSpec((1, tm // bk, 512, bk), lambda b, i: (b, i, 0, 0)),
        ],
        out_shape=[out_tok, out_tok, out_tok, out_tok,
                   jax.ShapeDtypeStruct((B, 512, L), bf16),
                   jax.ShapeDtypeStruct((B, L // bk, 512, bk), bf16)],
        compiler_params=_params("parallel", "parallel"),
        name="in_proj",
    )(x, norm_g, wa, wbt, ck, sk, cq, sq)


def _ssm_state_kernel(uf_ref, ub_ref, bf_ref, bb_ref, coef_ref, hf_ref, gb_ref, s_ref, st_ref,
                      *, nc, nb, gblk):
    @pl.when(pl.program_id(0) == 0)
    def _():
        st_ref[...] = jnp.zeros_like(st_ref)

    for g0 in range(0, N_GROUPS, gblk):
        for gi in range(gblk):
            s_ref[0, gi] = _dot(uf_ref[g0 + gi], bf_ref[g0 + gi])
            s_ref[1, gi] = _dot(ub_ref[g0 + gi], bb_ref[g0 + gi])
        gs = slice(g0, g0 + gblk)
        af1, af2, af3 = coef_ref[0, gs, 0], coef_ref[0, gs, 1], coef_ref[0, gs, 2]
        ab1, ab2, ab3 = coef_ref[1, gs, 0], coef_ref[1, gs, 1], coef_ref[1, gs, 2]

        def body(i, carry):
            hf, wf, hb, wb = carry
            rf = pl.multiple_of(i * nb, nb)
            rb = pl.multiple_of((nc - 1 - i) * nb, nb)
            hf_ref[gs, pl.ds(rf, nb), :] = hf
            gb_ref[gs, pl.ds(rb, nb), :] = hb
            sf = s_ref[0, :, pl.ds(rf, nb), :]
            sb = s_ref[1, :, pl.ds(rb, nb), :]
            hf2 = af1 * hf + af2 * wf + sf[..., :LANES]
            wf2 = af1 * wf + af3 * hf + sf[..., LANES:]
            hb2 = ab1 * hb + ab2 * wb + sb[..., :LANES]
            wb2 = ab1 * wb + ab3 * hb + sb[..., LANES:]
            return hf2, wf2, hb2, wb2

        init = (st_ref[0, 0, gs], st_ref[0, 1, gs], st_ref[1, 0, gs], st_ref[1, 1, gs])
        hf, wf, hb, wb = lax.fori_loop(0, nc, body, init)
        st_ref[0, 0, gs] = hf
        st_ref[0, 1, gs] = wf
        st_ref[1, 0, gs] = hb
        st_ref[1, 1, gs] = wb


def _ssm_state(u, bst_f, bst_b, coef, *, nb):
    G, rows, _ = u.shape
    nc = SSM_SEG_CHUNKS
    seg_rows = nc * nb
    nseg = rows // seg_rows
    gblk = SSM_GROUP_BLOCK
    fwd = lambda i: (0, i, 0)
    bwd = lambda i: (0, nseg - 1 - i, 0)
    cst3 = lambda i: (0, 0, 0)
    return pl.pallas_call(
        functools.partial(_ssm_state_kernel, nc=nc, nb=nb, gblk=gblk),
        grid=(nseg,),
        in_specs=[
            pl.BlockSpec((G, seg_rows, CHUNK_LANES), fwd),
            pl.BlockSpec((G, seg_rows, CHUNK_LANES), bwd),
            pl.BlockSpec(bst_f.shape, cst3),
            pl.BlockSpec(bst_b.shape, cst3),
            pl.BlockSpec(coef.shape, lambda i: (0, 0, 0, 0, 0)),
        ],
        out_specs=[
            pl.BlockSpec((G, seg_rows, LANES), fwd),
            pl.BlockSpec((G, seg_rows, LANES), bwd),
        ],
        out_shape=[jax.ShapeDtypeStruct((G, rows, LANES), f32),
                   jax.ShapeDtypeStruct((G, rows, LANES), f32)],
        scratch_shapes=[
            pltpu.VMEM((2, gblk, seg_rows, CHUNK_LANES), f32),
            pltpu.VMEM((2, 2, G, nb, LANES), f32),
        ],
        compiler_params=_params("arbitrary"),
        name="ssm_state",
    )(u, u, bst_f, bst_b, coef)


def _ssm_out_kernel(u_ref, hf_ref, gb_ref, m_ref, cf_ref, cb_ref, y_ref):
    for g in range(N_GROUPS):
        y = _dot(u_ref[g], m_ref[g])
        y = y + _dot(hf_ref[g].astype(bf16), cf_ref[g])
        y = y + _dot(gb_ref[g].astype(bf16), cb_ref[g])
        y_ref[g] = y.astype(bf16)


def _ssm_out(u, hf, gb, m, cst_f, cst_b):
    G, rows, _ = u.shape
    rb = min(SSM_OUT_ROWS, rows)
    blk = lambda i: (0, i, 0)
    cst3 = lambda i: (0, 0, 0)
    return pl.pallas_call(
        _ssm_out_kernel,
        grid=(rows // rb,),
        in_specs=[
            pl.BlockSpec((G, rb, CHUNK_LANES), blk),
            pl.BlockSpec((G, rb, LANES), blk),
            pl.BlockSpec((G, rb, LANES), blk),
            pl.BlockSpec(m.shape, cst3),
            pl.BlockSpec(cst_f.shape, cst3),
            pl.BlockSpec(cst_b.shape, cst3),
        ],
        out_specs=pl.BlockSpec((G, rb, CHUNK_LANES), blk),
        out_shape=jax.ShapeDtypeStruct((G, rows, CHUNK_LANES), bf16),
        compiler_params=_params("parallel"),
        name="ssm_out",
    )(u, hf, gb, m, cst_f, cst_b)


def _ssm_tables(a_re, a_im, log_dt, b_re, b_im, c_re, c_im, d_skip):
    T, G, P, C = CHUNK, N_GROUPS, STATE, SSM_GROUP
    a_re, a_im, log_dt = a_re.astype(f32), a_im.astype(f32), log_dt.astype(f32)
    dt = jnp.exp(log_dt)[..., None]
    ks = jnp.arange(T + 1, dtype=f32)[:, None, None, None]
    mag = jnp.exp(ks * (a_re * dt))
    pw_re = mag * jnp.cos(ks * (a_im * dt))
    pw_im = mag * jnp.sin(ks * (a_im * dt))
    n_re, n_im = pw_re[1] - 1.0, pw_im[1]
    den = a_re * a_re + a_im * a_im
    co_re = (n_re * a_re + n_im * a_im) / den
    co_im = (n_im * a_re - n_re * a_im) / den
    b_re, b_im = b_re.astype(f32), b_im.astype(f32)
    bb_re = co_re[..., None] * b_re - co_im[..., None] * b_im
    bb_im = co_re[..., None] * b_im + co_im[..., None] * b_re
    c_re, c_im = c_re.astype(f32), c_im.astype(f32)
    cp_re = c_re[None] * pw_re[:, :, :, None, :] - c_im[None] * pw_im[:, :, :, None, :]
    cp_im = c_re[None] * pw_im[:, :, :, None, :] + c_im[None] * pw_re[:, :, :, None, :]
    kmat = (jnp.einsum('kngcp,ngpd->kngcd', cp_re[:T], bb_re)
            - jnp.einsum('kngcp,ngpd->kngcd', cp_im[:T], bb_im))
    j = jnp.arange(T)[:, None]
    t = jnp.arange(T)[None, :]
    lag = t - j
    kf = jnp.where((lag >= 0)[:, :, None, None, None], kmat[jnp.clip(lag, 0, T - 1), 0], 0.0)
    kb = jnp.where((lag <= 0)[:, :, None, None, None], kmat[jnp.clip(-lag, 0, T - 1), 1], 0.0)
    m5 = (kf + kb).transpose(2, 0, 4, 1, 3)
    eye = (jnp.eye(T, dtype=f32)[:, None, :, None] * jnp.eye(C, dtype=f32)[None, :, None, :])
    m5 = m5 + eye[None] * d_skip.astype(f32).reshape(G, 1, 1, 1, C)
    m = m5.reshape(G, T * C, T * C)

    def bst(n, powers):
        pr, pi = pw_re[powers, n], pw_im[powers, n]
        re = pr[..., None] * bb_re[n][None] - pi[..., None] * bb_im[n][None]
        im = pr[..., None] * bb_im[n][None] + pi[..., None] * bb_re[n][None]
        re = re.transpose(1, 0, 3, 2).reshape(G, T * C, P)
        im = im.transpose(1, 0, 3, 2).reshape(G, T * C, P)
        return jnp.concatenate([re, im, im, re], axis=-1)

    bst_f = bst(0, T - 1 - jnp.arange(T))
    bst_b = bst(1, jnp.arange(T))

    def cst(n, powers):
        re = cp_re[powers, n]
        im = cp_im[powers, n]
        re = re.transpose(1, 3, 0, 2).reshape(G, P, T * C)
        im = im.transpose(1, 3, 0, 2).reshape(G, P, T * C)
        return jnp.concatenate([re, -im], axis=1)

    cst_f = cst(0, jnp.arange(T) + 1)
    cst_b = cst(1, T - jnp.arange(T))
    ar, ai = pw_re[T], pw_im[T]
    coef = jnp.stack([jnp.concatenate([ar, ar], -1),
                      jnp.concatenate([-ai, ai], -1),
                      jnp.concatenate([ai, -ai], -1)], axis=2)
    return m.astype(bf16), bst_f.astype(bf16), bst_b.astype(bf16), cst_f.astype(bf16), cst_b.astype(bf16), coef


def _attn_kernel(lam_ref, qt_ref, k_ref, vt_ref, g_ref, za_ref, o_ref,
                 qbd_ref, acc_ref, p_ref, m_ref, l_ref, a_ref, *, nkv, bq, bk):
    lam = lam_ref[0]
    half = HEAD_DIM
    ns = bq // LANES
    zero = jnp.zeros((half, LANES), bf16)
    for st in range(ns):
        qs = qt_ref[0, :, st * LANES:(st + 1) * LANES]
        qbd_ref[st, 0:half, 0:LANES] = qs[0:half]
        qbd_ref[st, 0:half, LANES:2 * LANES] = zero
        qbd_ref[st, half:2 * half, 0:LANES] = zero
        qbd_ref[st, half:2 * half, LANES:2 * LANES] = qs[half:2 * half]
    acc_ref[...] = jnp.zeros_like(acc_ref)
    p_ref[...] = jnp.zeros_like(p_ref)
    a_ref[...] = jnp.ones_like(a_ref)
    l_ref[...] = jnp.zeros_like(l_ref)
    m_ref[...] = jnp.full(m_ref.shape, -jnp.inf, f32)

    def fold(st, vb):
        acc_ref[st] = a_ref[st] * acc_ref[st] + _dot(vb, p_ref[st])

    def body(j, carry):
        off = pl.multiple_of(j * bk, bk)
        kb = k_ref[0, pl.ds(off, bk), :]
        vb = vt_ref[0, jnp.maximum(j - 1, 0)]
        for st in range(ns):
            fold(st, vb)
            s = _dot(kb, qbd_ref[st])
            m_old = m_ref[st]
            m_new = jnp.maximum(m_old, jnp.max(s, axis=0, keepdims=True))
            p = jnp.exp2(s - m_new)
            alpha = jnp.exp2(m_old - m_new)
            l_ref[st] = alpha * l_ref[st] + jnp.sum(p, axis=0, keepdims=True)
            m_ref[st] = m_new
            a_ref[st] = alpha
            p_ref[st] = p.astype(bf16)
        return carry

    lax.fori_loop(0, nkv, body, 0, unroll=2)

    vb_last = vt_ref[0, nkv - 1]
    for st in range(ns):
        fold(st, vb_last)
        acc = acc_ref[st]
        inv = 1.0 / l_ref[st]
        ot = acc[:, :LANES] * inv[:, :LANES] - lam * (acc[:, LANES:] * inv[:, LANES:])
        o = ot.T
        y = o * lax.rsqrt(jnp.mean(o * o, axis=-1, keepdims=True) + SUBLN_EPS) * g_ref[...] * (1.0 - LAM_INIT)
        za = za_ref[0, st * LANES:(st + 1) * LANES, :].astype(f32)
        o_ref[0, st * LANES:(st + 1) * LANES, :] = (y * (za * jax.nn.sigmoid(za))).astype(bf16)


def _attention(lam, qt, k, vt, subln_g, za, *, bq, bk):
    B, L, _ = k.shape
    nkv = L // bk
    return pl.pallas_call(
        functools.partial(_attn_kernel, nkv=nkv, bq=bq, bk=bk),
        grid=(B, N_HEADS, L // bq),
        in_specs=[
            pl.BlockSpec(memory_space=pltpu.SMEM),
            pl.BlockSpec((1, 2 * HEAD_DIM, bq), lambda b, h, i: (b, h, i)),
            pl.BlockSpec((1, L, 2 * HEAD_DIM), lambda b, h, i: (b, 0, h)),
            pl.BlockSpec((1, nkv, 2 * HEAD_DIM, bk), lambda b, h, i: (b, 0, h, 0)),
            pl.BlockSpec((1, 2 * HEAD_DIM), lambda b, h, i: (0, 0)),
            pl.BlockSpec((1, bq, 2 * HEAD_DIM), lambda b, h, i: (b, i, h)),
        ],
        out_specs=pl.BlockSpec((1, bq, 2 * HEAD_DIM), lambda b, h, i: (b, i, h)),
        out_shape=jax.ShapeDtypeStruct((B, L, D_ATTN), bf16),
        scratch_shapes=[
            pltpu.VMEM((bq // LANES, 2 * HEAD_DIM, 2 * LANES), bf16),
            pltpu.VMEM((bq // LANES, 2 * HEAD_DIM, 2 * LANES), f32),
            pltpu.VMEM((bq // LANES, bk, 2 * LANES), bf16),
            pltpu.VMEM((bq // LANES, 1, 2 * LANES), f32),
            pltpu.VMEM((bq // LANES, 1, 2 * LANES), f32),
            pltpu.VMEM((bq // LANES, 1, 2 * LANES), f32),
        ],
        compiler_params=_params("parallel", "parallel", "arbitrary"),
        name="attention",
    )(lam, qt, k, vt, subln_g, za)


def _out_proj_kernel(x_ref, yssm_ref, zs_ref, ya_ref, g_ref, wg_ref, wglu_ref, bglu_ref,
                     wb_ref, wout_ref, fg_ref, o_ref):
    x = x_ref[0]
    r = lax.rsqrt(jnp.mean(x * x, axis=-1, keepdims=True) + NORM_EPS)
    h = (x * r * g_ref[...]).astype(bf16)

    ys = jax.nn.gelu(yssm_ref[0].astype(f32))
    ys = ys * jax.nn.sigmoid(_dot(ys.astype(bf16), wglu_ref[...]) + bglu_ref[...])
    zs = zs_ref[0].astype(f32)
    ys = ys * (zs * jax.nn.sigmoid(zs))

    ps = _dot(ys.astype(bf16), wb_ref[0])
    merged = jax.nn.sigmoid(_dot(h, wg_ref[:, 0:D_MODEL])) * ps
    pa = _dot(ya_ref[0], wb_ref[1])
    merged = merged + jax.nn.sigmoid(_dot(h, wg_ref[:, D_MODEL:2 * D_MODEL])) * pa
    out = x + _dot(merged.astype(bf16), wout_ref[...])
    o_ref[0] = out * lax.rsqrt(jnp.mean(out * out, axis=-1, keepdims=True) + NORM_EPS) * fg_ref[...]


def _out_proj(x, yssm, zs, ya, norm_g, wg, wglu, bglu, wb, wout, final_g, *, tm):
    B, L, _ = x.shape
    tok = lambda b, i: (b, i, 0)
    cst = lambda b, i: (0, 0)
    return pl.pallas_call(
        _out_proj_kernel,
        grid=(B, L // tm),
        in_specs=[
            pl.BlockSpec((1, tm, D_MODEL), tok),
            pl.BlockSpec((1, tm, 512), tok),
            pl.BlockSpec((1, tm, 512), tok),
            pl.BlockSpec((1, tm, 512), tok),
            pl.BlockSpec((1, D_MODEL), cst),
            pl.BlockSpec(wg.shape, cst),
            pl.BlockSpec(wglu.shape, cst),
            pl.BlockSpec((1, D_SSM), cst),
            pl.BlockSpec(wb.shape, lambda b, i: (0, 0, 0)),
            pl.BlockSpec(wout.shape, cst),
            pl.BlockSpec((1, D_MODEL), cst),
        ],
        out_specs=pl.BlockSpec((1, tm, D_MODEL), tok),
        out_shape=jax.ShapeDtypeStruct((B, L, D_MODEL), x.dtype),
        compiler_params=_params("parallel", "parallel"),
        name="out_proj",
    )(x, yssm, zs, ya, norm_g, wg, wglu, bglu, wb, wout, final_g)


def _rotary_tables(L):
    half = HEAD_DIM // 2
    inv_freq = 1.0 / (ROPE_THETA ** (jnp.arange(0, half, dtype=f32) * 2.0 / HEAD_DIM))
    ang = jnp.arange(L, dtype=f32)[:, None] * inv_freq[None, :]
    cos, sin = jnp.cos(ang), jnp.sin(ang)
    ck = jnp.tile(cos, (1, LANES // half))
    sk = jnp.tile(sin, (1, LANES // half))
    scale = math.log2(math.e) / math.sqrt(HEAD_DIM)
    return ck, sk, cos.T * scale, sin.T * scale


def _trunk(x, w):
    B, L, _ = x.shape
    tm = min(TOKEN_BLOCK, L)
    bk = min(ATTN_BK, L)
    bq = min(ATTN_BQ, L)
    ck, sk, cq, sq = _rotary_tables(L)
    xs, zs, k, za, qt, vt = _in_proj(x, w["norm_g"], w["wa"], w["wbt"], ck, sk, cq, sq, tm=tm, bk=bk)

    nck = L // CHUNK
    u = xs.reshape(B, nck, CHUNK, N_GROUPS, SSM_GROUP).transpose(3, 1, 0, 2, 4).reshape(N_GROUPS, nck * B, CHUNK_LANES)
    hf, gb = _ssm_state(u, w["bst_f"], w["bst_b"], w["coef"], nb=B)
    y = _ssm_out(u, hf, gb, w["m"], w["cst_f"], w["cst_b"])
    yssm = y.reshape(N_GROUPS, nck, B, CHUNK, SSM_GROUP).transpose(2, 1, 3, 0, 4).reshape(B, L, D_SSM)

    ya = _attention(w["lam"], qt, k, vt, w["subln_g"], za, bq=bq, bk=bk)
    return _out_proj(x, yssm, zs, ya, w["norm_g"], w["wg"], w["wglu"], w["bglu"], w["wb"], w["wout"],
                     w["final_g"], tm=tm)


def _rotate_half_columns(wk):
    d = wk.shape[0]
    w4 = wk.reshape(d, D_ATTN // HEAD_DIM, 2, HEAD_DIM // 2)
    return jnp.stack([-w4[:, :, 1], w4[:, :, 0]], axis=2).reshape(d, D_ATTN)


def kernel(x_prompt, x_sample, norm_g, w_in, ssm_a_re, ssm_a_im, ssm_log_dt, ssm_b_re, ssm_b_im, ssm_c_re, ssm_c_im, ssm_d, w_glu, b_glu, lambda_q1, lambda_k1, lambda_q2, lambda_k2, subln_g, w_branch, w_out, final_g):
    li = 0
    wi = w_in[li].astype(f32)
    w_xs, w_zs = wi[:, 0:512], wi[:, 512:1024]
    w_q, w_k, w_v, w_za = wi[:, 1024:1536], wi[:, 1536:2048], wi[:, 2048:2560], wi[:, 2560:3072]
    m, bst_f, bst_b, cst_f, cst_b, coef = _ssm_tables(
        ssm_a_re[li], ssm_a_im[li], ssm_log_dt[li], ssm_b_re[li], ssm_b_im[li],
        ssm_c_re[li], ssm_c_im[li], ssm_d[li])
    nb = x_prompt.shape[0]
    lam = (jnp.exp(jnp.sum(lambda_q1[li].astype(f32) * lambda_k1[li].astype(f32)))
           - jnp.exp(jnp.sum(lambda_q2[li].astype(f32) * lambda_k2[li].astype(f32))) + LAM_INIT)
    w = dict(
        norm_g=norm_g[li].astype(f32).reshape(1, D_MODEL),
        wa=jnp.concatenate([w_xs, w_zs, w_k, _rotate_half_columns(w_k), w_za], axis=1).astype(bf16),
        wbt=jnp.concatenate([w_q, w_v], axis=1).T.astype(bf16),
        wg=wi[:, 3072:5120].astype(bf16),
        m=m, bst_f=bst_f, bst_b=bst_b, cst_f=cst_f, cst_b=cst_b,
        coef=jnp.broadcast_to(coef[:, :, :, None, :], (2, N_GROUPS, 3, nb, LANES)),
        lam=lam.reshape(1).astype(f32),
        subln_g=subln_g[li].astype(f32).reshape(1, 2 * HEAD_DIM),
        wglu=w_glu[li].astype(bf16),
        bglu=b_glu[li].astype(f32).reshape(1, D_SSM),
        wb=w_branch[li].astype(bf16),
        wout=w_out[li].astype(bf16),
        final_g=final_g.astype(f32).reshape(1, D_MODEL),
    )
    return (_trunk(x_prompt, w), _trunk(x_sample, w))
```

```python
import functools
import math

import jax
import jax.numpy as jnp
from jax import lax
from jax.experimental import pallas as pl
from jax.experimental.pallas import tpu as pltpu

D_MODEL = 1024
D_SSM = 512
SSM_GROUP = 16
N_GROUPS = 32
STATE = 64
D_ATTN = 512
N_HEADS = 4
HEAD_DIM = 64
ROPE_THETA = 10000.0
NORM_EPS = 1e-6
SUBLN_EPS = 1e-5
LAM_INIT = 0.8 - 0.6 * math.exp(-0.3 * 0)

CHUNK = 16
CHUNK_LANES = CHUNK * SSM_GROUP
LANES = 128
VMEM_LIMIT = 56 * 1024 * 1024

TOKEN_BLOCK = 512
ATTN_BQ = 1024
ATTN_BK = 128
SSM_SEG_CHUNKS = 16
SSM_GROUP_BLOCK = 4
SSM_OUT_ROWS = 128

f32 = jnp.float32
bf16 = jnp.bfloat16


def _params(*sem):
    return pltpu.CompilerParams(dimension_semantics=sem, vmem_limit_bytes=VMEM_LIMIT)


def _dot(a, b):
    return jnp.dot(a, b, preferred_element_type=f32)


def _dot_nt(a, b):
    return lax.dot_general(a, b, (((1,), (1,)), ((), ())), preferred_element_type=f32)


def _segment_transpose8(vs):
    slot = lax.broadcasted_iota(jnp.int32, vs[0].shape, 1) // SSM_GROUP
    for d in (4, 2, 1):
        keep = (slot & d) == 0
        new = list(vs)
        for i in range(8):
            if i & d == 0:
                a, b = vs[i], vs[i + d]
                new[i] = jnp.where(keep, a, pltpu.roll(b, d * SSM_GROUP, 1))
                new[i + d] = jnp.where(keep, pltpu.roll(a, LANES - d * SSM_GROUP, 1), b)
        vs = new
    return vs


def _in_proj_kernel(x_ref, g_ref, wa_ref, wbt_ref, ck_ref, sk_ref, cq_ref, sq_ref,
                    u_ref, zs_ref, k_ref, za_ref, qt_ref, vt_ref, xs_s, *, nb, tb):
    rows = nb * tb
    x = x_ref[...].reshape(rows, D_MODEL)
    r = lax.rsqrt(jnp.mean(x * x, axis=-1, keepdims=True) + NORM_EPS)
    h = (x * r * g_ref[...]).astype(bf16)

    for tile in range(D_SSM // LANES):
        xs_s[tile] = _dot(h, wa_ref[:, tile * LANES:(tile + 1) * LANES])
    zs_ref[...] = _dot(h, wa_ref[:, 512:1024]).astype(bf16).reshape(nb, tb, D_SSM)
    kk = _dot(h, wa_ref[:, 1024:1536]).reshape(nb, tb, D_ATTN)
    kr = _dot(h, wa_ref[:, 1536:2048]).reshape(nb, tb, D_ATTN)
    ck = ck_ref[...]
    sk = sk_ref[...]
    for j in range(D_ATTN // LANES):
        sl = slice(j * LANES, (j + 1) * LANES)
        k_ref[:, :, sl] = (kk[:, :, sl] * ck + kr[:, :, sl] * sk).astype(bf16)
    za_ref[...] = _dot(h, wa_ref[:, 2048:2560]).astype(bf16).reshape(nb, tb, D_ATTN)

    pq = _dot_nt(wbt_ref[0:512, :], h)
    cq = jnp.tile(cq_ref[...], (1, nb))
    sq = jnp.tile(sq_ref[...], (1, nb))
    half = HEAD_DIM // 2
    for hb in range(D_ATTN // HEAD_DIM):
        x1 = pq[hb * HEAD_DIM:hb * HEAD_DIM + half]
        x2 = pq[hb * HEAD_DIM + half:(hb + 1) * HEAD_DIM]
        q1 = (x1 * cq - x2 * sq).astype(bf16)
        q2 = (x2 * cq + x1 * sq).astype(bf16)
        for b in range(nb):
            qt_ref[b, hb * HEAD_DIM:hb * HEAD_DIM + half, :] = q1[:, b * tb:(b + 1) * tb]
            qt_ref[b, hb * HEAD_DIM + half:(hb + 1) * HEAD_DIM, :] = q2[:, b * tb:(b + 1) * tb]
    pv = _dot_nt(wbt_ref[512:1024, :], h).astype(bf16)
    for b in range(nb):
        vt_ref[b, 0] = pv[:, b * tb:(b + 1) * tb]

    nchunk = tb // CHUNK
    for tile in range(D_SSM // LANES):
        for hf in range(CHUNK_LANES // LANES):
            vs = []
            for t8 in range(8):
                t = 8 * hf + t8
                vs.append(jnp.concatenate(
                    [xs_s[tile, pl.ds(t + CHUNK * ch, nb, stride=tb), :] for ch in range(nchunk)],
                    axis=0))
            out = _segment_transpose8(vs)
            for g8 in range(8):
                u_ref[8 * tile + g8, :, hf * LANES:(hf + 1) * LANES] = out[g8].astype(bf16)


def _in_proj(x, norm_g, wa, wbt, ck, sk, cq, sq, *, tb):
    B, L, _ = x.shape
    nchunk = tb // CHUNK
    tok = lambda i: (0, i, 0)
    cst = lambda i: (0, 0)
    out_tok = jax.ShapeDtypeStruct((B, L, 512), bf16)
    return pl.pallas_call(
        functools.partial(_in_proj_kernel, nb=B, tb=tb),
        grid=(L // tb,),
        in_specs=[
            pl.BlockSpec((B, tb, D_MODEL), tok),
            pl.BlockSpec((1, D_MODEL), cst),
            pl.BlockSpec(wa.shape, cst),
            pl.BlockSpec(wbt.shape, cst),
            pl.BlockSpec((tb, LANES), lambda i: (i, 0)),
            pl.BlockSpec((tb, LANES), lambda i: (i, 0)),
            pl.BlockSpec((HEAD_DIM // 2, tb), lambda i: (0, i)),
            pl.BlockSpec((HEAD_DIM // 2, tb), lambda i: (0, i)),
        ],
        out_specs=[
            pl.BlockSpec((N_GROUPS, nchunk * B, CHUNK_LANES), tok),
            pl.BlockSpec((B, tb, 512), tok),
            pl.BlockSpec((B, tb, 512), tok),
            pl.BlockSpec((B, tb, 512), tok),
            pl.BlockSpec((B, 512, tb), lambda i: (0, 0, i)),
            pl.BlockSpec((B, 1, 512, tb), lambda i: (0, i, 0, 0)),
        ],
        out_shape=[jax.ShapeDtypeStruct((N_GROUPS, (L // CHUNK) * B, CHUNK_LANES), bf16),
                   out_tok, out_tok, out_tok,
                   jax.ShapeDtypeStruct((B, 512, L), bf16),
                   jax.ShapeDtypeStruct((B, L // tb, 512, tb), bf16)],
        scratch_shapes=[pltpu.VMEM((D_SSM // LANES, B * tb, LANES), f32)],
        compiler_params=_params("parallel"),
        name="in_proj",
    )(x, norm_g, wa, wbt, ck, sk, cq, sq)


def _ssm_state_kernel(uf_ref, ub_ref, bf_ref, bb_ref, coef_ref, hf_ref, gb_ref, s_ref, st_ref,
                      *, nc, nb, gblk):
    @pl.when(pl.program_id(0) == 0)
    def _():
        st_ref[...] = jnp.zeros_like(st_ref)

    for g0 in range(0, N_GROUPS, gblk):
        for gi in range(gblk):
            s_ref[0, gi] = _dot(uf_ref[g0 + gi], bf_ref[g0 + gi])
            s_ref[1, gi] = _dot(ub_ref[g0 + gi], bb_ref[g0 + gi])
        gs = slice(g0, g0 + gblk)
        af1, af2, af3 = coef_ref[0, gs, 0], coef_ref[0, gs, 1], coef_ref[0, gs, 2]
        ab1, ab2, ab3 = coef_ref[1, gs, 0], coef_ref[1, gs, 1], coef_ref[1, gs, 2]

        def body(i, carry):
            hf, wf, hb, wb = carry
            rf = pl.multiple_of(i * nb, nb)
            rb = pl.multiple_of((nc - 1 - i) * nb, nb)
            hf_ref[gs, pl.ds(rf, nb), :] = hf
            gb_ref[gs, pl.ds(rb, nb), :] = hb
            sf = s_ref[0, :, pl.ds(rf, nb), :]
            sb = s_ref[1, :, pl.ds(rb, nb), :]
            hf2 = af1 * hf + af2 * wf + sf[..., :LANES]
            wf2 = af1 * wf + af3 * hf + sf[..., LANES:]
            hb2 = ab1 * hb + ab2 * wb + sb[..., :LANES]
            wb2 = ab1 * wb + ab3 * hb + sb[..., LANES:]
            return hf2, wf2, hb2, wb2

        init = (st_ref[0, 0, gs], st_ref[0, 1, gs], st_ref[1, 0, gs], st_ref[1, 1, gs])
        hf, wf, hb, wb = lax.fori_loop(0, nc, body, init)
        st_ref[0, 0, gs] = hf
        st_ref[0, 1, gs] = wf
        st_ref[1, 0, gs] = hb
        st_ref[1, 1, gs] = wb


def _ssm_state(u, bst_f, bst_b, coef, *, nb):
    G, rows, _ = u.shape
    nc = SSM_SEG_CHUNKS
    seg_rows = nc * nb
    nseg = rows // seg_rows
    gblk = SSM_GROUP_BLOCK
    fwd = lambda i: (0, i, 0)
    bwd = lambda i: (0, nseg - 1 - i, 0)
    cst3 = lambda i: (0, 0, 0)
    return pl.pallas_call(
        functools.partial(_ssm_state_kernel, nc=nc, nb=nb, gblk=gblk),
        grid=(nseg,),
        in_specs=[
            pl.BlockSpec((G, seg_rows, CHUNK_LANES), fwd),
            pl.BlockSpec((G, seg_rows, CHUNK_LANES), bwd),
            pl.BlockSpec(bst_f.shape, cst3),
            pl.BlockSpec(bst_b.shape, cst3),
            pl.BlockSpec(coef.shape, lambda i: (0, 0, 0, 0, 0)),
        ],
        out_specs=[
            pl.BlockSpec((G, seg_rows, LANES), fwd),
            pl.BlockSpec((G, seg_rows, LANES), bwd),
        ],
        out_shape=[jax.ShapeDtypeStruct((G, rows, LANES), f32),
                   jax.ShapeDtypeStruct((G, rows, LANES), f32)],
        scratch_shapes=[
            pltpu.VMEM((2, gblk, seg_rows, CHUNK_LANES), f32),
            pltpu.VMEM((2, 2, G, nb, LANES), f32),
        ],
        compiler_params=_params("arbitrary"),
        name="ssm_state",
    )(u, u, bst_f, bst_b, coef)


def _ssm_out_kernel(u_ref, hf_ref, gb_ref, m_ref, cf_ref, cb_ref, y_ref, nat_s, *, nb, nchunk):
    tok = nchunk * CHUNK
    for tile in range(D_SSM // LANES):
        ys = []
        for g8 in range(8):
            g = 8 * tile + g8
            y = _dot(u_ref[g], m_ref[g])
            y = y + _dot(hf_ref[g].astype(bf16), cf_ref[g])
            y = y + _dot(gb_ref[g].astype(bf16), cb_ref[g])
            ys.append(y)
        for hf in range(CHUNK_LANES // LANES):
            out = _segment_transpose8([y[:, hf * LANES:(hf + 1) * LANES] for y in ys])
            for t8 in range(8):
                t = 8 * hf + t8
                for ch in range(nchunk):
                    nat_s[tile, pl.ds(t + CHUNK * ch, nb, stride=tok), :] = out[t8][ch * nb:(ch + 1) * nb]
        y_ref[:, :, tile * LANES:(tile + 1) * LANES] = nat_s[tile].reshape(nb, tok, LANES).astype(bf16)


def _ssm_out(u, hf, gb, m, cst_f, cst_b, *, nb):
    G, rows, _ = u.shape
    rb = min(SSM_OUT_ROWS, rows)
    nchunk = rb // nb
    tok = nchunk * CHUNK
    blk = lambda i: (0, i, 0)
    cst3 = lambda i: (0, 0, 0)
    return pl.pallas_call(
        functools.partial(_ssm_out_kernel, nb=nb, nchunk=nchunk),
        grid=(rows // rb,),
        in_specs=[
            pl.BlockSpec((G, rb, CHUNK_LANES), blk),
            pl.BlockSpec((G, rb, LANES), blk),
            pl.BlockSpec((G, rb, LANES), blk),
            pl.BlockSpec(m.shape, cst3),
            pl.BlockSpec(cst_f.shape, cst3),
            pl.BlockSpec(cst_b.shape, cst3),
        ],
        out_specs=pl.BlockSpec((nb, tok, D_SSM), blk),
        out_shape=jax.ShapeDtypeStruct((nb, (rows // nb) * CHUNK, D_SSM), bf16),
        scratch_shapes=[pltpu.VMEM((D_SSM // LANES, nb * tok, LANES), f32)],
        compiler_params=_params("parallel"),
        name="ssm_out",
    )(u, hf, gb, m, cst_f, cst_b)


def _ssm_tables(a_re, a_im, log_dt, b_re, b_im, c_re, c_im, d_skip):
    T, G, P, C = CHUNK, N_GROUPS, STATE, SSM_GROUP
    a_re, a_im, log_dt = a_re.astype(f32), a_im.astype(f32), log_dt.astype(f32)
    dt = jnp.exp(log_dt)[..., None]
    ks = jnp.arange(T + 1, dtype=f32)[:, None, None, None]
    mag = jnp.exp(ks * (a_re * dt))
    pw_re = mag * jnp.cos(ks * (a_im * dt))
    pw_im = mag * jnp.sin(ks * (a_im * dt))
    n_re, n_im = pw_re[1] - 1.0, pw_im[1]
    den = a_re * a_re + a_im * a_im
    co_re = (n_re * a_re + n_im * a_im) / den
    co_im = (n_im * a_re - n_re * a_im) / den
    b_re, b_im = b_re.astype(f32), b_im.astype(f32)
    bb_re = co_re[..., None] * b_re - co_im[..., None] * b_im
    bb_im = co_re[..., None] * b_im + co_im[..., None] * b_re
    c_re, c_im = c_re.astype(f32), c_im.astype(f32)
    cp_re = c_re[None] * pw_re[:, :, :, None, :] - c_im[None] * pw_im[:, :, :, None, :]
    cp_im = c_re[None] * pw_im[:, :, :, None, :] + c_im[None] * pw_re[:, :, :, None, :]
    kmat = (jnp.einsum('kngcp,ngpd->kngcd', cp_re[:T], bb_re)
            - jnp.einsum('kngcp,ngpd->kngcd', cp_im[:T], bb_im))
    j = jnp.arange(T)[:, None]
    t = jnp.arange(T)[None, :]
    lag = t - j
    kf = jnp.where((lag >= 0)[:, :, None, None, None], kmat[jnp.clip(lag, 0, T - 1), 0], 0.0)
    kb = jnp.where((lag <= 0)[:, :, None, None, None], kmat[jnp.clip(-lag, 0, T - 1), 1], 0.0)
    m5 = (kf + kb).transpose(2, 0, 4, 1, 3)
    eye = (jnp.eye(T, dtype=f32)[:, None, :, None] * jnp.eye(C, dtype=f32)[None, :, None, :])
    m5 = m5 + eye[None] * d_skip.astype(f32).reshape(G, 1, 1, 1, C)
    m = m5.reshape(G, T * C, T * C)

    def bst(n, powers):
        pr, pi = pw_re[powers, n], pw_im[powers, n]
        re = pr[..., None] * bb_re[n][None] - pi[..., None] * bb_im[n][None]
        im = pr[..., None] * bb_im[n][None] + pi[..., None] * bb_re[n][None]
        re = re.transpose(1, 0, 3, 2).reshape(G, T * C, P)
        im = im.transpose(1, 0, 3, 2).reshape(G, T * C, P)
        return jnp.concatenate([re, im, im, re], axis=-1)

    bst_f = bst(0, T - 1 - jnp.arange(T))
    bst_b = bst(1, jnp.arange(T))

    def cst(n, powers):
        re = cp_re[powers, n]
        im = cp_im[powers, n]
        re = re.transpose(1, 3, 0, 2).reshape(G, P, T * C)
        im = im.transpose(1, 3, 0, 2).reshape(G, P, T * C)
        return jnp.concatenate([re, -im], axis=1)

    cst_f = cst(0, jnp.arange(T) + 1)
    cst_b = cst(1, T - jnp.arange(T))
    ar, ai = pw_re[T], pw_im[T]
    coef = jnp.stack([jnp.concatenate([ar, ar], -1),
                      jnp.concatenate([-ai, ai], -1),
                      jnp.concatenate([ai, -ai], -1)], axis=2)
    return m.astype(bf16), bst_f.astype(bf16), bst_b.astype(bf16), cst_f.astype(bf16), cst_b.astype(bf16), coef


def _attn_kernel(lam_ref, qt_ref, k_ref, vt_ref, g_ref, za_ref, o_ref,
                 qbd_ref, acc_ref, p_ref, m_ref, l_ref, a_ref, *, nkv, bq, bk):
    lam = lam_ref[0]
    half = HEAD_DIM
    ns = bq // LANES
    zero = jnp.zeros((half, LANES), bf16)
    for st in range(ns):
        qs = qt_ref[0, :, st * LANES:(st + 1) * LANES]
        qbd_ref[st, 0:half, 0:LANES] = qs[0:half]
        qbd_ref[st, 0:half, LANES:2 * LANES] = zero
        qbd_ref[st, half:2 * half, 0:LANES] = zero
        qbd_ref[st, half:2 * half, LANES:2 * LANES] = qs[half:2 * half]
    acc_ref[...] = jnp.zeros_like(acc_ref)
    p_ref[...] = jnp.zeros_like(p_ref)
    a_ref[...] = jnp.ones_like(a_ref)
    l_ref[...] = jnp.zeros_like(l_ref)
    m_ref[...] = jnp.full(m_ref.shape, -jnp.inf, f32)

    def fold(st, vb):
        acc_ref[st] = a_ref[st] * acc_ref[st] + _dot(vb, p_ref[st])

    def body(j, carry):
        off = pl.multiple_of(j * bk, bk)
        kb = k_ref[0, pl.ds(off, bk), :]
        vb = vt_ref[0, jnp.maximum(j - 1, 0)]
        for st in range(ns):
            fold(st, vb)
            s = _dot(kb, qbd_ref[st])
            m_old = m_ref[st]
            m_new = jnp.maximum(m_old, jnp.max(s, axis=0, keepdims=True))
            p = jnp.exp2(s - m_new)
            alpha = jnp.exp2(m_old - m_new)
            l_ref[st] = alpha * l_ref[st] + jnp.sum(p, axis=0, keepdims=True)
            m_ref[st] = m_new
            a_ref[st] = alpha
            p_ref[st] = p.astype(bf16)
        return carry

    lax.fori_loop(0, nkv, body, 0, unroll=2)

    vb_last = vt_ref[0, nkv - 1]
    for st in range(ns):
        fold(st, vb_last)
        acc = acc_ref[st]
        inv = 1.0 / l_ref[st]
        ot = acc[:, :LANES] * inv[:, :LANES] - lam * (acc[:, LANES:] * inv[:, LANES:])
        o = ot.T
        y = o * lax.rsqrt(jnp.mean(o * o, axis=-1, keepdims=True) + SUBLN_EPS) * g_ref[...] * (1.0 - LAM_INIT)
        za = za_ref[0, st * LANES:(st + 1) * LANES, :].astype(f32)
        o_ref[0, st * LANES:(st + 1) * LANES, :] = (y * (za * jax.nn.sigmoid(za))).astype(bf16)


def _attention(lam, qt, k, vt, subln_g, za, *, bq, bk):
    B, L, _ = k.shape
    nkv = L // bk
    return pl.pallas_call(
        functools.partial(_attn_kernel, nkv=nkv, bq=bq, bk=bk),
        grid=(B, N_HEADS, L // bq),
        in_specs=[
            pl.BlockSpec(memory_space=pltpu.SMEM),
            pl.BlockSpec((1, 2 * HEAD_DIM, bq), lambda b, h, i: (b, h, i)),
            pl.BlockSpec((1, L, 2 * HEAD_DIM), lambda b, h, i: (b, 0, h)),
            pl.BlockSpec((1, nkv, 2 * HEAD_DIM, bk), lambda b, h, i: (b, 0, h, 0)),
            pl.BlockSpec((1, 2 * HEAD_DIM), lambda b, h, i: (0, 0)),
            pl.BlockSpec((1, bq, 2 * HEAD_DIM), lambda b, h, i: (b, i, h)),
        ],
        out_specs=pl.BlockSpec((1, bq, 2 * HEAD_DIM), lambda b, h, i: (b, i, h)),
        out_shape=jax.ShapeDtypeStruct((B, L, D_ATTN), bf16),
        scratch_shapes=[
            pltpu.VMEM((bq // LANES, 2 * HEAD_DIM, 2 * LANES), bf16),
            pltpu.VMEM((bq // LANES, 2 * HEAD_DIM, 2 * LANES), f32),
            pltpu.VMEM((bq // LANES, bk, 2 * LANES), bf16),
            pltpu.VMEM((bq // LANES, 1, 2 * LANES), f32),
            pltpu.VMEM((bq // LANES, 1, 2 * LANES), f32),
            pltpu.VMEM((bq // LANES, 1, 2 * LANES), f32),
        ],
        compiler_params=_params("parallel", "parallel", "arbitrary"),
        name="attention",
    )(lam, qt, k, vt, subln_g, za)


def _out_proj_kernel(x_ref, yssm_ref, zs_ref, ya_ref, g_ref, wg_ref, wglu_ref, bglu_ref,
                     wb_ref, wout_ref, fg_ref, o_ref):
    x = x_ref[0]
    r = lax.rsqrt(jnp.mean(x * x, axis=-1, keepdims=True) + NORM_EPS)
    h = (x * r * g_ref[...]).astype(bf16)

    ys = jax.nn.gelu(yssm_ref[0].astype(f32))
    ys = ys * jax.nn.sigmoid(_dot(ys.astype(bf16), wglu_ref[...]) + bglu_ref[...])
    zs = zs_ref[0].astype(f32)
    ys = ys * (zs * jax.nn.sigmoid(zs))

    ps = _dot(ys.astype(bf16), wb_ref[0])
    merged = jax.nn.sigmoid(_dot(h, wg_ref[:, 0:D_MODEL])) * ps
    pa = _dot(ya_ref[0], wb_ref[1])
    merged = merged + jax.nn.sigmoid(_dot(h, wg_ref[:, D_MODEL:2 * D_MODEL])) * pa
    out = x + _dot(merged.astype(bf16), wout_ref[...])
    o_ref[0] = out * lax.rsqrt(jnp.mean(out * out, axis=-1, keepdims=True) + NORM_EPS) * fg_ref[...]


def _out_proj(x, yssm, zs, ya, norm_g, wg, wglu, bglu, wb, wout, final_g, *, tm):
    B, L, _ = x.shape
    tok = lambda b, i: (b, i, 0)
    cst = lambda b, i: (0, 0)
    return pl.pallas_call(
        _out_proj_kernel,
        grid=(B, L // tm),
        in_specs=[
            pl.BlockSpec((1, tm, D_MODEL), tok),
            pl.BlockSpec((1, tm, 512), tok),
            pl.BlockSpec((1, tm, 512), tok),
            pl.BlockSpec((1, tm, 512), tok),
            pl.BlockSpec((1, D_MODEL), cst),
            pl.BlockSpec(wg.shape, cst),
            pl.BlockSpec(wglu.shape, cst),
            pl.BlockSpec((1, D_SSM), cst),
            pl.BlockSpec(wb.shape, lambda b, i: (0, 0, 0)),
            pl.BlockSpec(wout.shape, cst),
            pl.BlockSpec((1, D_MODEL), cst),
        ],
        out_specs=pl.BlockSpec((1, tm, D_MODEL), tok),
        out_shape=jax.ShapeDtypeStruct((B, L, D_MODEL), x.dtype),
        compiler_params=_params("parallel", "parallel"),
        name="out_proj",
    )(x, yssm, zs, ya, norm_g, wg, wglu, bglu, wb, wout, final_g)


def _rotary_tables(L):
    half = HEAD_DIM // 2
    inv_freq = 1.0 / (ROPE_THETA ** (jnp.arange(0, half, dtype=f32) * 2.0 / HEAD_DIM))
    ang = jnp.arange(L, dtype=f32)[:, None] * inv_freq[None, :]
    cos, sin = jnp.cos(ang), jnp.sin(ang)
    ck = jnp.tile(cos, (1, LANES // half))
    sk = jnp.tile(sin, (1, LANES // half))
    scale = math.log2(math.e) / math.sqrt(HEAD_DIM)
    return ck, sk, cos.T * scale, sin.T * scale


def _trunk(x, w):
    B, L, _ = x.shape
    assert B == 8 and L % ATTN_BQ == 0, "scan state vregs hold one row per batch element"
    ck, sk, cq, sq = _rotary_tables(L)
    u, zs, k, za, qt, vt = _in_proj(x, w["norm_g"], w["wa"], w["wbt"], ck, sk, cq, sq, tb=ATTN_BK)
    hf, gb = _ssm_state(u, w["bst_f"], w["bst_b"], w["coef"], nb=B)
    yssm = _ssm_out(u, hf, gb, w["m"], w["cst_f"], w["cst_b"], nb=B)
    ya = _attention(w["lam"], qt, k, vt, w["subln_g"], za, bq=ATTN_BQ, bk=ATTN_BK)
    return _out_proj(x, yssm, zs, ya, w["norm_g"], w["wg"], w["wglu"], w["bglu"], w["wb"], w["wout"],
                     w["final_g"], tm=TOKEN_BLOCK)


def _rotate_half_columns(wk):
    d = wk.shape[0]
    w4 = wk.reshape(d, D_ATTN // HEAD_DIM, 2, HEAD_DIM // 2)
    return jnp.stack([-w4[:, :, 1], w4[:, :, 0]], axis=2).reshape(d, D_ATTN)


def kernel(x_prompt, x_sample, norm_g, w_in, ssm_a_re, ssm_a_im, ssm_log_dt, ssm_b_re, ssm_b_im, ssm_c_re, ssm_c_im, ssm_d, w_glu, b_glu, lambda_q1, lambda_k1, lambda_q2, lambda_k2, subln_g, w_branch, w_out, final_g):
    li = 0
    wi = w_in[li].astype(f32)
    w_xs, w_zs = wi[:, 0:512], wi[:, 512:1024]
    w_q, w_k, w_v, w_za = wi[:, 1024:1536], wi[:, 1536:2048], wi[:, 2048:2560], wi[:, 2560:3072]
    m, bst_f, bst_b, cst_f, cst_b, coef = _ssm_tables(
        ssm_a_re[li], ssm_a_im[li], ssm_log_dt[li], ssm_b_re[li], ssm_b_im[li],
        ssm_c_re[li], ssm_c_im[li], ssm_d[li])
    nb = x_prompt.shape[0]
    lam = (jnp.exp(jnp.sum(lambda_q1[li].astype(f32) * lambda_k1[li].astype(f32)))
           - jnp.exp(jnp.sum(lambda_q2[li].astype(f32) * lambda_k2[li].astype(f32))) + LAM_INIT)
    w = dict(
        norm_g=norm_g[li].astype(f32).reshape(1, D_MODEL),
        wa=jnp.concatenate([w_xs, w_zs, w_k, _rotate_half_columns(w_k), w_za], axis=1).astype(bf16),
        wbt=jnp.concatenate([w_q, w_v], axis=1).T.astype(bf16),
        wg=wi[:, 3072:5120].astype(bf16),
        m=m, bst_f=bst_f, bst_b=bst_b, cst_f=cst_f, cst_b=cst_b,
        coef=jnp.broadcast_to(coef[:, :, :, None, :], (2, N_GROUPS, 3, nb, LANES)),
        lam=lam.reshape(1).astype(f32),
        subln_g=subln_g[li].astype(f32).reshape(1, 2 * HEAD_DIM),
        wglu=w_glu[li].astype(bf16),
        bglu=b_glu[li].astype(f32).reshape(1, D_SSM),
        wb=w_branch[li].astype(bf16),
        wout=w_out[li].astype(bf16),
        final_g=final_g.astype(f32).reshape(1, D_MODEL),
    )
    return (_trunk(x_prompt, w), _trunk(x_sample, w))
```

```python
import functools
import math

import jax
import jax.numpy as jnp
from jax import lax
from jax.experimental import pallas as pl
from jax.experimental.pallas import tpu as pltpu

D_MODEL = 1024
D_SSM = 512
SSM_GROUP = 16
N_GROUPS = 32
STATE = 64
D_ATTN = 512
N_HEADS = 4
HEAD_DIM = 64
ROPE_THETA = 10000.0
NORM_EPS = 1e-6
SUBLN_EPS = 1e-5
LAM_INIT = 0.8 - 0.6 * math.exp(-0.3 * 0)

CHUNK = 16
CHUNK_LANES = CHUNK * SSM_GROUP
LANES = 128
VMEM_LIMIT = 56 * 1024 * 1024

TOKEN_BLOCK = 512
ATTN_BQ = 1024
ATTN_BK = 256
PROJ_TOKENS = ATTN_BK // 2
F32_ROWS = 8
BF16_ROWS = 16
V_ROWS = 2 * HEAD_DIM + BF16_ROWS
SSM_SEG_CHUNKS = 16
SSM_GROUP_BLOCK = 4
SSM_OUT_ROWS = 128

f32 = jnp.float32
bf16 = jnp.bfloat16


def _params(*sem):
    return pltpu.CompilerParams(dimension_semantics=sem, vmem_limit_bytes=VMEM_LIMIT)


def _dot(a, b):
    return jnp.dot(a, b, preferred_element_type=f32)


def _dot_nt(a, b):
    return lax.dot_general(a, b, (((1,), (1,)), ((), ())), preferred_element_type=f32)


def _segment_transpose8(vs):
    slot = lax.broadcasted_iota(jnp.int32, vs[0].shape, 1) // SSM_GROUP
    for d in (4, 2, 1):
        keep = (slot & d) == 0
        new = list(vs)
        for i in range(8):
            if i & d == 0:
                a, b = vs[i], vs[i + d]
                new[i] = jnp.where(keep, a, pltpu.roll(b, d * SSM_GROUP, 1))
                new[i + d] = jnp.where(keep, pltpu.roll(a, LANES - d * SSM_GROUP, 1), b)
        vs = new
    return vs


def _in_proj_kernel(x_ref, g_ref, wa_ref, wbt_ref, ck_ref, sk_ref, cq_ref, sq_ref,
                    u_ref, zs_ref, k_ref, za_ref, qt_ref, vt_ref, xs_s, *, nb, tb):
    rows = nb * tb
    x = x_ref[...].reshape(rows, D_MODEL)
    r = lax.rsqrt(jnp.mean(x * x, axis=-1, keepdims=True) + NORM_EPS)
    h = (x * r * g_ref[...]).astype(bf16)

    for tile in range(D_SSM // LANES):
        xs_s[tile] = _dot(h, wa_ref[:, tile * LANES:(tile + 1) * LANES])
    zs_ref[...] = _dot(h, wa_ref[:, 512:1024]).astype(bf16).reshape(nb, tb, D_SSM)
    kk = _dot(h, wa_ref[:, 1024:1536]).reshape(nb, tb, D_ATTN)
    kr = _dot(h, wa_ref[:, 1536:2048]).reshape(nb, tb, D_ATTN)
    ck = ck_ref[...]
    sk = sk_ref[...]
    for j in range(D_ATTN // LANES):
        sl = slice(j * LANES, (j + 1) * LANES)
        k_ref[:, :, sl] = (kk[:, :, sl] * ck + kr[:, :, sl] * sk).astype(bf16)
    za_ref[...] = _dot(h, wa_ref[:, 2048:2560]).astype(bf16).reshape(nb, tb, D_ATTN)

    pq = _dot_nt(wbt_ref[0:512, :], h)
    cq = jnp.tile(cq_ref[...], (1, nb))
    sq = jnp.tile(sq_ref[...], (1, nb))
    half = HEAD_DIM // 2
    for hb in range(D_ATTN // HEAD_DIM):
        x1 = pq[hb * HEAD_DIM:hb * HEAD_DIM + half]
        x2 = pq[hb * HEAD_DIM + half:(hb + 1) * HEAD_DIM]
        q1 = (x1 * cq - x2 * sq).astype(bf16)
        q2 = (x2 * cq + x1 * sq).astype(bf16)
        for b in range(nb):
            qt_ref[b, hb * HEAD_DIM:hb * HEAD_DIM + half, :] = q1[:, b * tb:(b + 1) * tb]
            qt_ref[b, hb * HEAD_DIM + half:(hb + 1) * HEAD_DIM, :] = q2[:, b * tb:(b + 1) * tb]
    pv = _dot_nt(wbt_ref[512:1024, :], h).astype(bf16)
    dv = 2 * HEAD_DIM
    ones = jnp.ones((V_ROWS - dv, tb), bf16)
    for b in range(nb):
        for hd in range(N_HEADS):
            vt_ref[b, 0, hd * V_ROWS:hd * V_ROWS + dv, :] = pv[hd * dv:(hd + 1) * dv, b * tb:(b + 1) * tb]
            vt_ref[b, 0, hd * V_ROWS + dv:(hd + 1) * V_ROWS, :] = ones

    nchunk = tb // CHUNK
    for tile in range(D_SSM // LANES):
        for hf in range(CHUNK_LANES // LANES):
            vs = []
            for t8 in range(8):
                t = 8 * hf + t8
                vs.append(jnp.concatenate(
                    [xs_s[tile, pl.ds(t + CHUNK * ch, nb, stride=tb), :] for ch in range(nchunk)],
                    axis=0))
            out = _segment_transpose8(vs)
            for g8 in range(8):
                u_ref[8 * tile + g8, :, hf * LANES:(hf + 1) * LANES] = out[g8].astype(bf16)


def _in_proj(x, norm_g, wa, wbt, ck, sk, cq, sq, *, tb):
    B, L, _ = x.shape
    nchunk = tb // CHUNK
    tok = lambda i: (0, i, 0)
    cst = lambda i: (0, 0)
    out_tok = jax.ShapeDtypeStruct((B, L, 512), bf16)
    return pl.pallas_call(
        functools.partial(_in_proj_kernel, nb=B, tb=tb),
        grid=(L // tb,),
        in_specs=[
            pl.BlockSpec((B, tb, D_MODEL), tok),
            pl.BlockSpec((1, D_MODEL), cst),
            pl.BlockSpec(wa.shape, cst),
            pl.BlockSpec(wbt.shape, cst),
            pl.BlockSpec((tb, LANES), lambda i: (i, 0)),
            pl.BlockSpec((tb, LANES), lambda i: (i, 0)),
            pl.BlockSpec((HEAD_DIM // 2, tb), lambda i: (0, i)),
            pl.BlockSpec((HEAD_DIM // 2, tb), lambda i: (0, i)),
        ],
        out_specs=[
            pl.BlockSpec((N_GROUPS, nchunk * B, CHUNK_LANES), tok),
            pl.BlockSpec((B, tb, 512), tok),
            pl.BlockSpec((B, tb, 512), tok),
            pl.BlockSpec((B, tb, 512), tok),
            pl.BlockSpec((B, 512, tb), lambda i: (0, 0, i)),
            pl.BlockSpec((B, 1, N_HEADS * V_ROWS, tb), lambda i: (0, i, 0, 0)),
        ],
        out_shape=[jax.ShapeDtypeStruct((N_GROUPS, (L // CHUNK) * B, CHUNK_LANES), bf16),
                   out_tok, out_tok, out_tok,
                   jax.ShapeDtypeStruct((B, 512, L), bf16),
                   jax.ShapeDtypeStruct((B, L // tb, N_HEADS * V_ROWS, tb), bf16)],
        scratch_shapes=[pltpu.VMEM((D_SSM // LANES, B * tb, LANES), f32)],
        compiler_params=_params("parallel"),
        name="in_proj",
    )(x, norm_g, wa, wbt, ck, sk, cq, sq)


def _ssm_state_kernel(uf_ref, ub_ref, bf_ref, bb_ref, coef_ref, hf_ref, gb_ref, s_ref, st_ref,
                      *, nc, nb, gblk):
    @pl.when(pl.program_id(0) == 0)
    def _():
        st_ref[...] = jnp.zeros_like(st_ref)

    for g0 in range(0, N_GROUPS, gblk):
        for gi in range(gblk):
            s_ref[0, gi] = _dot(uf_ref[g0 + gi], bf_ref[g0 + gi])
            s_ref[1, gi] = _dot(ub_ref[g0 + gi], bb_ref[g0 + gi])
        gs = slice(g0, g0 + gblk)
        af1, af2, af3 = coef_ref[0, gs, 0], coef_ref[0, gs, 1], coef_ref[0, gs, 2]
        ab1, ab2, ab3 = coef_ref[1, gs, 0], coef_ref[1, gs, 1], coef_ref[1, gs, 2]

        def body(i, carry):
            hf, wf, hb, wb = carry
            rf = pl.multiple_of(i * nb, nb)
            rb = pl.multiple_of((nc - 1 - i) * nb, nb)
            hf_ref[gs, pl.ds(rf, nb), :] = hf
            gb_ref[gs, pl.ds(rb, nb), :] = hb
            sf = s_ref[0, :, pl.ds(rf, nb), :]
            sb = s_ref[1, :, pl.ds(rb, nb), :]
            hf2 = af1 * hf + af2 * wf + sf[..., :LANES]
            wf2 = af1 * wf + af3 * hf + sf[..., LANES:]
            hb2 = ab1 * hb + ab2 * wb + sb[..., :LANES]
            wb2 = ab1 * wb + ab3 * hb + sb[..., LANES:]
            return hf2, wf2, hb2, wb2

        init = (st_ref[0, 0, gs], st_ref[0, 1, gs], st_ref[1, 0, gs], st_ref[1, 1, gs])
        hf, wf, hb, wb = lax.fori_loop(0, nc, body, init)
        st_ref[0, 0, gs] = hf
        st_ref[0, 1, gs] = wf
        st_ref[1, 0, gs] = hb
        st_ref[1, 1, gs] = wb


def _ssm_state(u, bst_f, bst_b, coef, *, nb):
    G, rows, _ = u.shape
    nc = SSM_SEG_CHUNKS
    seg_rows = nc * nb
    nseg = rows // seg_rows
    gblk = SSM_GROUP_BLOCK
    fwd = lambda i: (0, i, 0)
    bwd = lambda i: (0, nseg - 1 - i, 0)
    cst3 = lambda i: (0, 0, 0)
    return pl.pallas_call(
        functools.partial(_ssm_state_kernel, nc=nc, nb=nb, gblk=gblk),
        grid=(nseg,),
        in_specs=[
            pl.BlockSpec((G, seg_rows, CHUNK_LANES), fwd),
            pl.BlockSpec((G, seg_rows, CHUNK_LANES), bwd),
            pl.BlockSpec(bst_f.shape, cst3),
            pl.BlockSpec(bst_b.shape, cst3),
            pl.BlockSpec(coef.shape, lambda i: (0, 0, 0, 0, 0)),
        ],
        out_specs=[
            pl.BlockSpec((G, seg_rows, LANES), fwd),
            pl.BlockSpec((G, seg_rows, LANES), bwd),
        ],
        out_shape=[jax.ShapeDtypeStruct((G, rows, LANES), f32),
                   jax.ShapeDtypeStruct((G, rows, LANES), f32)],
        scratch_shapes=[
            pltpu.VMEM((2, gblk, seg_rows, CHUNK_LANES), f32),
            pltpu.VMEM((2, 2, G, nb, LANES), f32),
        ],
        compiler_params=_params("arbitrary"),
        name="ssm_state",
    )(u, u, bst_f, bst_b, coef)


def _ssm_out_kernel(u_ref, hf_ref, gb_ref, m_ref, cf_ref, cb_ref, y_ref, nat_s, *, nb, nchunk):
    tok = nchunk * CHUNK
    for tile in range(D_SSM // LANES):
        ys = []
        for g8 in range(8):
            g = 8 * tile + g8
            y = _dot(u_ref[g], m_ref[g])
            y = y + _dot(hf_ref[g].astype(bf16), cf_ref[g])
            y = y + _dot(gb_ref[g].astype(bf16), cb_ref[g])
            ys.append(y)
        for hf in range(CHUNK_LANES // LANES):
            out = _segment_transpose8([y[:, hf * LANES:(hf + 1) * LANES] for y in ys])
            for t8 in range(8):
                t = 8 * hf + t8
                for ch in range(nchunk):
                    nat_s[tile, pl.ds(t + CHUNK * ch, nb, stride=tok), :] = out[t8][ch * nb:(ch + 1) * nb]
        y_ref[:, :, tile * LANES:(tile + 1) * LANES] = nat_s[tile].reshape(nb, tok, LANES).astype(bf16)


def _ssm_out(u, hf, gb, m, cst_f, cst_b, *, nb):
    G, rows, _ = u.shape
    rb = min(SSM_OUT_ROWS, rows)
    nchunk = rb // nb
    tok = nchunk * CHUNK
    blk = lambda i: (0, i, 0)
    cst3 = lambda i: (0, 0, 0)
    return pl.pallas_call(
        functools.partial(_ssm_out_kernel, nb=nb, nchunk=nchunk),
        grid=(rows // rb,),
        in_specs=[
            pl.BlockSpec((G, rb, CHUNK_LANES), blk),
            pl.BlockSpec((G, rb, LANES), blk),
            pl.BlockSpec((G, rb, LANES), blk),
            pl.BlockSpec(m.shape, cst3),
            pl.BlockSpec(cst_f.shape, cst3),
            pl.BlockSpec(cst_b.shape, cst3),
        ],
        out_specs=pl.BlockSpec((nb, tok, D_SSM), blk),
        out_shape=jax.ShapeDtypeStruct((nb, (rows // nb) * CHUNK, D_SSM), bf16),
        scratch_shapes=[pltpu.VMEM((D_SSM // LANES, nb * tok, LANES), f32)],
        compiler_params=_params("parallel"),
        name="ssm_out",
    )(u, hf, gb, m, cst_f, cst_b)


def _ssm_tables(a_re, a_im, log_dt, b_re, b_im, c_re, c_im, d_skip):
    T, G, P, C = CHUNK, N_GROUPS, STATE, SSM_GROUP
    a_re, a_im, log_dt = a_re.astype(f32), a_im.astype(f32), log_dt.astype(f32)
    dt = jnp.exp(log_dt)[..., None]
    ks = jnp.arange(T + 1, dtype=f32)[:, None, None, None]
    mag = jnp.exp(ks * (a_re * dt))
    pw_re = mag * jnp.cos(ks * (a_im * dt))
    pw_im = mag * jnp.sin(ks * (a_im * dt))
    n_re, n_im = pw_re[1] - 1.0, pw_im[1]
    den = a_re * a_re + a_im * a_im
    co_re = (n_re * a_re + n_im * a_im) / den
    co_im = (n_im * a_re - n_re * a_im) / den
    b_re, b_im = b_re.astype(f32), b_im.astype(f32)
    bb_re = co_re[..., None] * b_re - co_im[..., None] * b_im
    bb_im = co_re[..., None] * b_im + co_im[..., None] * b_re
    c_re, c_im = c_re.astype(f32), c_im.astype(f32)
    cp_re = c_re[None] * pw_re[:, :, :, None, :] - c_im[None] * pw_im[:, :, :, None, :]
    cp_im = c_re[None] * pw_im[:, :, :, None, :] + c_im[None] * pw_re[:, :, :, None, :]
    kmat = (jnp.einsum('kngcp,ngpd->kngcd', cp_re[:T], bb_re)
            - jnp.einsum('kngcp,ngpd->kngcd', cp_im[:T], bb_im))
    j = jnp.arange(T)[:, None]
    t = jnp.arange(T)[None, :]
    lag = t - j
    kf = jnp.where((lag >= 0)[:, :, None, None, None], kmat[jnp.clip(lag, 0, T - 1), 0], 0.0)
    kb = jnp.where((lag <= 0)[:, :, None, None, None], kmat[jnp.clip(-lag, 0, T - 1), 1], 0.0)
    m5 = (kf + kb).transpose(2, 0, 4, 1, 3)
    eye = (jnp.eye(T, dtype=f32)[:, None, :, None] * jnp.eye(C, dtype=f32)[None, :, None, :])
    m5 = m5 + eye[None] * d_skip.astype(f32).reshape(G, 1, 1, 1, C)
    m = m5.reshape(G, T * C, T * C)

    def bst(n, powers):
        pr, pi = pw_re[powers, n], pw_im[powers, n]
        re = pr[..., None] * bb_re[n][None] - pi[..., None] * bb_im[n][None]
        im = pr[..., None] * bb_im[n][None] + pi[..., None] * bb_re[n][None]
        re = re.transpose(1, 0, 3, 2).reshape(G, T * C, P)
        im = im.transpose(1, 0, 3, 2).reshape(G, T * C, P)
        return jnp.concatenate([re, im, im, re], axis=-1)

    bst_f = bst(0, T - 1 - jnp.arange(T))
    bst_b = bst(1, jnp.arange(T))

    def cst(n, powers):
        re = cp_re[powers, n]
        im = cp_im[powers, n]
        re = re.transpose(1, 3, 0, 2).reshape(G, P, T * C)
        im = im.transpose(1, 3, 0, 2).reshape(G, P, T * C)
        return jnp.concatenate([re, -im], axis=1)

    cst_f = cst(0, jnp.arange(T) + 1)
    cst_b = cst(1, T - jnp.arange(T))
    ar, ai = pw_re[T], pw_im[T]
    coef = jnp.stack([jnp.concatenate([ar, ar], -1),
                      jnp.concatenate([-ai, ai], -1),
                      jnp.concatenate([ai, -ai], -1)], axis=2)
    return m.astype(bf16), bst_f.astype(bf16), bst_b.astype(bf16), cst_f.astype(bf16), cst_b.astype(bf16), coef


def _attn_kernel(lam_ref, qt_ref, k_ref, vt_ref, g_ref, za_ref, o_ref,
                 qbd_ref, acc_ref, p_ref, m_ref, a_ref, *, nkv, bq, bk):
    lam = lam_ref[0]
    half = HEAD_DIM
    ns = bq // LANES
    dv = 2 * HEAD_DIM
    zero = jnp.zeros((half, LANES), bf16)
    for st in range(ns):
        qs = qt_ref[0, :, st * LANES:(st + 1) * LANES]
        qbd_ref[st, 0:half, 0:LANES] = qs[0:half]
        qbd_ref[st, 0:half, LANES:2 * LANES] = zero
        qbd_ref[st, half:2 * half, 0:LANES] = zero
        qbd_ref[st, half:2 * half, LANES:2 * LANES] = qs[half:2 * half]
    acc_ref[...] = jnp.zeros_like(acc_ref)
    p_ref[...] = jnp.zeros_like(p_ref)
    a_ref[...] = jnp.ones_like(a_ref)
    m_ref[...] = jnp.full(m_ref.shape, -jnp.inf, f32)

    def values(j):
        return jnp.concatenate([vt_ref[0, 2 * j], vt_ref[0, 2 * j + 1]], axis=1)

    nsub = F32_ROWS

    def fold(st, vb):
        acc = acc_ref[st].reshape(V_ROWS // nsub, nsub, 2 * LANES) * a_ref[st][None]
        acc_ref[st] = acc.reshape(V_ROWS, 2 * LANES) + _dot(vb, p_ref[st])

    def colmax(s):
        mx = jnp.max(s.reshape(-1, nsub, 2 * LANES), axis=0)
        for sh in (4, 2, 1):
            mx = jnp.maximum(mx, pltpu.roll(mx, sh, 0))
        return mx

    def probs(s, m):
        x = s.reshape(-1, nsub, 2 * LANES) - m[None]
        return jnp.exp2(x.reshape(s.shape).astype(bf16))

    def body(j, carry):
        off = pl.multiple_of(j * bk, bk)
        ka = k_ref[0, pl.ds(off, bk // 2), :]
        kb = k_ref[0, pl.ds(off + bk // 2, bk // 2), :]
        vb = values(jnp.maximum(j - 1, 0))
        for st in range(ns):
            fold(st, vb)
            m_old = m_ref[st]
            s_a = _dot(ka, qbd_ref[st])
            m_a = jnp.maximum(m_old, colmax(s_a))
            p_a = probs(s_a, m_a)
            s_b = _dot(kb, qbd_ref[st])
            m_b = jnp.maximum(m_a, colmax(s_b))
            p_b = probs(s_b, m_b)
            corr = jnp.exp2(m_a - m_b)
            corr = jnp.concatenate([corr, corr], axis=0).astype(bf16)
            p_a = p_a.reshape(-1, BF16_ROWS, 2 * LANES) * corr[None]
            p_ref[st, 0:bk // 2] = p_a.reshape(bk // 2, 2 * LANES)
            p_ref[st, bk // 2:bk] = p_b
            a_ref[st] = jnp.exp2(m_old - m_b)
            m_ref[st] = m_b
        return carry

    lax.fori_loop(0, nkv, body, 0, unroll=2)

    vb_last = values(nkv - 1)
    for st in range(ns):
        fold(st, vb_last)
        acc = acc_ref[st, 0:dv]
        inv = 1.0 / acc_ref[st, dv:dv + 1]
        ot = acc[:, :LANES] * inv[:, :LANES] - lam * (acc[:, LANES:] * inv[:, LANES:])
        o = ot.T
        y = o * lax.rsqrt(jnp.mean(o * o, axis=-1, keepdims=True) + SUBLN_EPS) * g_ref[...] * (1.0 - LAM_INIT)
        za = za_ref[0, st * LANES:(st + 1) * LANES, :].astype(f32)
        o_ref[0, st * LANES:(st + 1) * LANES, :] = (y * (za * jax.nn.sigmoid(za))).astype(bf16)


def _attention(lam, qt, k, vt, subln_g, za, *, bq, bk):
    B, L, _ = k.shape
    nkv = L // bk
    nvt = vt.shape[1]
    return pl.pallas_call(
        functools.partial(_attn_kernel, nkv=nkv, bq=bq, bk=bk),
        grid=(B, N_HEADS, L // bq),
        in_specs=[
            pl.BlockSpec(memory_space=pltpu.SMEM),
            pl.BlockSpec((1, 2 * HEAD_DIM, bq), lambda b, h, i: (b, h, i)),
            pl.BlockSpec((1, L, 2 * HEAD_DIM), lambda b, h, i: (b, 0, h)),
            pl.BlockSpec((1, nvt, V_ROWS, bk // 2), lambda b, h, i: (b, 0, h, 0)),
            pl.BlockSpec((1, 2 * HEAD_DIM), lambda b, h, i: (0, 0)),
            pl.BlockSpec((1, bq, 2 * HEAD_DIM), lambda b, h, i: (b, i, h)),
        ],
        out_specs=pl.BlockSpec((1, bq, 2 * HEAD_DIM), lambda b, h, i: (b, i, h)),
        out_shape=jax.ShapeDtypeStruct((B, L, D_ATTN), bf16),
        scratch_shapes=[
            pltpu.VMEM((bq // LANES, 2 * HEAD_DIM, 2 * LANES), bf16),
            pltpu.VMEM((bq // LANES, V_ROWS, 2 * LANES), f32),
            pltpu.VMEM((bq // LANES, bk, 2 * LANES), bf16),
            pltpu.VMEM((bq // LANES, F32_ROWS, 2 * LANES), f32),
            pltpu.VMEM((bq // LANES, F32_ROWS, 2 * LANES), f32),
        ],
        compiler_params=_params("parallel", "parallel", "arbitrary"),
        name="attention",
    )(lam, qt, k, vt, subln_g, za)


def _out_proj_kernel(x_ref, yssm_ref, zs_ref, ya_ref, g_ref, wg_ref, wglu_ref, bglu_ref,
                     wb_ref, wout_ref, fg_ref, o_ref):
    x = x_ref[0]
    r = lax.rsqrt(jnp.mean(x * x, axis=-1, keepdims=True) + NORM_EPS)
    h = (x * r * g_ref[...]).astype(bf16)

    ys = jax.nn.gelu(yssm_ref[0].astype(f32))
    ys = ys * jax.nn.sigmoid(_dot(ys.astype(bf16), wglu_ref[...]) + bglu_ref[...])
    zs = zs_ref[0].astype(f32)
    ys = ys * (zs * jax.nn.sigmoid(zs))

    ps = _dot(ys.astype(bf16), wb_ref[0])
    merged = jax.nn.sigmoid(_dot(h, wg_ref[:, 0:D_MODEL])) * ps
    pa = _dot(ya_ref[0], wb_ref[1])
    merged = merged + jax.nn.sigmoid(_dot(h, wg_ref[:, D_MODEL:2 * D_MODEL])) * pa
    out = x + _dot(merged.astype(bf16), wout_ref[...])
    o_ref[0] = out * lax.rsqrt(jnp.mean(out * out, axis=-1, keepdims=True) + NORM_EPS) * fg_ref[...]


def _out_proj(x, yssm, zs, ya, norm_g, wg, wglu, bglu, wb, wout, final_g, *, tm):
    B, L, _ = x.shape
    tok = lambda b, i: (b, i, 0)
    cst = lambda b, i: (0, 0)
    return pl.pallas_call(
        _out_proj_kernel,
        grid=(B, L // tm),
        in_specs=[
            pl.BlockSpec((1, tm, D_MODEL), tok),
            pl.BlockSpec((1, tm, 512), tok),
            pl.BlockSpec((1, tm, 512), tok),
            pl.BlockSpec((1, tm, 512), tok),
            pl.BlockSpec((1, D_MODEL), cst),
            pl.BlockSpec(wg.shape, cst),
            pl.BlockSpec(wglu.shape, cst),
            pl.BlockSpec((1, D_SSM), cst),
            pl.BlockSpec(wb.shape, lambda b, i: (0, 0, 0)),
            pl.BlockSpec(wout.shape, cst),
            pl.BlockSpec((1, D_MODEL), cst),
        ],
        out_specs=pl.BlockSpec((1, tm, D_MODEL), tok),
        out_shape=jax.ShapeDtypeStruct((B, L, D_MODEL), x.dtype),
        compiler_params=_params("parallel", "parallel"),
        name="out_proj",
    )(x, yssm, zs, ya, norm_g, wg, wglu, bglu, wb, wout, final_g)


def _rotary_tables(L):
    half = HEAD_DIM // 2
    inv_freq = 1.0 / (ROPE_THETA ** (jnp.arange(0, half, dtype=f32) * 2.0 / HEAD_DIM))
    ang = jnp.arange(L, dtype=f32)[:, None] * inv_freq[None, :]
    cos, sin = jnp.cos(ang), jnp.sin(ang)
    ck = jnp.tile(cos, (1, LANES // half))
    sk = jnp.tile(sin, (1, LANES // half))
    scale = math.log2(math.e) / math.sqrt(HEAD_DIM)
    return ck, sk, cos.T * scale, sin.T * scale


def _trunk(x, w):
    B, L, _ = x.shape
    assert B == 8 and L % ATTN_BQ == 0, "scan state vregs hold one row per batch element"
    ck, sk, cq, sq = _rotary_tables(L)
    u, zs, k, za, qt, vt = _in_proj(x, w["norm_g"], w["wa"], w["wbt"], ck, sk, cq, sq, tb=PROJ_TOKENS)
    hf, gb = _ssm_state(u, w["bst_f"], w["bst_b"], w["coef"], nb=B)
    yssm = _ssm_out(u, hf, gb, w["m"], w["cst_f"], w["cst_b"], nb=B)
    ya = _attention(w["lam"], qt, k, vt, w["subln_g"], za, bq=ATTN_BQ, bk=ATTN_BK)
    return _out_proj(x, yssm, zs, ya, w["norm_g"], w["wg"], w["wglu"], w["bglu"], w["wb"], w["wout"],
                     w["final_g"], tm=TOKEN_BLOCK)


def _rotate_half_columns(wk):
    d = wk.shape[0]
    w4 = wk.reshape(d, D_ATTN // HEAD_DIM, 2, HEAD_DIM // 2)
    return jnp.stack([-w4[:, :, 1], w4[:, :, 0]], axis=2).reshape(d, D_ATTN)


def kernel(x_prompt, x_sample, norm_g, w_in, ssm_a_re, ssm_a_im, ssm_log_dt, ssm_b_re, ssm_b_im, ssm_c_re, ssm_c_im, ssm_d, w_glu, b_glu, lambda_q1, lambda_k1, lambda_q2, lambda_k2, subln_g, w_branch, w_out, final_g):
    li = 0
    wi = w_in[li].astype(f32)
    w_xs, w_zs = wi[:, 0:512], wi[:, 512:1024]
    w_q, w_k, w_v, w_za = wi[:, 1024:1536], wi[:, 1536:2048], wi[:, 2048:2560], wi[:, 2560:3072]
    m, bst_f, bst_b, cst_f, cst_b, coef = _ssm_tables(
        ssm_a_re[li], ssm_a_im[li], ssm_log_dt[li], ssm_b_re[li], ssm_b_im[li],
        ssm_c_re[li], ssm_c_im[li], ssm_d[li])
    nb = x_prompt.shape[0]
    lam = (jnp.exp(jnp.sum(lambda_q1[li].astype(f32) * lambda_k1[li].astype(f32)))
           - jnp.exp(jnp.sum(lambda_q2[li].astype(f32) * lambda_k2[li].astype(f32))) + LAM_INIT)
    w = dict(
        norm_g=norm_g[li].astype(f32).reshape(1, D_MODEL),
        wa=jnp.concatenate([w_xs, w_zs, w_k, _rotate_half_columns(w_k), w_za], axis=1).astype(bf16),
        wbt=jnp.concatenate([w_q, w_v], axis=1).T.astype(bf16),
        wg=wi[:, 3072:5120].astype(bf16),
        m=m, bst_f=bst_f, bst_b=bst_b, cst_f=cst_f, cst_b=cst_b,
        coef=jnp.broadcast_to(coef[:, :, :, None, :], (2, N_GROUPS, 3, nb, LANES)),
        lam=lam.reshape(1).astype(f32),
        subln_g=subln_g[li].astype(f32).reshape(1, 2 * HEAD_DIM),
        wglu=w_glu[li].astype(bf16),
        bglu=b_glu[li].astype(f32).reshape(1, D_SSM),
        wb=w_branch[li].astype(bf16),
        wout=w_out[li].astype(bf16),
        final_g=final_g.astype(f32).reshape(1, D_MODEL),
    )
    return (_trunk(x_prompt, w), _trunk(x_sample, w))
```

```python
import functools
import math

import jax
import jax.numpy as jnp
from jax import lax
from jax.experimental import pallas as pl
from jax.experimental.pallas import tpu as pltpu

D_MODEL = 1024
D_SSM = 512
SSM_GROUP = 16
N_GROUPS = 32
STATE = 64
D_ATTN = 512
N_HEADS = 4
HEAD_DIM = 64
ROPE_THETA = 10000.0
NORM_EPS = 1e-6
SUBLN_EPS = 1e-5
LAM_INIT = 0.8 - 0.6 * math.exp(-0.3 * 0)

CHUNK = 16
CHUNK_LANES = CHUNK * SSM_GROUP
LANES = 128
VMEM_LIMIT = 56 * 1024 * 1024

TOKEN_BLOCK = 512
ATTN_BQ = 2048
ATTN_STEPS_PER_TRIP = 4
ATTN_BK = 256
PROJ_TOKENS = ATTN_BK // 2
F32_ROWS = 8
BF16_ROWS = 16
V_ROWS = 2 * HEAD_DIM + BF16_ROWS
SSM_SEG_CHUNKS = 16
SSM_GROUP_BLOCK = 4
SSM_OUT_ROWS = 128

f32 = jnp.float32
bf16 = jnp.bfloat16


def _params(*sem):
    return pltpu.CompilerParams(dimension_semantics=sem, vmem_limit_bytes=VMEM_LIMIT)


def _dot(a, b):
    return jnp.dot(a, b, preferred_element_type=f32)


def _dot_nt(a, b):
    return lax.dot_general(a, b, (((1,), (1,)), ((), ())), preferred_element_type=f32)


def _segment_transpose8(vs):
    slot = lax.broadcasted_iota(jnp.int32, vs[0].shape, 1) // SSM_GROUP
    for d in (4, 2, 1):
        keep = (slot & d) == 0
        new = list(vs)
        for i in range(8):
            if i & d == 0:
                a, b = vs[i], vs[i + d]
                new[i] = jnp.where(keep, a, pltpu.roll(b, d * SSM_GROUP, 1))
                new[i + d] = jnp.where(keep, pltpu.roll(a, LANES - d * SSM_GROUP, 1), b)
        vs = new
    return vs


def _in_proj_kernel(x_ref, g_ref, wa_ref, wbt_ref, ck_ref, sk_ref, cq_ref, sq_ref,
                    u_ref, zs_ref, k_ref, za_ref, qt_ref, vt_ref, xs_s, *, nb, tb):
    rows = nb * tb
    x = x_ref[...].reshape(rows, D_MODEL)
    r = lax.rsqrt(jnp.mean(x * x, axis=-1, keepdims=True) + NORM_EPS)
    h = (x * r * g_ref[...]).astype(bf16)

    for tile in range(D_SSM // LANES):
        xs_s[tile] = _dot(h, wa_ref[:, tile * LANES:(tile + 1) * LANES])
    zs_ref[...] = _dot(h, wa_ref[:, 512:1024]).astype(bf16).reshape(nb, tb, D_SSM)
    kk = _dot(h, wa_ref[:, 1024:1536]).reshape(nb, tb, D_ATTN)
    kr = _dot(h, wa_ref[:, 1536:2048]).reshape(nb, tb, D_ATTN)
    ck = ck_ref[...]
    sk = sk_ref[...]
    for j in range(D_ATTN // LANES):
        sl = slice(j * LANES, (j + 1) * LANES)
        k_ref[:, :, sl] = (kk[:, :, sl] * ck + kr[:, :, sl] * sk).astype(bf16)
    za_ref[...] = _dot(h, wa_ref[:, 2048:2560]).astype(bf16).reshape(nb, tb, D_ATTN)

    pq = _dot_nt(wbt_ref[0:512, :], h)
    cq = jnp.tile(cq_ref[...], (1, nb))
    sq = jnp.tile(sq_ref[...], (1, nb))
    half = HEAD_DIM // 2
    for hb in range(D_ATTN // HEAD_DIM):
        x1 = pq[hb * HEAD_DIM:hb * HEAD_DIM + half]
        x2 = pq[hb * HEAD_DIM + half:(hb + 1) * HEAD_DIM]
        q1 = (x1 * cq - x2 * sq).astype(bf16)
        q2 = (x2 * cq + x1 * sq).astype(bf16)
        for b in range(nb):
            qt_ref[b, hb * HEAD_DIM:hb * HEAD_DIM + half, :] = q1[:, b * tb:(b + 1) * tb]
            qt_ref[b, hb * HEAD_DIM + half:(hb + 1) * HEAD_DIM, :] = q2[:, b * tb:(b + 1) * tb]
    pv = _dot_nt(wbt_ref[512:1024, :], h).astype(bf16)
    dv = 2 * HEAD_DIM
    ones = jnp.ones((V_ROWS - dv, tb), bf16)
    for b in range(nb):
        for hd in range(N_HEADS):
            vt_ref[b, 0, hd * V_ROWS:hd * V_ROWS + dv, :] = pv[hd * dv:(hd + 1) * dv, b * tb:(b + 1) * tb]
            vt_ref[b, 0, hd * V_ROWS + dv:(hd + 1) * V_ROWS, :] = ones

    nchunk = tb // CHUNK
    for tile in range(D_SSM // LANES):
        for hf in range(CHUNK_LANES // LANES):
            vs = []
            for t8 in range(8):
                t = 8 * hf + t8
                vs.append(jnp.concatenate(
                    [xs_s[tile, pl.ds(t + CHUNK * ch, nb, stride=tb), :] for ch in range(nchunk)],
                    axis=0))
            out = _segment_transpose8(vs)
            for g8 in range(8):
                u_ref[8 * tile + g8, :, hf * LANES:(hf + 1) * LANES] = out[g8].astype(bf16)


def _in_proj(x, norm_g, wa, wbt, ck, sk, cq, sq, *, tb):
    B, L, _ = x.shape
    nchunk = tb // CHUNK
    tok = lambda i: (0, i, 0)
    cst = lambda i: (0, 0)
    out_tok = jax.ShapeDtypeStruct((B, L, 512), bf16)
    return pl.pallas_call(
        functools.partial(_in_proj_kernel, nb=B, tb=tb),
        grid=(L // tb,),
        in_specs=[
            pl.BlockSpec((B, tb, D_MODEL), tok),
            pl.BlockSpec((1, D_MODEL), cst),
            pl.BlockSpec(wa.shape, cst),
            pl.BlockSpec(wbt.shape, cst),
            pl.BlockSpec((tb, LANES), lambda i: (i, 0)),
            pl.BlockSpec((tb, LANES), lambda i: (i, 0)),
            pl.BlockSpec((HEAD_DIM // 2, tb), lambda i: (0, i)),
            pl.BlockSpec((HEAD_DIM // 2, tb), lambda i: (0, i)),
        ],
        out_specs=[
            pl.BlockSpec((N_GROUPS, nchunk * B, CHUNK_LANES), tok),
            pl.BlockSpec((B, tb, 512), tok),
            pl.BlockSpec((B, tb, 512), tok),
            pl.BlockSpec((B, tb, 512), tok),
            pl.BlockSpec((B, 512, tb), lambda i: (0, 0, i)),
            pl.BlockSpec((B, 1, N_HEADS * V_ROWS, tb), lambda i: (0, i, 0, 0)),
        ],
        out_shape=[jax.ShapeDtypeStruct((N_GROUPS, (L // CHUNK) * B, CHUNK_LANES), bf16),
                   out_tok, out_tok, out_tok,
                   jax.ShapeDtypeStruct((B, 512, L), bf16),
                   jax.ShapeDtypeStruct((B, L // tb, N_HEADS * V_ROWS, tb), bf16)],
        scratch_shapes=[pltpu.VMEM((D_SSM // LANES, B * tb, LANES), f32)],
        compiler_params=_params("parallel"),
        name="in_proj",
    )(x, norm_g, wa, wbt, ck, sk, cq, sq)


def _ssm_state_kernel(uf_ref, ub_ref, bf_ref, bb_ref, coef_ref, hf_ref, gb_ref, s_ref, st_ref,
                      *, nc, nb, gblk):
    @pl.when(pl.program_id(0) == 0)
    def _():
        st_ref[...] = jnp.zeros_like(st_ref)

    for g0 in range(0, N_GROUPS, gblk):
        for gi in range(gblk):
            s_ref[0, gi] = _dot(uf_ref[g0 + gi], bf_ref[g0 + gi])
            s_ref[1, gi] = _dot(ub_ref[g0 + gi], bb_ref[g0 + gi])
        gs = slice(g0, g0 + gblk)
        af1, af2, af3 = coef_ref[0, gs, 0], coef_ref[0, gs, 1], coef_ref[0, gs, 2]
        ab1, ab2, ab3 = coef_ref[1, gs, 0], coef_ref[1, gs, 1], coef_ref[1, gs, 2]

        def body(i, carry):
            hf, wf, hb, wb = carry
            rf = pl.multiple_of(i * nb, nb)
            rb = pl.multiple_of((nc - 1 - i) * nb, nb)
            hf_ref[gs, pl.ds(rf, nb), :] = hf
            gb_ref[gs, pl.ds(rb, nb), :] = hb
            sf = s_ref[0, :, pl.ds(rf, nb), :]
            sb = s_ref[1, :, pl.ds(rb, nb), :]
            hf2 = af1 * hf + af2 * wf + sf[..., :LANES]
            wf2 = af1 * wf + af3 * hf + sf[..., LANES:]
            hb2 = ab1 * hb + ab2 * wb + sb[..., :LANES]
            wb2 = ab1 * wb + ab3 * hb + sb[..., LANES:]
            return hf2, wf2, hb2, wb2

        init = (st_ref[0, 0, gs], st_ref[0, 1, gs], st_ref[1, 0, gs], st_ref[1, 1, gs])
        hf, wf, hb, wb = lax.fori_loop(0, nc, body, init)
        st_ref[0, 0, gs] = hf
        st_ref[0, 1, gs] = wf
        st_ref[1, 0, gs] = hb
        st_ref[1, 1, gs] = wb


def _ssm_state(u, bst_f, bst_b, coef, *, nb):
    G, rows, _ = u.shape
    nc = SSM_SEG_CHUNKS
    seg_rows = nc * nb
    nseg = rows // seg_rows
    gblk = SSM_GROUP_BLOCK
    fwd = lambda i: (0, i, 0)
    bwd = lambda i: (0, nseg - 1 - i, 0)
    cst3 = lambda i: (0, 0, 0)
    return pl.pallas_call(
        functools.partial(_ssm_state_kernel, nc=nc, nb=nb, gblk=gblk),
        grid=(nseg,),
        in_specs=[
            pl.BlockSpec((G, seg_rows, CHUNK_LANES), fwd),
            pl.BlockSpec((G, seg_rows, CHUNK_LANES), bwd),
            pl.BlockSpec(bst_f.shape, cst3),
            pl.BlockSpec(bst_b.shape, cst3),
            pl.BlockSpec(coef.shape, lambda i: (0, 0, 0, 0, 0)),
        ],
        out_specs=[
            pl.BlockSpec((G, seg_rows, LANES), fwd),
            pl.BlockSpec((G, seg_rows, LANES), bwd),
        ],
        out_shape=[jax.ShapeDtypeStruct((G, rows, LANES), f32),
                   jax.ShapeDtypeStruct((G, rows, LANES), f32)],
        scratch_shapes=[
            pltpu.VMEM((2, gblk, seg_rows, CHUNK_LANES), f32),
            pltpu.VMEM((2, 2, G, nb, LANES), f32),
        ],
        compiler_params=_params("arbitrary"),
        name="ssm_state",
    )(u, u, bst_f, bst_b, coef)


def _ssm_out_kernel(u_ref, hf_ref, gb_ref, m_ref, cf_ref, cb_ref, y_ref, nat_s, *, nb, nchunk):
    tok = nchunk * CHUNK
    for tile in range(D_SSM // LANES):
        ys = []
        for g8 in range(8):
            g = 8 * tile + g8
            y = _dot(u_ref[g], m_ref[g])
            y = y + _dot(hf_ref[g].astype(bf16), cf_ref[g])
            y = y + _dot(gb_ref[g].astype(bf16), cb_ref[g])
            ys.append(y)
        for hf in range(CHUNK_LANES // LANES):
            out = _segment_transpose8([y[:, hf * LANES:(hf + 1) * LANES] for y in ys])
            for t8 in range(8):
                t = 8 * hf + t8
                for ch in range(nchunk):
                    nat_s[tile, pl.ds(t + CHUNK * ch, nb, stride=tok), :] = out[t8][ch * nb:(ch + 1) * nb]
        y_ref[:, :, tile * LANES:(tile + 1) * LANES] = nat_s[tile].reshape(nb, tok, LANES).astype(bf16)


def _ssm_out(u, hf, gb, m, cst_f, cst_b, *, nb):
    G, rows, _ = u.shape
    rb = min(SSM_OUT_ROWS, rows)
    nchunk = rb // nb
    tok = nchunk * CHUNK
    blk = lambda i: (0, i, 0)
    cst3 = lambda i: (0, 0, 0)
    return pl.pallas_call(
        functools.partial(_ssm_out_kernel, nb=nb, nchunk=nchunk),
        grid=(rows // rb,),
        in_specs=[
            pl.BlockSpec((G, rb, CHUNK_LANES), blk),
            pl.BlockSpec((G, rb, LANES), blk),
            pl.BlockSpec((G, rb, LANES), blk),
            pl.BlockSpec(m.shape, cst3),
            pl.BlockSpec(cst_f.shape, cst3),
            pl.BlockSpec(cst_b.shape, cst3),
        ],
        out_specs=pl.BlockSpec((nb, tok, D_SSM), blk),
        out_shape=jax.ShapeDtypeStruct((nb, (rows // nb) * CHUNK, D_SSM), bf16),
        scratch_shapes=[pltpu.VMEM((D_SSM // LANES, nb * tok, LANES), f32)],
        compiler_params=_params("parallel"),
        name="ssm_out",
    )(u, hf, gb, m, cst_f, cst_b)


def _ssm_tables(a_re, a_im, log_dt, b_re, b_im, c_re, c_im, d_skip):
    T, G, P, C = CHUNK, N_GROUPS, STATE, SSM_GROUP
    a_re, a_im, log_dt = a_re.astype(f32), a_im.astype(f32), log_dt.astype(f32)
    dt = jnp.exp(log_dt)[..., None]
    ks = jnp.arange(T + 1, dtype=f32)[:, None, None, None]
    mag = jnp.exp(ks * (a_re * dt))
    pw_re = mag * jnp.cos(ks * (a_im * dt))
    pw_im = mag * jnp.sin(ks * (a_im * dt))
    n_re, n_im = pw_re[1] - 1.0, pw_im[1]
    den = a_re * a_re + a_im * a_im
    co_re = (n_re * a_re + n_im * a_im) / den
    co_im = (n_im * a_re - n_re * a_im) / den
    b_re, b_im = b_re.astype(f32), b_im.astype(f32)
    bb_re = co_re[..., None] * b_re - co_im[..., None] * b_im
    bb_im = co_re[..., None] * b_im + co_im[..., None] * b_re
    c_re, c_im = c_re.astype(f32), c_im.astype(f32)
    cp_re = c_re[None] * pw_re[:, :, :, None, :] - c_im[None] * pw_im[:, :, :, None, :]
    cp_im = c_re[None] * pw_im[:, :, :, None, :] + c_im[None] * pw_re[:, :, :, None, :]
    kmat = (jnp.einsum('kngcp,ngpd->kngcd', cp_re[:T], bb_re)
            - jnp.einsum('kngcp,ngpd->kngcd', cp_im[:T], bb_im))
    j = jnp.arange(T)[:, None]
    t = jnp.arange(T)[None, :]
    lag = t - j
    kf = jnp.where((lag >= 0)[:, :, None, None, None], kmat[jnp.clip(lag, 0, T - 1), 0], 0.0)
    kb = jnp.where((lag <= 0)[:, :, None, None, None], kmat[jnp.clip(-lag, 0, T - 1), 1], 0.0)
    m5 = (kf + kb).transpose(2, 0, 4, 1, 3)
    eye = (jnp.eye(T, dtype=f32)[:, None, :, None] * jnp.eye(C, dtype=f32)[None, :, None, :])
    m5 = m5 + eye[None] * d_skip.astype(f32).reshape(G, 1, 1, 1, C)
    m = m5.reshape(G, T * C, T * C)

    def bst(n, powers):
        pr, pi = pw_re[powers, n], pw_im[powers, n]
        re = pr[..., None] * bb_re[n][None] - pi[..., None] * bb_im[n][None]
        im = pr[..., None] * bb_im[n][None] + pi[..., None] * bb_re[n][None]
        re = re.transpose(1, 0, 3, 2).reshape(G, T * C, P)
        im = im.transpose(1, 0, 3, 2).reshape(G, T * C, P)
        return jnp.concatenate([re, im, im, re], axis=-1)

    bst_f = bst(0, T - 1 - jnp.arange(T))
    bst_b = bst(1, jnp.arange(T))

    def cst(n, powers):
        re = cp_re[powers, n]
        im = cp_im[powers, n]
        re = re.transpose(1, 3, 0, 2).reshape(G, P, T * C)
        im = im.transpose(1, 3, 0, 2).reshape(G, P, T * C)
        return jnp.concatenate([re, -im], axis=1)

    cst_f = cst(0, jnp.arange(T) + 1)
    cst_b = cst(1, T - jnp.arange(T))
    ar, ai = pw_re[T], pw_im[T]
    coef = jnp.stack([jnp.concatenate([ar, ar], -1),
                      jnp.concatenate([-ai, ai], -1),
                      jnp.concatenate([ai, -ai], -1)], axis=2)
    return m.astype(bf16), bst_f.astype(bf16), bst_b.astype(bf16), cst_f.astype(bf16), cst_b.astype(bf16), coef


def _attn_kernel(lam_ref, qt_ref, k_ref, vt_ref, g_ref, za_ref, o_ref,
                 qbd_ref, acc_ref, s_ref, p_ref, m_ref, a_ref, *, nkv, bq, bk):
    lam = lam_ref[0]
    half = HEAD_DIM
    ns = bq // LANES
    dv = 2 * HEAD_DIM
    nsub = F32_ROWS
    zero = jnp.zeros((half, LANES), bf16)
    for st in range(ns):
        qs = qt_ref[0, :, st * LANES:(st + 1) * LANES]
        qbd_ref[st, 0:half, 0:LANES] = qs[0:half]
        qbd_ref[st, 0:half, LANES:2 * LANES] = zero
        qbd_ref[st, half:2 * half, 0:LANES] = zero
        qbd_ref[st, half:2 * half, LANES:2 * LANES] = qs[half:2 * half]
    acc_ref[...] = jnp.zeros_like(acc_ref)
    p_ref[...] = jnp.zeros_like(p_ref)
    a_ref[...] = jnp.ones_like(a_ref)
    m_ref[...] = jnp.full(m_ref.shape, -jnp.inf, f32)

    def values(j):
        return jnp.concatenate([vt_ref[0, 2 * j], vt_ref[0, 2 * j + 1]], axis=1)

    def keys(j):
        return k_ref[0, pl.ds(pl.multiple_of(j * bk, bk), bk), :]

    def score(st, kb, buf):
        s_ref[buf, st] = _dot(kb, qbd_ref[st]).astype(bf16)

    def soften(st, buf):
        for hl in range(2):
            ls = slice(hl * LANES, (hl + 1) * LANES)
            sb = s_ref[buf, st, :, ls].reshape(bk // BF16_ROWS, BF16_ROWS, LANES)
            mx = jnp.max(sb, axis=0).astype(f32)
            mx = jnp.maximum(mx[0:nsub], mx[nsub:2 * nsub])
            for sh in (4, 2, 1):
                mx = jnp.maximum(mx, pltpu.roll(mx, sh, 0))
            m_old = m_ref[st, :, ls]
            m_new = jnp.maximum(m_old, mx)
            m16 = jnp.concatenate([m_new, m_new], axis=0).astype(bf16)
            p_ref[buf, st, :, ls] = jnp.exp2(sb - m16[None]).reshape(bk, LANES)
            a_ref[buf, st, :, ls] = jnp.exp2(m_old - m_new)
            m_ref[st, :, ls] = m_new

    def fold(st, vb, buf):
        acc = acc_ref[st].reshape(V_ROWS // nsub, nsub, 2 * LANES) * a_ref[buf, st][None]
        acc_ref[st] = acc.reshape(V_ROWS, 2 * LANES) + _dot(vb, p_ref[buf, st])

    def step(j, cur):
        nxt = 1 - cur
        kb = keys(jnp.minimum(j + 1, nkv - 1))
        vb = values(jnp.maximum(j - 1, 0))
        for st in range(ns):
            fold(st, vb, nxt)
            score(st, kb, nxt)
        for st in range(ns):
            soften(st, cur)

    kb0 = keys(0)
    for st in range(ns):
        score(st, kb0, 0)

    def body(i, carry):
        for u in range(ATTN_STEPS_PER_TRIP):
            step(ATTN_STEPS_PER_TRIP * i + u, u % 2)
        return carry

    lax.fori_loop(0, nkv // ATTN_STEPS_PER_TRIP, body, 0)

    vb_last = values(nkv - 1)
    for st in range(ns):
        fold(st, vb_last, (nkv - 1) % 2)
        acc = acc_ref[st, 0:dv]
        inv = 1.0 / acc_ref[st, dv:dv + 1]
        ot = acc[:, :LANES] * inv[:, :LANES] - lam * (acc[:, LANES:] * inv[:, LANES:])
        o = ot.T
        y = o * lax.rsqrt(jnp.mean(o * o, axis=-1, keepdims=True) + SUBLN_EPS) * g_ref[...] * (1.0 - LAM_INIT)
        za = za_ref[0, st * LANES:(st + 1) * LANES, :].astype(f32)
        o_ref[0, st * LANES:(st + 1) * LANES, :] = (y * (za * jax.nn.sigmoid(za))).astype(bf16)


def _attention(lam, qt, k, vt, subln_g, za, *, bq, bk):
    B, L, _ = k.shape
    nkv = L // bk
    nvt = vt.shape[1]
    return pl.pallas_call(
        functools.partial(_attn_kernel, nkv=nkv, bq=bq, bk=bk),
        grid=(B, N_HEADS, L // bq),
        in_specs=[
            pl.BlockSpec(memory_space=pltpu.SMEM),
            pl.BlockSpec((1, 2 * HEAD_DIM, bq), lambda b, h, i: (b, h, i)),
            pl.BlockSpec((1, L, 2 * HEAD_DIM), lambda b, h, i: (b, 0, h)),
            pl.BlockSpec((1, nvt, V_ROWS, bk // 2), lambda b, h, i: (b, 0, h, 0)),
            pl.BlockSpec((1, 2 * HEAD_DIM), lambda b, h, i: (0, 0)),
            pl.BlockSpec((1, bq, 2 * HEAD_DIM), lambda b, h, i: (b, i, h)),
        ],
        out_specs=pl.BlockSpec((1, bq, 2 * HEAD_DIM), lambda b, h, i: (b, i, h)),
        out_shape=jax.ShapeDtypeStruct((B, L, D_ATTN), bf16),
        scratch_shapes=[
            pltpu.VMEM((bq // LANES, 2 * HEAD_DIM, 2 * LANES), bf16),
            pltpu.VMEM((bq // LANES, V_ROWS, 2 * LANES), f32),
            pltpu.VMEM((2, bq // LANES, bk, 2 * LANES), bf16),
            pltpu.VMEM((2, bq // LANES, bk, 2 * LANES), bf16),
            pltpu.VMEM((bq // LANES, F32_ROWS, 2 * LANES), f32),
            pltpu.VMEM((2, bq // LANES, F32_ROWS, 2 * LANES), f32),
        ],
        compiler_params=_params("parallel", "parallel", "arbitrary"),
        name="attention",
    )(lam, qt, k, vt, subln_g, za)


def _out_proj_kernel(x_ref, yssm_ref, zs_ref, ya_ref, g_ref, wg_ref, wglu_ref, bglu_ref,
                     wb_ref, wout_ref, fg_ref, o_ref):
    x = x_ref[0]
    r = lax.rsqrt(jnp.mean(x * x, axis=-1, keepdims=True) + NORM_EPS)
    h = (x * r * g_ref[...]).astype(bf16)

    ys = jax.nn.gelu(yssm_ref[0].astype(f32))
    ys = ys * jax.nn.sigmoid(_dot(ys.astype(bf16), wglu_ref[...]) + bglu_ref[...])
    zs = zs_ref[0].astype(f32)
    ys = ys * (zs * jax.nn.sigmoid(zs))

    ps = _dot(ys.astype(bf16), wb_ref[0])
    merged = jax.nn.sigmoid(_dot(h, wg_ref[:, 0:D_MODEL])) * ps
    pa = _dot(ya_ref[0], wb_ref[1])
    merged = merged + jax.nn.sigmoid(_dot(h, wg_ref[:, D_MODEL:2 * D_MODEL])) * pa
    out = x + _dot(merged.astype(bf16), wout_ref[...])
    o_ref[0] = out * lax.rsqrt(jnp.mean(out * out, axis=-1, keepdims=True) + NORM_EPS) * fg_ref[...]


def _out_proj(x, yssm, zs, ya, norm_g, wg, wglu, bglu, wb, wout, final_g, *, tm):
    B, L, _ = x.shape
    tok = lambda b, i: (b, i, 0)
    cst = lambda b, i: (0, 0)
    return pl.pallas_call(
        _out_proj_kernel,
        grid=(B, L // tm),
        in_specs=[
            pl.BlockSpec((1, tm, D_MODEL), tok),
            pl.BlockSpec((1, tm, 512), tok),
            pl.BlockSpec((1, tm, 512), tok),
            pl.BlockSpec((1, tm, 512), tok),
            pl.BlockSpec((1, D_MODEL), cst),
            pl.BlockSpec(wg.shape, cst),
            pl.BlockSpec(wglu.shape, cst),
            pl.BlockSpec((1, D_SSM), cst),
            pl.BlockSpec(wb.shape, lambda b, i: (0, 0, 0)),
            pl.BlockSpec(wout.shape, cst),
            pl.BlockSpec((1, D_MODEL), cst),
        ],
        out_specs=pl.BlockSpec((1, tm, D_MODEL), tok),
        out_shape=jax.ShapeDtypeStruct((B, L, D_MODEL), x.dtype),
        compiler_params=_params("parallel", "parallel"),
        name="out_proj",
    )(x, yssm, zs, ya, norm_g, wg, wglu, bglu, wb, wout, final_g)


def _rotary_tables(L):
    half = HEAD_DIM // 2
    inv_freq = 1.0 / (ROPE_THETA ** (jnp.arange(0, half, dtype=f32) * 2.0 / HEAD_DIM))
    ang = jnp.arange(L, dtype=f32)[:, None] * inv_freq[None, :]
    cos, sin = jnp.cos(ang), jnp.sin(ang)
    ck = jnp.tile(cos, (1, LANES // half))
    sk = jnp.tile(sin, (1, LANES // half))
    scale = math.log2(math.e) / math.sqrt(HEAD_DIM)
    return ck, sk, cos.T * scale, sin.T * scale


def _trunk(x, w):
    B, L, _ = x.shape
    assert B == 8 and L % ATTN_BQ == 0, "scan state vregs hold one row per batch element"
    ck, sk, cq, sq = _rotary_tables(L)
    u, zs, k, za, qt, vt = _in_proj(x, w["norm_g"], w["wa"], w["wbt"], ck, sk, cq, sq, tb=PROJ_TOKENS)
    hf, gb = _ssm_state(u, w["bst_f"], w["bst_b"], w["coef"], nb=B)
    yssm = _ssm_out(u, hf, gb, w["m"], w["cst_f"], w["cst_b"], nb=B)
    ya = _attention(w["lam"], qt, k, vt, w["subln_g"], za, bq=ATTN_BQ, bk=ATTN_BK)
    return _out_proj(x, yssm, zs, ya, w["norm_g"], w["wg"], w["wglu"], w["bglu"], w["wb"], w["wout"],
                     w["final_g"], tm=TOKEN_BLOCK)


def _rotate_half_columns(wk):
    d = wk.shape[0]
    w4 = wk.reshape(d, D_ATTN // HEAD_DIM, 2, HEAD_DIM // 2)
    return jnp.stack([-w4[:, :, 1], w4[:, :, 0]], axis=2).reshape(d, D_ATTN)


def kernel(x_prompt, x_sample, norm_g, w_in, ssm_a_re, ssm_a_im, ssm_log_dt, ssm_b_re, ssm_b_im, ssm_c_re, ssm_c_im, ssm_d, w_glu, b_glu, lambda_q1, lambda_k1, lambda_q2, lambda_k2, subln_g, w_branch, w_out, final_g):
    li = 0
    wi = w_in[li].astype(f32)
    w_xs, w_zs = wi[:, 0:512], wi[:, 512:1024]
    w_q, w_k, w_v, w_za = wi[:, 1024:1536], wi[:, 1536:2048], wi[:, 2048:2560], wi[:, 2560:3072]
    m, bst_f, bst_b, cst_f, cst_b, coef = _ssm_tables(
        ssm_a_re[li], ssm_a_im[li], ssm_log_dt[li], ssm_b_re[li], ssm_b_im[li],
        ssm_c_re[li], ssm_c_im[li], ssm_d[li])
    nb = x_prompt.shape[0]
    lam = (jnp.exp(jnp.sum(lambda_q1[li].astype(f32) * lambda_k1[li].astype(f32)))
           - jnp.exp(jnp.sum(lambda_q2[li].astype(f32) * lambda_k2[li].astype(f32))) + LAM_INIT)
    w = dict(
        norm_g=norm_g[li].astype(f32).reshape(1, D_MODEL),
        wa=jnp.concatenate([w_xs, w_zs, w_k, _rotate_half_columns(w_k), w_za], axis=1).astype(bf16),
        wbt=jnp.concatenate([w_q, w_v], axis=1).T.astype(bf16),
        wg=wi[:, 3072:5120].astype(bf16),
        m=m, bst_f=bst_f, bst_b=bst_b, cst_f=cst_f, cst_b=cst_b,
        coef=jnp.broadcast_to(coef[:, :, :, None, :], (2, N_GROUPS, 3, nb, LANES)),
        lam=lam.reshape(1).astype(f32),
        subln_g=subln_g[li].astype(f32).reshape(1, 2 * HEAD_DIM),
        wglu=w_glu[li].astype(bf16),
        bglu=b_glu[li].astype(f32).reshape(1, D_SSM),
        wb=w_branch[li].astype(bf16),
        wout=w_out[li].astype(bf16),
        final_g=final_g.astype(f32).reshape(1, D_MODEL),
    )
    return (_trunk(x_prompt, w), _trunk(x_sample, w))
```

```python
import functools
import math

import jax
import jax.numpy as jnp
from jax import lax
from jax.experimental import pallas as pl
from jax.experimental.pallas import tpu as pltpu

D_MODEL = 1024
D_SSM = 512
SSM_GROUP = 16
N_GROUPS = 32
STATE = 64
D_ATTN = 512
N_HEADS = 4
HEAD_DIM = 64
ROPE_THETA = 10000.0
NORM_EPS = 1e-6
SUBLN_EPS = 1e-5
LAM_INIT = 0.8 - 0.6 * math.exp(-0.3 * 0)

CHUNK = 16
CHUNK_LANES = CHUNK * SSM_GROUP
LANES = 128
VMEM_LIMIT = 56 * 1024 * 1024

TOKEN_BLOCK = 512
ATTN_BQ = 2048
ATTN_BK = 256
PROJ_TOKENS = ATTN_BK // 2
F32_ROWS = 8
BF16_ROWS = 16
V_ROWS = 2 * HEAD_DIM + BF16_ROWS
SSM_SEG_CHUNKS = 16
SSM_GROUP_BLOCK = 4
SSM_OUT_ROWS = 128

f32 = jnp.float32
bf16 = jnp.bfloat16


def _params(*sem):
    return pltpu.CompilerParams(dimension_semantics=sem, vmem_limit_bytes=VMEM_LIMIT)


def _dot(a, b):
    return jnp.dot(a, b, preferred_element_type=f32)


def _dot_nt(a, b):
    return lax.dot_general(a, b, (((1,), (1,)), ((), ())), preferred_element_type=f32)


def _segment_transpose8(vs):
    slot = lax.broadcasted_iota(jnp.int32, vs[0].shape, 1) // SSM_GROUP
    for d in (4, 2, 1):
        keep = (slot & d) == 0
        new = list(vs)
        for i in range(8):
            if i & d == 0:
                a, b = vs[i], vs[i + d]
                new[i] = jnp.where(keep, a, pltpu.roll(b, d * SSM_GROUP, 1))
                new[i + d] = jnp.where(keep, pltpu.roll(a, LANES - d * SSM_GROUP, 1), b)
        vs = new
    return vs


def _in_proj_kernel(x_ref, g_ref, wa_ref, wbt_ref, ck_ref, sk_ref, cq_ref, sq_ref,
                    u_ref, zs_ref, k_ref, za_ref, qt_ref, vt_ref, xs_s, *, nb, tb):
    rows = nb * tb
    x = x_ref[...].reshape(rows, D_MODEL)
    r = lax.rsqrt(jnp.mean(x * x, axis=-1, keepdims=True) + NORM_EPS)
    h = (x * r * g_ref[...]).astype(bf16)

    for tile in range(D_SSM // LANES):
        xs_s[tile] = _dot(h, wa_ref[:, tile * LANES:(tile + 1) * LANES])
    zs_ref[...] = _dot(h, wa_ref[:, 512:1024]).astype(bf16).reshape(nb, tb, D_SSM)
    kk = _dot(h, wa_ref[:, 1024:1536]).reshape(nb, tb, D_ATTN)
    kr = _dot(h, wa_ref[:, 1536:2048]).reshape(nb, tb, D_ATTN)
    ck = ck_ref[...]
    sk = sk_ref[...]
    for j in range(D_ATTN // LANES):
        sl = slice(j * LANES, (j + 1) * LANES)
        k_ref[:, :, sl] = (kk[:, :, sl] * ck + kr[:, :, sl] * sk).astype(bf16)
    za_ref[...] = _dot(h, wa_ref[:, 2048:2560]).astype(bf16).reshape(nb, tb, D_ATTN)

    pq = _dot_nt(wbt_ref[0:512, :], h)
    cq = jnp.tile(cq_ref[...], (1, nb))
    sq = jnp.tile(sq_ref[...], (1, nb))
    half = HEAD_DIM // 2
    for hb in range(D_ATTN // HEAD_DIM):
        x1 = pq[hb * HEAD_DIM:hb * HEAD_DIM + half]
        x2 = pq[hb * HEAD_DIM + half:(hb + 1) * HEAD_DIM]
        q1 = (x1 * cq - x2 * sq).astype(bf16)
        q2 = (x2 * cq + x1 * sq).astype(bf16)
        for b in range(nb):
            qt_ref[b, hb * HEAD_DIM:hb * HEAD_DIM + half, :] = q1[:, b * tb:(b + 1) * tb]
            qt_ref[b, hb * HEAD_DIM + half:(hb + 1) * HEAD_DIM, :] = q2[:, b * tb:(b + 1) * tb]
    pv = _dot_nt(wbt_ref[512:1024, :], h).astype(bf16)
    dv = 2 * HEAD_DIM
    ones = jnp.ones((V_ROWS - dv, tb), bf16)
    for b in range(nb):
        for hd in range(N_HEADS):
            vt_ref[b, 0, hd * V_ROWS:hd * V_ROWS + dv, :] = pv[hd * dv:(hd + 1) * dv, b * tb:(b + 1) * tb]
            vt_ref[b, 0, hd * V_ROWS + dv:(hd + 1) * V_ROWS, :] = ones

    nchunk = tb // CHUNK
    for tile in range(D_SSM // LANES):
        for hf in range(CHUNK_LANES // LANES):
            vs = []
            for t8 in range(8):
                t = 8 * hf + t8
                vs.append(jnp.concatenate(
                    [xs_s[tile, pl.ds(t + CHUNK * ch, nb, stride=tb), :] for ch in range(nchunk)],
                    axis=0))
            out = _segment_transpose8(vs)
            for g8 in range(8):
                u_ref[8 * tile + g8, :, hf * LANES:(hf + 1) * LANES] = out[g8].astype(bf16)


def _in_proj(x, norm_g, wa, wbt, ck, sk, cq, sq, *, tb):
    B, L, _ = x.shape
    nchunk = tb // CHUNK
    tok = lambda i: (0, i, 0)
    cst = lambda i: (0, 0)
    out_tok = jax.ShapeDtypeStruct((B, L, 512), bf16)
    return pl.pallas_call(
        functools.partial(_in_proj_kernel, nb=B, tb=tb),
        grid=(L // tb,),
        in_specs=[
            pl.BlockSpec((B, tb, D_MODEL), tok),
            pl.BlockSpec((1, D_MODEL), cst),
            pl.BlockSpec(wa.shape, cst),
            pl.BlockSpec(wbt.shape, cst),
            pl.BlockSpec((tb, LANES), lambda i: (i, 0)),
            pl.BlockSpec((tb, LANES), lambda i: (i, 0)),
            pl.BlockSpec((HEAD_DIM // 2, tb), lambda i: (0, i)),
            pl.BlockSpec((HEAD_DIM // 2, tb), lambda i: (0, i)),
        ],
        out_specs=[
            pl.BlockSpec((N_GROUPS, nchunk * B, CHUNK_LANES), tok),
            pl.BlockSpec((B, tb, 512), tok),
            pl.BlockSpec((B, tb, 512), tok),
            pl.BlockSpec((B, tb, 512), tok),
            pl.BlockSpec((B, 512, tb), lambda i: (0, 0, i)),
            pl.BlockSpec((B, 1, N_HEADS * V_ROWS, tb), lambda i: (0, i, 0, 0)),
        ],
        out_shape=[jax.ShapeDtypeStruct((N_GROUPS, (L // CHUNK) * B, CHUNK_LANES), bf16),
                   out_tok, out_tok, out_tok,
                   jax.ShapeDtypeStruct((B, 512, L), bf16),
                   jax.ShapeDtypeStruct((B, L // tb, N_HEADS * V_ROWS, tb), bf16)],
        scratch_shapes=[pltpu.VMEM((D_SSM // LANES, B * tb, LANES), f32)],
        compiler_params=_params("parallel"),
        name="in_proj",
    )(x, norm_g, wa, wbt, ck, sk, cq, sq)


def _ssm_state_kernel(uf_ref, ub_ref, bf_ref, bb_ref, coef_ref, hf_ref, gb_ref, s_ref, st_ref,
                      *, nc, nb, gblk):
    @pl.when(pl.program_id(0) == 0)
    def _():
        st_ref[...] = jnp.zeros_like(st_ref)

    for g0 in range(0, N_GROUPS, gblk):
        for gi in range(gblk):
            s_ref[0, gi] = _dot(uf_ref[g0 + gi], bf_ref[g0 + gi])
            s_ref[1, gi] = _dot(ub_ref[g0 + gi], bb_ref[g0 + gi])
        gs = slice(g0, g0 + gblk)
        af1, af2, af3 = coef_ref[0, gs, 0], coef_ref[0, gs, 1], coef_ref[0, gs, 2]
        ab1, ab2, ab3 = coef_ref[1, gs, 0], coef_ref[1, gs, 1], coef_ref[1, gs, 2]

        def body(i, carry):
            hf, wf, hb, wb = carry
            rf = pl.multiple_of(i * nb, nb)
            rb = pl.multiple_of((nc - 1 - i) * nb, nb)
            hf_ref[gs, pl.ds(rf, nb), :] = hf
            gb_ref[gs, pl.ds(rb, nb), :] = hb
            sf = s_ref[0, :, pl.ds(rf, nb), :]
            sb = s_ref[1, :, pl.ds(rb, nb), :]
            hf2 = af1 * hf + af2 * wf + sf[..., :LANES]
            wf2 = af1 * wf + af3 * hf + sf[..., LANES:]
            hb2 = ab1 * hb + ab2 * wb + sb[..., :LANES]
            wb2 = ab1 * wb + ab3 * hb + sb[..., LANES:]
            return hf2, wf2, hb2, wb2

        init = (st_ref[0, 0, gs], st_ref[0, 1, gs], st_ref[1, 0, gs], st_ref[1, 1, gs])
        hf, wf, hb, wb = lax.fori_loop(0, nc, body, init)
        st_ref[0, 0, gs] = hf
        st_ref[0, 1, gs] = wf
        st_ref[1, 0, gs] = hb
        st_ref[1, 1, gs] = wb


def _ssm_state(u, bst_f, bst_b, coef, *, nb):
    G, rows, _ = u.shape
    nc = SSM_SEG_CHUNKS
    seg_rows = nc * nb
    nseg = rows // seg_rows
    gblk = SSM_GROUP_BLOCK
    fwd = lambda i: (0, i, 0)
    bwd = lambda i: (0, nseg - 1 - i, 0)
    cst3 = lambda i: (0, 0, 0)
    return pl.pallas_call(
        functools.partial(_ssm_state_kernel, nc=nc, nb=nb, gblk=gblk),
        grid=(nseg,),
        in_specs=[
            pl.BlockSpec((G, seg_rows, CHUNK_LANES), fwd),
            pl.BlockSpec((G, seg_rows, CHUNK_LANES), bwd),
            pl.BlockSpec(bst_f.shape, cst3),
            pl.BlockSpec(bst_b.shape, cst3),
            pl.BlockSpec(coef.shape, lambda i: (0, 0, 0, 0, 0)),
        ],
        out_specs=[
            pl.BlockSpec((G, seg_rows, LANES), fwd),
            pl.BlockSpec((G, seg_rows, LANES), bwd),
        ],
        out_shape=[jax.ShapeDtypeStruct((G, rows, LANES), f32),
                   jax.ShapeDtypeStruct((G, rows, LANES), f32)],
        scratch_shapes=[
            pltpu.VMEM((2, gblk, seg_rows, CHUNK_LANES), f32),
            pltpu.VMEM((2, 2, G, nb, LANES), f32),
        ],
        compiler_params=_params("arbitrary"),
        name="ssm_state",
    )(u, u, bst_f, bst_b, coef)


def _ssm_out_kernel(u_ref, hf_ref, gb_ref, m_ref, cf_ref, cb_ref, y_ref, nat_s, *, nb, nchunk):
    tok = nchunk * CHUNK
    for tile in range(D_SSM // LANES):
        ys = []
        for g8 in range(8):
            g = 8 * tile + g8
            y = _dot(u_ref[g], m_ref[g])
            y = y + _dot(hf_ref[g].astype(bf16), cf_ref[g])
            y = y + _dot(gb_ref[g].astype(bf16), cb_ref[g])
            ys.append(y)
        for hf in range(CHUNK_LANES // LANES):
            out = _segment_transpose8([y[:, hf * LANES:(hf + 1) * LANES] for y in ys])
            for t8 in range(8):
                t = 8 * hf + t8
                for ch in range(nchunk):
                    nat_s[tile, pl.ds(t + CHUNK * ch, nb, stride=tok), :] = out[t8][ch * nb:(ch + 1) * nb]
        y_ref[:, :, tile * LANES:(tile + 1) * LANES] = nat_s[tile].reshape(nb, tok, LANES).astype(bf16)


def _ssm_out(u, hf, gb, m, cst_f, cst_b, *, nb):
    G, rows, _ = u.shape
    rb = min(SSM_OUT_ROWS, rows)
    nchunk = rb // nb
    tok = nchunk * CHUNK
    blk = lambda i: (0, i, 0)
    cst3 = lambda i: (0, 0, 0)
    return pl.pallas_call(
        functools.partial(_ssm_out_kernel, nb=nb, nchunk=nchunk),
        grid=(rows // rb,),
        in_specs=[
            pl.BlockSpec((G, rb, CHUNK_LANES), blk),
            pl.BlockSpec((G, rb, LANES), blk),
            pl.BlockSpec((G, rb, LANES), blk),
            pl.BlockSpec(m.shape, cst3),
            pl.BlockSpec(cst_f.shape, cst3),
            pl.BlockSpec(cst_b.shape, cst3),
        ],
        out_specs=pl.BlockSpec((nb, tok, D_SSM), blk),
        out_shape=jax.ShapeDtypeStruct((nb, (rows // nb) * CHUNK, D_SSM), bf16),
        scratch_shapes=[pltpu.VMEM((D_SSM // LANES, nb * tok, LANES), f32)],
        compiler_params=_params("parallel"),
        name="ssm_out",
    )(u, hf, gb, m, cst_f, cst_b)


def _ssm_tables(a_re, a_im, log_dt, b_re, b_im, c_re, c_im, d_skip):
    T, G, P, C = CHUNK, N_GROUPS, STATE, SSM_GROUP
    a_re, a_im, log_dt = a_re.astype(f32), a_im.astype(f32), log_dt.astype(f32)
    dt = jnp.exp(log_dt)[..., None]
    ks = jnp.arange(T + 1, dtype=f32)[:, None, None, None]
    mag = jnp.exp(ks * (a_re * dt))
    pw_re = mag * jnp.cos(ks * (a_im * dt))
    pw_im = mag * jnp.sin(ks * (a_im * dt))
    n_re, n_im = pw_re[1] - 1.0, pw_im[1]
    den = a_re * a_re + a_im * a_im
    co_re = (n_re * a_re + n_im * a_im) / den
    co_im = (n_im * a_re - n_re * a_im) / den
    b_re, b_im = b_re.astype(f32), b_im.astype(f32)
    bb_re = co_re[..., None] * b_re - co_im[..., None] * b_im
    bb_im = co_re[..., None] * b_im + co_im[..., None] * b_re
    c_re, c_im = c_re.astype(f32), c_im.astype(f32)
    cp_re = c_re[None] * pw_re[:, :, :, None, :] - c_im[None] * pw_im[:, :, :, None, :]
    cp_im = c_re[None] * pw_im[:, :, :, None, :] + c_im[None] * pw_re[:, :, :, None, :]
    kmat = (jnp.einsum('kngcp,ngpd->kngcd', cp_re[:T], bb_re)
            - jnp.einsum('kngcp,ngpd->kngcd', cp_im[:T], bb_im))
    j = jnp.arange(T)[:, None]
    t = jnp.arange(T)[None, :]
    lag = t - j
    kf = jnp.where((lag >= 0)[:, :, None, None, None], kmat[jnp.clip(lag, 0, T - 1), 0], 0.0)
    kb = jnp.where((lag <= 0)[:, :, None, None, None], kmat[jnp.clip(-lag, 0, T - 1), 1], 0.0)
    m5 = (kf + kb).transpose(2, 0, 4, 1, 3)
    eye = (jnp.eye(T, dtype=f32)[:, None, :, None] * jnp.eye(C, dtype=f32)[None, :, None, :])
    m5 = m5 + eye[None] * d_skip.astype(f32).reshape(G, 1, 1, 1, C)
    m = m5.reshape(G, T * C, T * C)

    def bst(n, powers):
        pr, pi = pw_re[powers, n], pw_im[powers, n]
        re = pr[..., None] * bb_re[n][None] - pi[..., None] * bb_im[n][None]
        im = pr[..., None] * bb_im[n][None] + pi[..., None] * bb_re[n][None]
        re = re.transpose(1, 0, 3, 2).reshape(G, T * C, P)
        im = im.transpose(1, 0, 3, 2).reshape(G, T * C, P)
        return jnp.concatenate([re, im, im, re], axis=-1)

    bst_f = bst(0, T - 1 - jnp.arange(T))
    bst_b = bst(1, jnp.arange(T))

    def cst(n, powers):
        re = cp_re[powers, n]
        im = cp_im[powers, n]
        re = re.transpose(1, 3, 0, 2).reshape(G, P, T * C)
        im = im.transpose(1, 3, 0, 2).reshape(G, P, T * C)
        return jnp.concatenate([re, -im], axis=1)

    cst_f = cst(0, jnp.arange(T) + 1)
    cst_b = cst(1, T - jnp.arange(T))
    ar, ai = pw_re[T], pw_im[T]
    coef = jnp.stack([jnp.concatenate([ar, ar], -1),
                      jnp.concatenate([-ai, ai], -1),
                      jnp.concatenate([ai, -ai], -1)], axis=2)
    return m.astype(bf16), bst_f.astype(bf16), bst_b.astype(bf16), cst_f.astype(bf16), cst_b.astype(bf16), coef


def _attn_kernel(lam_ref, qt_ref, k_ref, vt_ref, g_ref, za_ref, o_ref,
                 qbd_ref, acc_ref, p_ref, m_ref, a_ref, *, nkv, bq, bk):
    lam = lam_ref[0]
    half = HEAD_DIM
    ns = bq // LANES
    dv = 2 * HEAD_DIM
    zero = jnp.zeros((half, LANES), bf16)
    for st in range(ns):
        qs = qt_ref[0, :, st * LANES:(st + 1) * LANES]
        qbd_ref[st, 0:half, 0:LANES] = qs[0:half]
        qbd_ref[st, 0:half, LANES:2 * LANES] = zero
        qbd_ref[st, half:2 * half, 0:LANES] = zero
        qbd_ref[st, half:2 * half, LANES:2 * LANES] = qs[half:2 * half]
    acc_ref[...] = jnp.zeros_like(acc_ref)
    p_ref[...] = jnp.zeros_like(p_ref)
    a_ref[...] = jnp.ones_like(a_ref)
    m_ref[...] = jnp.full(m_ref.shape, -jnp.inf, f32)

    def values(j):
        return jnp.concatenate([vt_ref[0, 2 * j], vt_ref[0, 2 * j + 1]], axis=1)

    nsub = F32_ROWS

    def fold(st, vb):
        acc = acc_ref[st].reshape(V_ROWS // nsub, nsub, 2 * LANES) * a_ref[st][None]
        acc_ref[st] = acc.reshape(V_ROWS, 2 * LANES) + _dot(vb, p_ref[st])

    def colmax(s):
        mx = jnp.max(s.reshape(-1, nsub, 2 * LANES), axis=0)
        for sh in (4, 2, 1):
            mx = jnp.maximum(mx, pltpu.roll(mx, sh, 0))
        return mx

    def probs(s, m):
        x = s.reshape(-1, nsub, 2 * LANES) - m[None]
        return jnp.exp2(x.reshape(s.shape).astype(bf16))

    def body(j, carry):
        off = pl.multiple_of(j * bk, bk)
        ka = k_ref[0, pl.ds(off, bk // 2), :]
        kb = k_ref[0, pl.ds(off + bk // 2, bk // 2), :]
        vb = values(jnp.maximum(j - 1, 0))
        for st in range(ns):
            fold(st, vb)
            m_old = m_ref[st]
            s_a = _dot(ka, qbd_ref[st])
            m_a = jnp.maximum(m_old, colmax(s_a))
            p_a = probs(s_a, m_a)
            s_b = _dot(kb, qbd_ref[st])
            m_b = jnp.maximum(m_a, colmax(s_b))
            p_b = probs(s_b, m_b)
            corr = jnp.exp2(m_a - m_b)
            corr = jnp.concatenate([corr, corr], axis=0).astype(bf16)
            p_a = p_a.reshape(-1, BF16_ROWS, 2 * LANES) * corr[None]
            p_ref[st, 0:bk // 2] = p_a.reshape(bk // 2, 2 * LANES)
            p_ref[st, bk // 2:bk] = p_b
            a_ref[st] = jnp.exp2(m_old - m_b)
            m_ref[st] = m_b
        return carry

    lax.fori_loop(0, nkv, body, 0, unroll=4)

    vb_last = values(nkv - 1)
    for st in range(ns):
        fold(st, vb_last)
        acc = acc_ref[st, 0:dv]
        inv = 1.0 / acc_ref[st, dv:dv + 1]
        ot = acc[:, :LANES] * inv[:, :LANES] - lam * (acc[:, LANES:] * inv[:, LANES:])
        o = ot.T
        y = o * lax.rsqrt(jnp.mean(o * o, axis=-1, keepdims=True) + SUBLN_EPS) * g_ref[...] * (1.0 - LAM_INIT)
        za = za_ref[0, st * LANES:(st + 1) * LANES, :].astype(f32)
        o_ref[0, st * LANES:(st + 1) * LANES, :] = (y * (za * jax.nn.sigmoid(za))).astype(bf16)


def _attention(lam, qt, k, vt, subln_g, za, *, bq, bk):
    B, L, _ = k.shape
    nkv = L // bk
    nvt = vt.shape[1]
    return pl.pallas_call(
        functools.partial(_attn_kernel, nkv=nkv, bq=bq, bk=bk),
        grid=(B, N_HEADS, L // bq),
        in_specs=[
            pl.BlockSpec(memory_space=pltpu.SMEM),
            pl.BlockSpec((1, 2 * HEAD_DIM, bq), lambda b, h, i: (b, h, i)),
            pl.BlockSpec((1, L, 2 * HEAD_DIM), lambda b, h, i: (b, 0, h)),
            pl.BlockSpec((1, nvt, V_ROWS, bk // 2), lambda b, h, i: (b, 0, h, 0)),
            pl.BlockSpec((1, 2 * HEAD_DIM), lambda b, h, i: (0, 0)),
            pl.BlockSpec((1, bq, 2 * HEAD_DIM), lambda b, h, i: (b, i, h)),
        ],
        out_specs=pl.BlockSpec((1, bq, 2 * HEAD_DIM), lambda b, h, i: (b, i, h)),
        out_shape=jax.ShapeDtypeStruct((B, L, D_ATTN), bf16),
        scratch_shapes=[
            pltpu.VMEM((bq // LANES, 2 * HEAD_DIM, 2 * LANES), bf16),
            pltpu.VMEM((bq // LANES, V_ROWS, 2 * LANES), f32),
            pltpu.VMEM((bq // LANES, bk, 2 * LANES), bf16),
            pltpu.VMEM((bq // LANES, F32_ROWS, 2 * LANES), f32),
            pltpu.VMEM((bq // LANES, F32_ROWS, 2 * LANES), f32),
        ],
        compiler_params=_params("parallel", "parallel", "arbitrary"),
        name="attention",
    )(lam, qt, k, vt, subln_g, za)


def _out_proj_kernel(x_ref, yssm_ref, zs_ref, ya_ref, g_ref, wg_ref, wglu_ref, bglu_ref,
                     wb_ref, wout_ref, fg_ref, o_ref):
    x = x_ref[0]
    r = lax.rsqrt(jnp.mean(x * x, axis=-1, keepdims=True) + NORM_EPS)
    h = (x * r * g_ref[...]).astype(bf16)

    ys = jax.nn.gelu(yssm_ref[0].astype(f32))
    ys = ys * jax.nn.sigmoid(_dot(ys.astype(bf16), wglu_ref[...]) + bglu_ref[...])
    zs = zs_ref[0].astype(f32)
    ys = ys * (zs * jax.nn.sigmoid(zs))

    ps = _dot(ys.astype(bf16), wb_ref[0])
    merged = jax.nn.sigmoid(_dot(h, wg_ref[:, 0:D_MODEL])) * ps
    pa = _dot(ya_ref[0], wb_ref[1])
    merged = merged + jax.nn.sigmoid(_dot(h, wg_ref[:, D_MODEL:2 * D_MODEL])) * pa
    out = x + _dot(merged.astype(bf16), wout_ref[...])
    o_ref[0] = out * lax.rsqrt(jnp.mean(out * out, axis=-1, keepdims=True) + NORM_EPS) * fg_ref[...]


def _out_proj(x, yssm, zs, ya, norm_g, wg, wglu, bglu, wb, wout, final_g, *, tm):
    B, L, _ = x.shape
    tok = lambda b, i: (b, i, 0)
    cst = lambda b, i: (0, 0)
    return pl.pallas_call(
        _out_proj_kernel,
        grid=(B, L // tm),
        in_specs=[
            pl.BlockSpec((1, tm, D_MODEL), tok),
            pl.BlockSpec((1, tm, 512), tok),
            pl.BlockSpec((1, tm, 512), tok),
            pl.BlockSpec((1, tm, 512), tok),
            pl.BlockSpec((1, D_MODEL), cst),
            pl.BlockSpec(wg.shape, cst),
            pl.BlockSpec(wglu.shape, cst),
            pl.BlockSpec((1, D_SSM), cst),
            pl.BlockSpec(wb.shape, lambda b, i: (0, 0, 0)),
            pl.BlockSpec(wout.shape, cst),
            pl.BlockSpec((1, D_MODEL), cst),
        ],
        out_specs=pl.BlockSpec((1, tm, D_MODEL), tok),
        out_shape=jax.ShapeDtypeStruct((B, L, D_MODEL), x.dtype),
        compiler_params=_params("parallel", "parallel"),
        name="out_proj",
    )(x, yssm, zs, ya, norm_g, wg, wglu, bglu, wb, wout, final_g)


def _rotary_tables(L):
    half = HEAD_DIM // 2
    inv_freq = 1.0 / (ROPE_THETA ** (jnp.arange(0, half, dtype=f32) * 2.0 / HEAD_DIM))
    ang = jnp.arange(L, dtype=f32)[:, None] * inv_freq[None, :]
    cos, sin = jnp.cos(ang), jnp.sin(ang)
    ck = jnp.tile(cos, (1, LANES // half))
    sk = jnp.tile(sin, (1, LANES // half))
    scale = math.log2(math.e) / math.sqrt(HEAD_DIM)
    return ck, sk, cos.T * scale, sin.T * scale


def _trunk(x, w):
    B, L, _ = x.shape
    assert B == 8 and L % ATTN_BQ == 0, "scan state vregs hold one row per batch element"
    ck, sk, cq, sq = _rotary_tables(L)
    u, zs, k, za, qt, vt = _in_proj(x, w["norm_g"], w["wa"], w["wbt"], ck, sk, cq, sq, tb=PROJ_TOKENS)
    hf, gb = _ssm_state(u, w["bst_f"], w["bst_b"], w["coef"], nb=B)
    yssm = _ssm_out(u, hf, gb, w["m"], w["cst_f"], w["cst_b"], nb=B)
    ya = _attention(w["lam"], qt, k, vt, w["subln_g"], za, bq=ATTN_BQ, bk=ATTN_BK)
    return _out_proj(x, yssm, zs, ya, w["norm_g"], w["wg"], w["wglu"], w["bglu"], w["wb"], w["wout"],
                     w["final_g"], tm=TOKEN_BLOCK)


def _rotate_half_columns(wk):
    d = wk.shape[0]
    w4 = wk.reshape(d, D_ATTN // HEAD_DIM, 2, HEAD_DIM // 2)
    return jnp.stack([-w4[:, :, 1], w4[:, :, 0]], axis=2).reshape(d, D_ATTN)


def kernel(x_prompt, x_sample, norm_g, w_in, ssm_a_re, ssm_a_im, ssm_log_dt, ssm_b_re, ssm_b_im, ssm_c_re, ssm_c_im, ssm_d, w_glu, b_glu, lambda_q1, lambda_k1, lambda_q2, lambda_k2, subln_g, w_branch, w_out, final_g):
    li = 0
    wi = w_in[li].astype(f32)
    w_xs, w_zs = wi[:, 0:512], wi[:, 512:1024]
    w_q, w_k, w_v, w_za = wi[:, 1024:1536], wi[:, 1536:2048], wi[:, 2048:2560], wi[:, 2560:3072]
    m, bst_f, bst_b, cst_f, cst_b, coef = _ssm_tables(
        ssm_a_re[li], ssm_a_im[li], ssm_log_dt[li], ssm_b_re[li], ssm_b_im[li],
        ssm_c_re[li], ssm_c_im[li], ssm_d[li])
    nb = x_prompt.shape[0]
    lam = (jnp.exp(jnp.sum(lambda_q1[li].astype(f32) * lambda_k1[li].astype(f32)))
           - jnp.exp(jnp.sum(lambda_q2[li].astype(f32) * lambda_k2[li].astype(f32))) + LAM_INIT)
    w = dict(
        norm_g=norm_g[li].astype(f32).reshape(1, D_MODEL),
        wa=jnp.concatenate([w_xs, w_zs, w_k, _rotate_half_columns(w_k), w_za], axis=1).astype(bf16),
        wbt=jnp.concatenate([w_q, w_v], axis=1).T.astype(bf16),
        wg=wi[:, 3072:5120].astype(bf16),
        m=m, bst_f=bst_f, bst_b=bst_b, cst_f=cst_f, cst_b=cst_b,
        coef=jnp.broadcast_to(coef[:, :, :, None, :], (2, N_GROUPS, 3, nb, LANES)),
        lam=lam.reshape(1).astype(f32),
        subln_g=subln_g[li].astype(f32).reshape(1, 2 * HEAD_DIM),
        wglu=w_glu[li].astype(bf16),
        bglu=b_glu[li].astype(f32).reshape(1, D_SSM),
        wb=w_branch[li].astype(bf16),
        wout=w_out[li].astype(bf16),
        final_g=final_g.astype(f32).reshape(1, D_MODEL),
    )
    return (_trunk(x_prompt, w), _trunk(x_sample, w))
```

```python
import functools
import math

import jax
import jax.numpy as jnp
from jax import lax
from jax.experimental import pallas as pl
from jax.experimental.pallas import tpu as pltpu

D_MODEL = 1024
D_SSM = 512
SSM_GROUP = 16
N_GROUPS = 32
STATE = 64
D_ATTN = 512
N_HEADS = 4
HEAD_DIM = 64
ROPE_THETA = 10000.0
NORM_EPS = 1e-6
SUBLN_EPS = 1e-5
LAM_INIT = 0.8 - 0.6 * math.exp(-0.3 * 0)

CHUNK = 16
CHUNK_LANES = CHUNK * SSM_GROUP
LANES = 128
VMEM_LIMIT = 56 * 1024 * 1024

TOKEN_BLOCK = 512
ATTN_BQ = 2048
ATTN_UNROLL = 16
ATTN_BK = 256
PROJ_TOKENS = ATTN_BK // 2
F32_ROWS = 8
BF16_ROWS = 16
V_ROWS = 2 * HEAD_DIM + BF16_ROWS
SSM_SEG_CHUNKS = 16
SSM_GROUP_BLOCK = 4
SSM_OUT_ROWS = 128

f32 = jnp.float32
bf16 = jnp.bfloat16


def _params(*sem):
    return pltpu.CompilerParams(dimension_semantics=sem, vmem_limit_bytes=VMEM_LIMIT)


def _dot(a, b):
    return jnp.dot(a, b, preferred_element_type=f32)


def _dot_nt(a, b):
    return lax.dot_general(a, b, (((1,), (1,)), ((), ())), preferred_element_type=f32)


def _segment_transpose8(vs):
    slot = lax.broadcasted_iota(jnp.int32, vs[0].shape, 1) // SSM_GROUP
    for d in (4, 2, 1):
        keep = (slot & d) == 0
        new = list(vs)
        for i in range(8):
            if i & d == 0:
                a, b = vs[i], vs[i + d]
                new[i] = jnp.where(keep, a, pltpu.roll(b, d * SSM_GROUP, 1))
                new[i + d] = jnp.where(keep, pltpu.roll(a, LANES - d * SSM_GROUP, 1), b)
        vs = new
    return vs


def _in_proj_kernel(x_ref, g_ref, wa_ref, wbt_ref, ck_ref, sk_ref, cq_ref, sq_ref,
                    u_ref, zs_ref, k_ref, za_ref, qt_ref, vt_ref, xs_s, *, nb, tb):
    rows = nb * tb
    x = x_ref[...].reshape(rows, D_MODEL)
    r = lax.rsqrt(jnp.mean(x * x, axis=-1, keepdims=True) + NORM_EPS)
    h = (x * r * g_ref[...]).astype(bf16)

    for tile in range(D_SSM // LANES):
        xs_s[tile] = _dot(h, wa_ref[:, tile * LANES:(tile + 1) * LANES])
    zs_ref[...] = _dot(h, wa_ref[:, 512:1024]).astype(bf16).reshape(nb, tb, D_SSM)
    kk = _dot(h, wa_ref[:, 1024:1536]).reshape(nb, tb, D_ATTN)
    kr = _dot(h, wa_ref[:, 1536:2048]).reshape(nb, tb, D_ATTN)
    ck = ck_ref[...]
    sk = sk_ref[...]
    for j in range(D_ATTN // LANES):
        sl = slice(j * LANES, (j + 1) * LANES)
        k_ref[:, :, sl] = (kk[:, :, sl] * ck + kr[:, :, sl] * sk).astype(bf16)
    za_ref[...] = _dot(h, wa_ref[:, 2048:2560]).astype(bf16).reshape(nb, tb, D_ATTN)

    pq = _dot_nt(wbt_ref[0:512, :], h)
    cq = jnp.tile(cq_ref[...], (1, nb))
    sq = jnp.tile(sq_ref[...], (1, nb))
    half = HEAD_DIM // 2
    for hb in range(D_ATTN // HEAD_DIM):
        x1 = pq[hb * HEAD_DIM:hb * HEAD_DIM + half]
        x2 = pq[hb * HEAD_DIM + half:(hb + 1) * HEAD_DIM]
        q1 = (x1 * cq - x2 * sq).astype(bf16)
        q2 = (x2 * cq + x1 * sq).astype(bf16)
        for b in range(nb):
            qt_ref[b, hb * HEAD_DIM:hb * HEAD_DIM + half, :] = q1[:, b * tb:(b + 1) * tb]
            qt_ref[b, hb * HEAD_DIM + half:(hb + 1) * HEAD_DIM, :] = q2[:, b * tb:(b + 1) * tb]
    pv = _dot_nt(wbt_ref[512:1024, :], h).astype(bf16)
    dv = 2 * HEAD_DIM
    ones = jnp.ones((V_ROWS - dv, tb), bf16)
    for b in range(nb):
        for hd in range(N_HEADS):
            vt_ref[b, 0, hd * V_ROWS:hd * V_ROWS + dv, :] = pv[hd * dv:(hd + 1) * dv, b * tb:(b + 1) * tb]
            vt_ref[b, 0, hd * V_ROWS + dv:(hd + 1) * V_ROWS, :] = ones

    nchunk = tb // CHUNK
    for tile in range(D_SSM // LANES):
        for hf in range(CHUNK_LANES // LANES):
            vs = []
            for t8 in range(8):
                t = 8 * hf + t8
                vs.append(jnp.concatenate(
                    [xs_s[tile, pl.ds(t + CHUNK * ch, nb, stride=tb), :] for ch in range(nchunk)],
                    axis=0))
            out = _segment_transpose8(vs)
            for g8 in range(8):
                u_ref[8 * tile + g8, :, hf * LANES:(hf + 1) * LANES] = out[g8].astype(bf16)


def _in_proj(x, norm_g, wa, wbt, ck, sk, cq, sq, *, tb):
    B, L, _ = x.shape
    nchunk = tb // CHUNK
    tok = lambda i: (0, i, 0)
    cst = lambda i: (0, 0)
    out_tok = jax.ShapeDtypeStruct((B, L, 512), bf16)
    return pl.pallas_call(
        functools.partial(_in_proj_kernel, nb=B, tb=tb),
        grid=(L // tb,),
        in_specs=[
            pl.BlockSpec((B, tb, D_MODEL), tok),
            pl.BlockSpec((1, D_MODEL), cst),
            pl.BlockSpec(wa.shape, cst),
            pl.BlockSpec(wbt.shape, cst),
            pl.BlockSpec((tb, LANES), lambda i: (i, 0)),
            pl.BlockSpec((tb, LANES), lambda i: (i, 0)),
            pl.BlockSpec((HEAD_DIM // 2, tb), lambda i: (0, i)),
            pl.BlockSpec((HEAD_DIM // 2, tb), lambda i: (0, i)),
        ],
        out_specs=[
            pl.BlockSpec((N_GROUPS, nchunk * B, CHUNK_LANES), tok),
            pl.BlockSpec((B, tb, 512), tok),
            pl.BlockSpec((B, tb, 512), tok),
            pl.BlockSpec((B, tb, 512), tok),
            pl.BlockSpec((B, 512, tb), lambda i: (0, 0, i)),
            pl.BlockSpec((B, 1, N_HEADS * V_ROWS, tb), lambda i: (0, i, 0, 0)),
        ],
        out_shape=[jax.ShapeDtypeStruct((N_GROUPS, (L // CHUNK) * B, CHUNK_LANES), bf16),
                   out_tok, out_tok, out_tok,
                   jax.ShapeDtypeStruct((B, 512, L), bf16),
                   jax.ShapeDtypeStruct((B, L // tb, N_HEADS * V_ROWS, tb), bf16)],
        scratch_shapes=[pltpu.VMEM((D_SSM // LANES, B * tb, LANES), f32)],
        compiler_params=_params("parallel"),
        name="in_proj",
    )(x, norm_g, wa, wbt, ck, sk, cq, sq)


def _ssm_state_kernel(uf_ref, ub_ref, bf_ref, bb_ref, coef_ref, hf_ref, gb_ref, s_ref, st_ref,
                      *, nc, nb, gblk):
    @pl.when(pl.program_id(0) == 0)
    def _():
        st_ref[...] = jnp.zeros_like(st_ref)

    for g0 in range(0, N_GROUPS, gblk):
        for gi in range(gblk):
            s_ref[0, gi] = _dot(uf_ref[g0 + gi], bf_ref[g0 + gi])
            s_ref[1, gi] = _dot(ub_ref[g0 + gi], bb_ref[g0 + gi])
        gs = slice(g0, g0 + gblk)
        af1, af2, af3 = coef_ref[0, gs, 0], coef_ref[0, gs, 1], coef_ref[0, gs, 2]
        ab1, ab2, ab3 = coef_ref[1, gs, 0], coef_ref[1, gs, 1], coef_ref[1, gs, 2]

        def body(i, carry):
            hf, wf, hb, wb = carry
            rf = pl.multiple_of(i * nb, nb)
            rb = pl.multiple_of((nc - 1 - i) * nb, nb)
            hf_ref[gs, pl.ds(rf, nb), :] = hf
            gb_ref[gs, pl.ds(rb, nb), :] = hb
            sf = s_ref[0, :, pl.ds(rf, nb), :]
            sb = s_ref[1, :, pl.ds(rb, nb), :]
            hf2 = af1 * hf + af2 * wf + sf[..., :LANES]
            wf2 = af1 * wf + af3 * hf + sf[..., LANES:]
            hb2 = ab1 * hb + ab2 * wb + sb[..., :LANES]
            wb2 = ab1 * wb + ab3 * hb + sb[..., LANES:]
            return hf2, wf2, hb2, wb2

        init = (st_ref[0, 0, gs], st_ref[0, 1, gs], st_ref[1, 0, gs], st_ref[1, 1, gs])
        hf, wf, hb, wb = lax.fori_loop(0, nc, body, init)
        st_ref[0, 0, gs] = hf
        st_ref[0, 1, gs] = wf
        st_ref[1, 0, gs] = hb
        st_ref[1, 1, gs] = wb


def _ssm_state(u, bst_f, bst_b, coef, *, nb):
    G, rows, _ = u.shape
    nc = SSM_SEG_CHUNKS
    seg_rows = nc * nb
    nseg = rows // seg_rows
    gblk = SSM_GROUP_BLOCK
    fwd = lambda i: (0, i, 0)
    bwd = lambda i: (0, nseg - 1 - i, 0)
    cst3 = lambda i: (0, 0, 0)
    return pl.pallas_call(
        functools.partial(_ssm_state_kernel, nc=nc, nb=nb, gblk=gblk),
        grid=(nseg,),
        in_specs=[
            pl.BlockSpec((G, seg_rows, CHUNK_LANES), fwd),
            pl.BlockSpec((G, seg_rows, CHUNK_LANES), bwd),
            pl.BlockSpec(bst_f.shape, cst3),
            pl.BlockSpec(bst_b.shape, cst3),
            pl.BlockSpec(coef.shape, lambda i: (0, 0, 0, 0, 0)),
        ],
        out_specs=[
            pl.BlockSpec((G, seg_rows, LANES), fwd),
            pl.BlockSpec((G, seg_rows, LANES), bwd),
        ],
        out_shape=[jax.ShapeDtypeStruct((G, rows, LANES), f32),
                   jax.ShapeDtypeStruct((G, rows, LANES), f32)],
        scratch_shapes=[
            pltpu.VMEM((2, gblk, seg_rows, CHUNK_LANES), f32),
            pltpu.VMEM((2, 2, G, nb, LANES), f32),
        ],
        compiler_params=_params("arbitrary"),
        name="ssm_state",
    )(u, u, bst_f, bst_b, coef)


def _ssm_out_kernel(u_ref, hf_ref, gb_ref, m_ref, cf_ref, cb_ref, y_ref, nat_s, *, nb, nchunk):
    tok = nchunk * CHUNK
    for tile in range(D_SSM // LANES):
        ys = []
        for g8 in range(8):
            g = 8 * tile + g8
            y = _dot(u_ref[g], m_ref[g])
            y = y + _dot(hf_ref[g].astype(bf16), cf_ref[g])
            y = y + _dot(gb_ref[g].astype(bf16), cb_ref[g])
            ys.append(y)
        for hf in range(CHUNK_LANES // LANES):
            out = _segment_transpose8([y[:, hf * LANES:(hf + 1) * LANES] for y in ys])
            for t8 in range(8):
                t = 8 * hf + t8
                for ch in range(nchunk):
                    nat_s[tile, pl.ds(t + CHUNK * ch, nb, stride=tok), :] = out[t8][ch * nb:(ch + 1) * nb]
        y_ref[:, :, tile * LANES:(tile + 1) * LANES] = nat_s[tile].reshape(nb, tok, LANES).astype(bf16)


def _ssm_out(u, hf, gb, m, cst_f, cst_b, *, nb):
    G, rows, _ = u.shape
    rb = min(SSM_OUT_ROWS, rows)
    nchunk = rb // nb
    tok = nchunk * CHUNK
    blk = lambda i: (0, i, 0)
    cst3 = lambda i: (0, 0, 0)
    return pl.pallas_call(
        functools.partial(_ssm_out_kernel, nb=nb, nchunk=nchunk),
        grid=(rows // rb,),
        in_specs=[
            pl.BlockSpec((G, rb, CHUNK_LANES), blk),
            pl.BlockSpec((G, rb, LANES), blk),
            pl.BlockSpec((G, rb, LANES), blk),
            pl.BlockSpec(m.shape, cst3),
            pl.BlockSpec(cst_f.shape, cst3),
            pl.BlockSpec(cst_b.shape, cst3),
        ],
        out_specs=pl.BlockSpec((nb, tok, D_SSM), blk),
        out_shape=jax.ShapeDtypeStruct((nb, (rows // nb) * CHUNK, D_SSM), bf16),
        scratch_shapes=[pltpu.VMEM((D_SSM // LANES, nb * tok, LANES), f32)],
        compiler_params=_params("parallel"),
        name="ssm_out",
    )(u, hf, gb, m, cst_f, cst_b)


def _ssm_tables(a_re, a_im, log_dt, b_re, b_im, c_re, c_im, d_skip):
    T, G, P, C = CHUNK, N_GROUPS, STATE, SSM_GROUP
    a_re, a_im, log_dt = a_re.astype(f32), a_im.astype(f32), log_dt.astype(f32)
    dt = jnp.exp(log_dt)[..., None]
    ks = jnp.arange(T + 1, dtype=f32)[:, None, None, None]
    mag = jnp.exp(ks * (a_re * dt))
    pw_re = mag * jnp.cos(ks * (a_im * dt))
    pw_im = mag * jnp.sin(ks * (a_im * dt))
    n_re, n_im = pw_re[1] - 1.0, pw_im[1]
    den = a_re * a_re + a_im * a_im
    co_re = (n_re * a_re + n_im * a_im) / den
    co_im = (n_im * a_re - n_re * a_im) / den
    b_re, b_im = b_re.astype(f32), b_im.astype(f32)
    bb_re = co_re[..., None] * b_re - co_im[..., None] * b_im
    bb_im = co_re[..., None] * b_im + co_im[..., None] * b_re
    c_re, c_im = c_re.astype(f32), c_im.astype(f32)
    cp_re = c_re[None] * pw_re[:, :, :, None, :] - c_im[None] * pw_im[:, :, :, None, :]
    cp_im = c_re[None] * pw_im[:, :, :, None, :] + c_im[None] * pw_re[:, :, :, None, :]
    kmat = (jnp.einsum('kngcp,ngpd->kngcd', cp_re[:T], bb_re)
            - jnp.einsum('kngcp,ngpd->kngcd', cp_im[:T], bb_im))
    j = jnp.arange(T)[:, None]
    t = jnp.arange(T)[None, :]
    lag = t - j
    kf = jnp.where((lag >= 0)[:, :, None, None, None], kmat[jnp.clip(lag, 0, T - 1), 0], 0.0)
    kb = jnp.where((lag <= 0)[:, :, None, None, None], kmat[jnp.clip(-lag, 0, T - 1), 1], 0.0)
    m5 = (kf + kb).transpose(2, 0, 4, 1, 3)
    eye = (jnp.eye(T, dtype=f32)[:, None, :, None] * jnp.eye(C, dtype=f32)[None, :, None, :])
    m5 = m5 + eye[None] * d_skip.astype(f32).reshape(G, 1, 1, 1, C)
    m = m5.reshape(G, T * C, T * C)

    def bst(n, powers):
        pr, pi = pw_re[powers, n], pw_im[powers, n]
        re = pr[..., None] * bb_re[n][None] - pi[..., None] * bb_im[n][None]
        im = pr[..., None] * bb_im[n][None] + pi[..., None] * bb_re[n][None]
        re = re.transpose(1, 0, 3, 2).reshape(G, T * C, P)
        im = im.transpose(1, 0, 3, 2).reshape(G, T * C, P)
        return jnp.concatenate([re, im, im, re], axis=-1)

    bst_f = bst(0, T - 1 - jnp.arange(T))
    bst_b = bst(1, jnp.arange(T))

    def cst(n, powers):
        re = cp_re[powers, n]
        im = cp_im[powers, n]
        re = re.transpose(1, 3, 0, 2).reshape(G, P, T * C)
        im = im.transpose(1, 3, 0, 2).reshape(G, P, T * C)
        return jnp.concatenate([re, -im], axis=1)

    cst_f = cst(0, jnp.arange(T) + 1)
    cst_b = cst(1, T - jnp.arange(T))
    ar, ai = pw_re[T], pw_im[T]
    coef = jnp.stack([jnp.concatenate([ar, ar], -1),
                      jnp.concatenate([-ai, ai], -1),
                      jnp.concatenate([ai, -ai], -1)], axis=2)
    return m.astype(bf16), bst_f.astype(bf16), bst_b.astype(bf16), cst_f.astype(bf16), cst_b.astype(bf16), coef


def _attn_kernel(lam_ref, qt_ref, k_ref, vt_ref, g_ref, za_ref, o_ref,
                 qbd_ref, acc_ref, p_ref, m_ref, a_ref, *, nkv, bq, bk):
    lam = lam_ref[0]
    half = HEAD_DIM
    ns = bq // LANES
    dv = 2 * HEAD_DIM
    zero = jnp.zeros((half, LANES), bf16)
    for st in range(ns):
        qs = qt_ref[0, :, st * LANES:(st + 1) * LANES]
        qbd_ref[st, 0:half, 0:LANES] = qs[0:half]
        qbd_ref[st, 0:half, LANES:2 * LANES] = zero
        qbd_ref[st, half:2 * half, 0:LANES] = zero
        qbd_ref[st, half:2 * half, LANES:2 * LANES] = qs[half:2 * half]
    acc_ref[...] = jnp.zeros_like(acc_ref)
    p_ref[...] = jnp.zeros_like(p_ref)
    a_ref[...] = jnp.ones_like(a_ref)
    m_ref[...] = jnp.full(m_ref.shape, -jnp.inf, f32)

    def values(j):
        return jnp.concatenate([vt_ref[0, 2 * j], vt_ref[0, 2 * j + 1]], axis=1)

    nsub = F32_ROWS

    def fold(st, vb):
        acc = acc_ref[st].reshape(V_ROWS // nsub, nsub, 2 * LANES) * a_ref[st][None]
        acc_ref[st] = acc.reshape(V_ROWS, 2 * LANES) + _dot(vb, p_ref[st])

    def colmax(s):
        mx = jnp.max(s.reshape(-1, nsub, 2 * LANES), axis=0)
        for sh in (4, 2, 1):
            mx = jnp.maximum(mx, pltpu.roll(mx, sh, 0))
        return mx

    def probs(s, m):
        x = s.reshape(-1, nsub, 2 * LANES) - m[None]
        return jnp.exp2(x.reshape(s.shape).astype(bf16))

    def body(j, carry):
        off = pl.multiple_of(j * bk, bk)
        ka = k_ref[0, pl.ds(off, bk // 2), :]
        kb = k_ref[0, pl.ds(off + bk // 2, bk // 2), :]
        vb = values(jnp.maximum(j - 1, 0))
        for st in range(ns):
            fold(st, vb)
            m_old = m_ref[st]
            s_a = _dot(ka, qbd_ref[st])
            m_a = jnp.maximum(m_old, colmax(s_a))
            p_a = probs(s_a, m_a)
            s_b = _dot(kb, qbd_ref[st])
            m_b = jnp.maximum(m_a, colmax(s_b))
            p_b = probs(s_b, m_b)
            corr = jnp.exp2(m_a - m_b)
            corr = jnp.concatenate([corr, corr], axis=0).astype(bf16)
            p_a = p_a.reshape(-1, BF16_ROWS, 2 * LANES) * corr[None]
            p_ref[st, 0:bk // 2] = p_a.reshape(bk // 2, 2 * LANES)
            p_ref[st, bk // 2:bk] = p_b
            a_ref[st] = jnp.exp2(m_old - m_b)
            m_ref[st] = m_b
        return carry

    lax.fori_loop(0, nkv, body, 0, unroll=min(ATTN_UNROLL, nkv))

    vb_last = values(nkv - 1)
    for st in range(ns):
        fold(st, vb_last)
        acc = acc_ref[st, 0:dv]
        inv = 1.0 / acc_ref[st, dv:dv + 1]
        ot = acc[:, :LANES] * inv[:, :LANES] - lam * (acc[:, LANES:] * inv[:, LANES:])
        o = ot.T
        y = o * lax.rsqrt(jnp.mean(o * o, axis=-1, keepdims=True) + SUBLN_EPS) * g_ref[...] * (1.0 - LAM_INIT)
        za = za_ref[0, st * LANES:(st + 1) * LANES, :].astype(f32)
        o_ref[0, st * LANES:(st + 1) * LANES, :] = (y * (za * jax.nn.sigmoid(za))).astype(bf16)


def _attention(lam, qt, k, vt, subln_g, za, *, bq, bk):
    B, L, _ = k.shape
    nkv = L // bk
    nvt = vt.shape[1]
    return pl.pallas_call(
        functools.partial(_attn_kernel, nkv=nkv, bq=bq, bk=bk),
        grid=(B, N_HEADS, L // bq),
        in_specs=[
            pl.BlockSpec(memory_space=pltpu.SMEM),
            pl.BlockSpec((1, 2 * HEAD_DIM, bq), lambda b, h, i: (b, h, i)),
            pl.BlockSpec((1, L, 2 * HEAD_DIM), lambda b, h, i: (b, 0, h)),
            pl.BlockSpec((1, nvt, V_ROWS, bk // 2), lambda b, h, i: (b, 0, h, 0)),
            pl.BlockSpec((1, 2 * HEAD_DIM), lambda b, h, i: (0, 0)),
            pl.BlockSpec((1, bq, 2 * HEAD_DIM), lambda b, h, i: (b, i, h)),
        ],
        out_specs=pl.BlockSpec((1, bq, 2 * HEAD_DIM), lambda b, h, i: (b, i, h)),
        out_shape=jax.ShapeDtypeStruct((B, L, D_ATTN), bf16),
        scratch_shapes=[
            pltpu.VMEM((bq // LANES, 2 * HEAD_DIM, 2 * LANES), bf16),
            pltpu.VMEM((bq // LANES, V_ROWS, 2 * LANES), f32),
            pltpu.VMEM((bq // LANES, bk, 2 * LANES), bf16),
            pltpu.VMEM((bq // LANES, F32_ROWS, 2 * LANES), f32),
            pltpu.VMEM((bq // LANES, F32_ROWS, 2 * LANES), f32),
        ],
        compiler_params=_params("parallel", "parallel", "arbitrary"),
        name="attention",
    )(lam, qt, k, vt, subln_g, za)


def _out_proj_kernel(x_ref, yssm_ref, zs_ref, ya_ref, g_ref, wg_ref, wglu_ref, bglu_ref,
                     wb_ref, wout_ref, fg_ref, o_ref):
    x = x_ref[0]
    r = lax.rsqrt(jnp.mean(x * x, axis=-1, keepdims=True) + NORM_EPS)
    h = (x * r * g_ref[...]).astype(bf16)

    ys = jax.nn.gelu(yssm_ref[0].astype(f32))
    ys = ys * jax.nn.sigmoid(_dot(ys.astype(bf16), wglu_ref[...]) + bglu_ref[...])
    zs = zs_ref[0].astype(f32)
    ys = ys * (zs * jax.nn.sigmoid(zs))

    ps = _dot(ys.astype(bf16), wb_ref[0])
    merged = jax.nn.sigmoid(_dot(h, wg_ref[:, 0:D_MODEL])) * ps
    pa = _dot(ya_ref[0], wb_ref[1])
    merged = merged + jax.nn.sigmoid(_dot(h, wg_ref[:, D_MODEL:2 * D_MODEL])) * pa
    out = x + _dot(merged.astype(bf16), wout_ref[...])
    o_ref[0] = out * lax.rsqrt(jnp.mean(out * out, axis=-1, keepdims=True) + NORM_EPS) * fg_ref[...]


def _out_proj(x, yssm, zs, ya, norm_g, wg, wglu, bglu, wb, wout, final_g, *, tm):
    B, L, _ = x.shape
    tok = lambda b, i: (b, i, 0)
    cst = lambda b, i: (0, 0)
    return pl.pallas_call(
        _out_proj_kernel,
        grid=(B, L // tm),
        in_specs=[
            pl.BlockSpec((1, tm, D_MODEL), tok),
            pl.BlockSpec((1, tm, 512), tok),
            pl.BlockSpec((1, tm, 512), tok),
            pl.BlockSpec((1, tm, 512), tok),
            pl.BlockSpec((1, D_MODEL), cst),
            pl.BlockSpec(wg.shape, cst),
            pl.BlockSpec(wglu.shape, cst),
            pl.BlockSpec((1, D_SSM), cst),
            pl.BlockSpec(wb.shape, lambda b, i: (0, 0, 0)),
            pl.BlockSpec(wout.shape, cst),
            pl.BlockSpec((1, D_MODEL), cst),
        ],
        out_specs=pl.BlockSpec((1, tm, D_MODEL), tok),
        out_shape=jax.ShapeDtypeStruct((B, L, D_MODEL), x.dtype),
        compiler_params=_params("parallel", "parallel"),
        name="out_proj",
    )(x, yssm, zs, ya, norm_g, wg, wglu, bglu, wb, wout, final_g)


def _rotary_tables(L):
    half = HEAD_DIM // 2
    inv_freq = 1.0 / (ROPE_THETA ** (jnp.arange(0, half, dtype=f32) * 2.0 / HEAD_DIM))
    ang = jnp.arange(L, dtype=f32)[:, None] * inv_freq[None, :]
    cos, sin = jnp.cos(ang), jnp.sin(ang)
    ck = jnp.tile(cos, (1, LANES // half))
    sk = jnp.tile(sin, (1, LANES // half))
    scale = math.log2(math.e) / math.sqrt(HEAD_DIM)
    return ck, sk, cos.T * scale, sin.T * scale


def _trunk(x, w):
    B, L, _ = x.shape
    assert B == 8 and L % ATTN_BQ == 0, "scan state vregs hold one row per batch element"
    ck, sk, cq, sq = _rotary_tables(L)
    u, zs, k, za, qt, vt = _in_proj(x, w["norm_g"], w["wa"], w["wbt"], ck, sk, cq, sq, tb=PROJ_TOKENS)
    hf, gb = _ssm_state(u, w["bst_f"], w["bst_b"], w["coef"], nb=B)
    yssm = _ssm_out(u, hf, gb, w["m"], w["cst_f"], w["cst_b"], nb=B)
    ya = _attention(w["lam"], qt, k, vt, w["subln_g"], za, bq=ATTN_BQ, bk=ATTN_BK)
    return _out_proj(x, yssm, zs, ya, w["norm_g"], w["wg"], w["wglu"], w["bglu"], w["wb"], w["wout"],
                     w["final_g"], tm=TOKEN_BLOCK)


def _rotate_half_columns(wk):
    d = wk.shape[0]
    w4 = wk.reshape(d, D_ATTN // HEAD_DIM, 2, HEAD_DIM // 2)
    return jnp.stack([-w4[:, :, 1], w4[:, :, 0]], axis=2).reshape(d, D_ATTN)


def kernel(x_prompt, x_sample, norm_g, w_in, ssm_a_re, ssm_a_im, ssm_log_dt, ssm_b_re, ssm_b_im, ssm_c_re, ssm_c_im, ssm_d, w_glu, b_glu, lambda_q1, lambda_k1, lambda_q2, lambda_k2, subln_g, w_branch, w_out, final_g):
    li = 0
    wi = w_in[li].astype(f32)
    w_xs, w_zs = wi[:, 0:512], wi[:, 512:1024]
    w_q, w_k, w_v, w_za = wi[:, 1024:1536], wi[:, 1536:2048], wi[:, 2048:2560], wi[:, 2560:3072]
    m, bst_f, bst_b, cst_f, cst_b, coef = _ssm_tables(
        ssm_a_re[li], ssm_a_im[li], ssm_log_dt[li], ssm_b_re[li], ssm_b_im[li],
        ssm_c_re[li], ssm_c_im[li], ssm_d[li])
    nb = x_prompt.shape[0]
    lam = (jnp.exp(jnp.sum(lambda_q1[li].astype(f32) * lambda_k1[li].astype(f32)))
           - jnp.exp(jnp.sum(lambda_q2[li].astype(f32) * lambda_k2[li].astype(f32))) + LAM_INIT)
    w = dict(
        norm_g=norm_g[li].astype(f32).reshape(1, D_MODEL),
        wa=jnp.concatenate([w_xs, w_zs, w_k, _rotate_half_columns(w_k), w_za], axis=1).astype(bf16),
        wbt=jnp.concatenate([w_q, w_v], axis=1).T.astype(bf16),
        wg=wi[:, 3072:5120].astype(bf16),
        m=m, bst_f=bst_f, bst_b=bst_b, cst_f=cst_f, cst_b=cst_b,
        coef=jnp.broadcast_to(coef[:, :, :, None, :], (2, N_GROUPS, 3, nb, LANES)),
        lam=lam.reshape(1).astype(f32),
        subln_g=subln_g[li].astype(f32).reshape(1, 2 * HEAD_DIM),
        wglu=w_glu[li].astype(bf16),
        bglu=b_glu[li].astype(f32).reshape(1, D_SSM),
        wb=w_branch[li].astype(bf16),
        wout=w_out[li].astype(bf16),
        final_g=final_g.astype(f32).reshape(1, D_MODEL),
    )
    return (_trunk(x_prompt, w), _trunk(x_sample, w))
```

```python
import functools
import math

import jax
import jax.numpy as jnp
from jax import lax
from jax.experimental import pallas as pl
from jax.experimental.pallas import tpu as pltpu

D_MODEL = 1024
D_SSM = 512
SSM_GROUP = 16
N_GROUPS = 32
STATE = 64
D_ATTN = 512
N_HEADS = 4
HEAD_DIM = 64
ROPE_THETA = 10000.0
NORM_EPS = 1e-6
SUBLN_EPS = 1e-5
LAM_INIT = 0.8 - 0.6 * math.exp(-0.3 * 0)

CHUNK = 16
CHUNK_LANES = CHUNK * SSM_GROUP
LANES = 128
VMEM_LIMIT = 56 * 1024 * 1024

TOKEN_BLOCK = 1024
OUT_PROJ_CHAINS = 4
ATTN_BQ = 2048
ATTN_UNROLL = 16
ATTN_BK = 256
PROJ_TOKENS = ATTN_BK // 2
F32_ROWS = 8
BF16_ROWS = 16
V_ROWS = 2 * HEAD_DIM + BF16_ROWS
SSM_SEG_CHUNKS = 16
SSM_GROUP_BLOCK = 4
SSM_OUT_ROWS = 128

f32 = jnp.float32
bf16 = jnp.bfloat16


def _params(*sem):
    return pltpu.CompilerParams(dimension_semantics=sem, vmem_limit_bytes=VMEM_LIMIT)


def _dot(a, b):
    return jnp.dot(a, b, preferred_element_type=f32)


def _dot_nt(a, b):
    return lax.dot_general(a, b, (((1,), (1,)), ((), ())), preferred_element_type=f32)


def _segment_transpose8(vs):
    slot = lax.broadcasted_iota(jnp.int32, vs[0].shape, 1) // SSM_GROUP
    for d in (4, 2, 1):
        keep = (slot & d) == 0
        new = list(vs)
        for i in range(8):
            if i & d == 0:
                a, b = vs[i], vs[i + d]
                new[i] = jnp.where(keep, a, pltpu.roll(b, d * SSM_GROUP, 1))
                new[i + d] = jnp.where(keep, pltpu.roll(a, LANES - d * SSM_GROUP, 1), b)
        vs = new
    return vs


def _in_proj_kernel(x_ref, g_ref, wa_ref, wbt_ref, ck_ref, sk_ref, cq_ref, sq_ref,
                    u_ref, zs_ref, k_ref, za_ref, qt_ref, vt_ref, xs_s, *, nb, tb):
    rows = nb * tb
    x = x_ref[...].reshape(rows, D_MODEL)
    r = lax.rsqrt(jnp.mean(x * x, axis=-1, keepdims=True) + NORM_EPS)
    h = (x * r * g_ref[...]).astype(bf16)

    for tile in range(D_SSM // LANES):
        xs_s[tile] = _dot(h, wa_ref[:, tile * LANES:(tile + 1) * LANES])
    zs_ref[...] = _dot(h, wa_ref[:, 512:1024]).astype(bf16).reshape(nb, tb, D_SSM)
    kk = _dot(h, wa_ref[:, 1024:1536]).reshape(nb, tb, D_ATTN)
    kr = _dot(h, wa_ref[:, 1536:2048]).reshape(nb, tb, D_ATTN)
    ck = ck_ref[...]
    sk = sk_ref[...]
    for j in range(D_ATTN // LANES):
        sl = slice(j * LANES, (j + 1) * LANES)
        k_ref[:, :, sl] = (kk[:, :, sl] * ck + kr[:, :, sl] * sk).astype(bf16)
    za_ref[...] = _dot(h, wa_ref[:, 2048:2560]).astype(bf16).reshape(nb, tb, D_ATTN)

    pq = _dot_nt(wbt_ref[0:512, :], h)
    cq = jnp.tile(cq_ref[...], (1, nb))
    sq = jnp.tile(sq_ref[...], (1, nb))
    half = HEAD_DIM // 2
    for hb in range(D_ATTN // HEAD_DIM):
        x1 = pq[hb * HEAD_DIM:hb * HEAD_DIM + half]
        x2 = pq[hb * HEAD_DIM + half:(hb + 1) * HEAD_DIM]
        q1 = (x1 * cq - x2 * sq).astype(bf16)
        q2 = (x2 * cq + x1 * sq).astype(bf16)
        for b in range(nb):
            qt_ref[b, hb * HEAD_DIM:hb * HEAD_DIM + half, :] = q1[:, b * tb:(b + 1) * tb]
            qt_ref[b, hb * HEAD_DIM + half:(hb + 1) * HEAD_DIM, :] = q2[:, b * tb:(b + 1) * tb]
    pv = _dot_nt(wbt_ref[512:1024, :], h).astype(bf16)
    dv = 2 * HEAD_DIM
    ones = jnp.ones((V_ROWS - dv, tb), bf16)
    for b in range(nb):
        for hd in range(N_HEADS):
            vt_ref[b, 0, hd * V_ROWS:hd * V_ROWS + dv, :] = pv[hd * dv:(hd + 1) * dv, b * tb:(b + 1) * tb]
            vt_ref[b, 0, hd * V_ROWS + dv:(hd + 1) * V_ROWS, :] = ones

    nchunk = tb // CHUNK
    for tile in range(D_SSM // LANES):
        for hf in range(CHUNK_LANES // LANES):
            vs = []
            for t8 in range(8):
                t = 8 * hf + t8
                vs.append(jnp.concatenate(
                    [xs_s[tile, pl.ds(t + CHUNK * ch, nb, stride=tb), :] for ch in range(nchunk)],
                    axis=0))
            out = _segment_transpose8(vs)
            for g8 in range(8):
                u_ref[8 * tile + g8, :, hf * LANES:(hf + 1) * LANES] = out[g8].astype(bf16)


def _in_proj(x, norm_g, wa, wbt, ck, sk, cq, sq, *, tb):
    B, L, _ = x.shape
    nchunk = tb // CHUNK
    tok = lambda i: (0, i, 0)
    cst = lambda i: (0, 0)
    out_tok = jax.ShapeDtypeStruct((B, L, 512), bf16)
    return pl.pallas_call(
        functools.partial(_in_proj_kernel, nb=B, tb=tb),
        grid=(L // tb,),
        in_specs=[
            pl.BlockSpec((B, tb, D_MODEL), tok),
            pl.BlockSpec((1, D_MODEL), cst),
            pl.BlockSpec(wa.shape, cst),
            pl.BlockSpec(wbt.shape, cst),
            pl.BlockSpec((tb, LANES), lambda i: (i, 0)),
            pl.BlockSpec((tb, LANES), lambda i: (i, 0)),
            pl.BlockSpec((HEAD_DIM // 2, tb), lambda i: (0, i)),
            pl.BlockSpec((HEAD_DIM // 2, tb), lambda i: (0, i)),
        ],
        out_specs=[
            pl.BlockSpec((N_GROUPS, nchunk * B, CHUNK_LANES), tok),
            pl.BlockSpec((B, tb, 512), tok),
            pl.BlockSpec((B, tb, 512), tok),
            pl.BlockSpec((B, tb, 512), tok),
            pl.BlockSpec((B, 512, tb), lambda i: (0, 0, i)),
            pl.BlockSpec((B, 1, N_HEADS * V_ROWS, tb), lambda i: (0, i, 0, 0)),
        ],
        out_shape=[jax.ShapeDtypeStruct((N_GROUPS, (L // CHUNK) * B, CHUNK_LANES), bf16),
                   out_tok, out_tok, out_tok,
                   jax.ShapeDtypeStruct((B, 512, L), bf16),
                   jax.ShapeDtypeStruct((B, L // tb, N_HEADS * V_ROWS, tb), bf16)],
        scratch_shapes=[pltpu.VMEM((D_SSM // LANES, B * tb, LANES), f32)],
        compiler_params=_params("parallel"),
        name="in_proj",
    )(x, norm_g, wa, wbt, ck, sk, cq, sq)


def _ssm_state_kernel(uf_ref, ub_ref, bf_ref, bb_ref, coef_ref, hf_ref, gb_ref, s_ref, st_ref,
                      *, nc, nb, gblk):
    @pl.when(pl.program_id(0) == 0)
    def _():
        st_ref[...] = jnp.zeros_like(st_ref)

    for g0 in range(0, N_GROUPS, gblk):
        for gi in range(gblk):
            s_ref[0, gi] = _dot(uf_ref[g0 + gi], bf_ref[g0 + gi])
            s_ref[1, gi] = _dot(ub_ref[g0 + gi], bb_ref[g0 + gi])
        gs = slice(g0, g0 + gblk)
        af1, af2, af3 = coef_ref[0, gs, 0], coef_ref[0, gs, 1], coef_ref[0, gs, 2]
        ab1, ab2, ab3 = coef_ref[1, gs, 0], coef_ref[1, gs, 1], coef_ref[1, gs, 2]

        def body(i, carry):
            hf, wf, hb, wb = carry
            rf = pl.multiple_of(i * nb, nb)
            rb = pl.multiple_of((nc - 1 - i) * nb, nb)
            hf_ref[gs, pl.ds(rf, nb), :] = hf
            gb_ref[gs, pl.ds(rb, nb), :] = hb
            sf = s_ref[0, :, pl.ds(rf, nb), :]
            sb = s_ref[1, :, pl.ds(rb, nb), :]
            hf2 = af1 * hf + af2 * wf + sf[..., :LANES]
            wf2 = af1 * wf + af3 * hf + sf[..., LANES:]
            hb2 = ab1 * hb + ab2 * wb + sb[..., :LANES]
            wb2 = ab1 * wb + ab3 * hb + sb[..., LANES:]
            return hf2, wf2, hb2, wb2

        init = (st_ref[0, 0, gs], st_ref[0, 1, gs], st_ref[1, 0, gs], st_ref[1, 1, gs])
        hf, wf, hb, wb = lax.fori_loop(0, nc, body, init)
        st_ref[0, 0, gs] = hf
        st_ref[0, 1, gs] = wf
        st_ref[1, 0, gs] = hb
        st_ref[1, 1, gs] = wb


def _ssm_state(u, bst_f, bst_b, coef, *, nb):
    G, rows, _ = u.shape
    nc = SSM_SEG_CHUNKS
    seg_rows = nc * nb
    nseg = rows // seg_rows
    gblk = SSM_GROUP_BLOCK
    fwd = lambda i: (0, i, 0)
    bwd = lambda i: (0, nseg - 1 - i, 0)
    cst3 = lambda i: (0, 0, 0)
    return pl.pallas_call(
        functools.partial(_ssm_state_kernel, nc=nc, nb=nb, gblk=gblk),
        grid=(nseg,),
        in_specs=[
            pl.BlockSpec((G, seg_rows, CHUNK_LANES), fwd),
            pl.BlockSpec((G, seg_rows, CHUNK_LANES), bwd),
            pl.BlockSpec(bst_f.shape, cst3),
            pl.BlockSpec(bst_b.shape, cst3),
            pl.BlockSpec(coef.shape, lambda i: (0, 0, 0, 0, 0)),
        ],
        out_specs=[
            pl.BlockSpec((G, seg_rows, LANES), fwd),
            pl.BlockSpec((G, seg_rows, LANES), bwd),
        ],
        out_shape=[jax.ShapeDtypeStruct((G, rows, LANES), f32),
                   jax.ShapeDtypeStruct((G, rows, LANES), f32)],
        scratch_shapes=[
            pltpu.VMEM((2, gblk, seg_rows, CHUNK_LANES), f32),
            pltpu.VMEM((2, 2, G, nb, LANES), f32),
        ],
        compiler_params=_params("arbitrary"),
        name="ssm_state",
    )(u, u, bst_f, bst_b, coef)


def _ssm_out_kernel(u_ref, hf_ref, gb_ref, m_ref, cf_ref, cb_ref, y_ref, nat_s, *, nb, nchunk):
    tok = nchunk * CHUNK
    for tile in range(D_SSM // LANES):
        ys = []
        for g8 in range(8):
            g = 8 * tile + g8
            y = _dot(u_ref[g], m_ref[g])
            y = y + _dot(hf_ref[g].astype(bf16), cf_ref[g])
            y = y + _dot(gb_ref[g].astype(bf16), cb_ref[g])
            ys.append(y)
        for hf in range(CHUNK_LANES // LANES):
            out = _segment_transpose8([y[:, hf * LANES:(hf + 1) * LANES] for y in ys])
            for t8 in range(8):
                t = 8 * hf + t8
                for ch in range(nchunk):
                    nat_s[tile, pl.ds(t + CHUNK * ch, nb, stride=tok), :] = out[t8][ch * nb:(ch + 1) * nb]
        y_ref[:, :, tile * LANES:(tile + 1) * LANES] = nat_s[tile].reshape(nb, tok, LANES).astype(bf16)


def _ssm_out(u, hf, gb, m, cst_f, cst_b, *, nb):
    G, rows, _ = u.shape
    rb = min(SSM_OUT_ROWS, rows)
    nchunk = rb // nb
    tok = nchunk * CHUNK
    blk = lambda i: (0, i, 0)
    cst3 = lambda i: (0, 0, 0)
    return pl.pallas_call(
        functools.partial(_ssm_out_kernel, nb=nb, nchunk=nchunk),
        grid=(rows // rb,),
        in_specs=[
            pl.BlockSpec((G, rb, CHUNK_LANES), blk),
            pl.BlockSpec((G, rb, LANES), blk),
            pl.BlockSpec((G, rb, LANES), blk),
            pl.BlockSpec(m.shape, cst3),
            pl.BlockSpec(cst_f.shape, cst3),
            pl.BlockSpec(cst_b.shape, cst3),
        ],
        out_specs=pl.BlockSpec((nb, tok, D_SSM), blk),
        out_shape=jax.ShapeDtypeStruct((nb, (rows // nb) * CHUNK, D_SSM), bf16),
        scratch_shapes=[pltpu.VMEM((D_SSM // LANES, nb * tok, LANES), f32)],
        compiler_params=_params("parallel"),
        name="ssm_out",
    )(u, hf, gb, m, cst_f, cst_b)


def _ssm_tables(a_re, a_im, log_dt, b_re, b_im, c_re, c_im, d_skip):
    T, G, P, C = CHUNK, N_GROUPS, STATE, SSM_GROUP
    hi = lax.Precision.HIGHEST
    a_re, a_im, log_dt = a_re.astype(f32), a_im.astype(f32), log_dt.astype(f32)
    dt = jnp.exp(log_dt)[..., None]
    ks = jnp.arange(T + 1, dtype=f32)[:, None, None, None]
    mag = jnp.exp(ks * (a_re * dt))
    pw_re = mag * jnp.cos(ks * (a_im * dt))
    pw_im = mag * jnp.sin(ks * (a_im * dt))
    inv_mag = jnp.exp(-T * (a_re * dt))
    ni_re = inv_mag * jnp.cos(T * (a_im * dt))
    ni_im = -inv_mag * jnp.sin(T * (a_im * dt))
    n_re, n_im = pw_re[1] - 1.0, pw_im[1]
    den = a_re * a_re + a_im * a_im
    co_re = (n_re * a_re + n_im * a_im) / den
    co_im = (n_im * a_re - n_re * a_im) / den
    b_re, b_im = b_re.astype(f32), b_im.astype(f32)
    bb_re = co_re[..., None] * b_re - co_im[..., None] * b_im
    bb_im = co_re[..., None] * b_im + co_im[..., None] * b_re
    c_re, c_im = c_re.astype(f32), c_im.astype(f32)

    def xtab(n, pr, pi):
        re = pr[..., None] * bb_re[n][None] - pi[..., None] * bb_im[n][None]
        im = pr[..., None] * bb_im[n][None] + pi[..., None] * bb_re[n][None]
        return (re.transpose(1, 0, 3, 2).reshape(G, T * C, P), im.transpose(1, 0, 3, 2).reshape(G, T * C, P))

    def ytab(n, pr, pi):
        re = c_re[n][None] * pr[:, :, None, :] - c_im[n][None] * pi[:, :, None, :]
        im = c_re[n][None] * pi[:, :, None, :] + c_im[n][None] * pr[:, :, None, :]
        return (re.transpose(1, 3, 0, 2).reshape(G, P, T * C), im.transpose(1, 3, 0, 2).reshape(G, P, T * C))

    def lag_kernels(n, x, y):
        xs_re = x[0] * ni_re[n][:, None, :] - x[1] * ni_im[n][:, None, :]
        xs_im = x[0] * ni_im[n][:, None, :] + x[1] * ni_re[n][:, None, :]
        return (jnp.einsum('gjp,gpt->gjt', xs_re, y[0], precision=hi)
                - jnp.einsum('gjp,gpt->gjt', xs_im, y[1], precision=hi))

    xf = xtab(0, pw_re[:T, 0][::-1], pw_im[:T, 0][::-1])
    xb = xtab(1, pw_re[:T, 1], pw_im[:T, 1])
    yf = ytab(0, pw_re[1:, 0], pw_im[1:, 0])
    yb = ytab(1, pw_re[1:, 1][::-1], pw_im[1:, 1][::-1])
    step = jnp.arange(T * C) // C
    causal = step[None, :] >= step[:, None]
    anti = step[None, :] <= step[:, None]
    d_diag = jnp.tile(d_skip.astype(f32).reshape(G, 1, C), (1, T, 1)).reshape(G, 1, T * C)
    m = (jnp.where(causal[None], lag_kernels(0, xf, yf), 0.0)
         + jnp.where(anti[None], lag_kernels(1, xb, yb), 0.0)
         + jnp.eye(T * C, dtype=f32)[None] * d_diag)

    bst_f = jnp.concatenate([xf[0], xf[1], xf[1], xf[0]], axis=-1)
    bst_b = jnp.concatenate([xb[0], xb[1], xb[1], xb[0]], axis=-1)
    cst_f = jnp.concatenate([yf[0], -yf[1]], axis=1)
    cst_b = jnp.concatenate([yb[0], -yb[1]], axis=1)
    ar, ai = pw_re[T], pw_im[T]
    coef = jnp.stack([jnp.concatenate([ar, ar], -1),
                      jnp.concatenate([-ai, ai], -1),
                      jnp.concatenate([ai, -ai], -1)], axis=2)
    return m.astype(bf16), bst_f.astype(bf16), bst_b.astype(bf16), cst_f.astype(bf16), cst_b.astype(bf16), coef


def _attn_kernel(lam_ref, qt_ref, k_ref, vt_ref, g_ref, za_ref, o_ref,
                 qbd_ref, acc_ref, p_ref, m_ref, a_ref, *, nkv, bq, bk):
    lam = lam_ref[0]
    half = HEAD_DIM
    ns = bq // LANES
    dv = 2 * HEAD_DIM
    zero = jnp.zeros((half, LANES), bf16)
    for st in range(ns):
        qs = qt_ref[0, :, st * LANES:(st + 1) * LANES]
        qbd_ref[st, 0:half, 0:LANES] = qs[0:half]
        qbd_ref[st, 0:half, LANES:2 * LANES] = zero
        qbd_ref[st, half:2 * half, 0:LANES] = zero
        qbd_ref[st, half:2 * half, LANES:2 * LANES] = qs[half:2 * half]
    acc_ref[...] = jnp.zeros_like(acc_ref)
    p_ref[...] = jnp.zeros_like(p_ref)
    a_ref[...] = jnp.ones_like(a_ref)
    m_ref[...] = jnp.full(m_ref.shape, -jnp.inf, f32)

    def values(j):
        return jnp.concatenate([vt_ref[0, 2 * j], vt_ref[0, 2 * j + 1]], axis=1)

    nsub = F32_ROWS

    def fold(st, vb):
        acc = acc_ref[st].reshape(V_ROWS // nsub, nsub, 2 * LANES) * a_ref[st][None]
        acc_ref[st] = acc.reshape(V_ROWS, 2 * LANES) + _dot(vb, p_ref[st])

    def colmax(s):
        mx = jnp.max(s.reshape(-1, nsub, 2 * LANES), axis=0)
        for sh in (4, 2, 1):
            mx = jnp.maximum(mx, pltpu.roll(mx, sh, 0))
        return mx

    def probs(s, m):
        x = s.reshape(-1, nsub, 2 * LANES) - m[None]
        return jnp.exp2(x.reshape(s.shape).astype(bf16))

    def body(j, carry):
        off = pl.multiple_of(j * bk, bk)
        ka = k_ref[0, pl.ds(off, bk // 2), :]
        kb = k_ref[0, pl.ds(off + bk // 2, bk // 2), :]
        vb = values(jnp.maximum(j - 1, 0))
        for st in range(ns):
            fold(st, vb)
            m_old = m_ref[st]
            s_a = _dot(ka, qbd_ref[st])
            m_a = jnp.maximum(m_old, colmax(s_a))
            p_a = probs(s_a, m_a)
            s_b = _dot(kb, qbd_ref[st])
            m_b = jnp.maximum(m_a, colmax(s_b))
            p_b = probs(s_b, m_b)
            corr = jnp.exp2(m_a - m_b)
            corr = jnp.concatenate([corr, corr], axis=0).astype(bf16)
            p_a = p_a.reshape(-1, BF16_ROWS, 2 * LANES) * corr[None]
            p_ref[st, 0:bk // 2] = p_a.reshape(bk // 2, 2 * LANES)
            p_ref[st, bk // 2:bk] = p_b
            a_ref[st] = jnp.exp2(m_old - m_b)
            m_ref[st] = m_b
        return carry

    lax.fori_loop(0, nkv, body, 0, unroll=min(ATTN_UNROLL, nkv))

    vb_last = values(nkv - 1)
    for st in range(ns):
        fold(st, vb_last)
        acc = acc_ref[st, 0:dv]
        inv = 1.0 / acc_ref[st, dv:dv + 1]
        ot = acc[:, :LANES] * inv[:, :LANES] - lam * (acc[:, LANES:] * inv[:, LANES:])
        o = ot.T
        y = o * lax.rsqrt(jnp.mean(o * o, axis=-1, keepdims=True) + SUBLN_EPS) * g_ref[...] * (1.0 - LAM_INIT)
        za = za_ref[0, st * LANES:(st + 1) * LANES, :].astype(f32)
        o_ref[0, st * LANES:(st + 1) * LANES, :] = (y * (za * jax.nn.sigmoid(za))).astype(bf16)


def _attention(lam, qt, k, vt, subln_g, za, *, bq, bk):
    B, L, _ = k.shape
    nkv = L // bk
    nvt = vt.shape[1]
    return pl.pallas_call(
        functools.partial(_attn_kernel, nkv=nkv, bq=bq, bk=bk),
        grid=(B, N_HEADS, L // bq),
        in_specs=[
            pl.BlockSpec(memory_space=pltpu.SMEM),
            pl.BlockSpec((1, 2 * HEAD_DIM, bq), lambda b, h, i: (b, h, i)),
            pl.BlockSpec((1, L, 2 * HEAD_DIM), lambda b, h, i: (b, 0, h)),
            pl.BlockSpec((1, nvt, V_ROWS, bk // 2), lambda b, h, i: (b, 0, h, 0)),
            pl.BlockSpec((1, 2 * HEAD_DIM), lambda b, h, i: (0, 0)),
            pl.BlockSpec((1, bq, 2 * HEAD_DIM), lambda b, h, i: (b, i, h)),
        ],
        out_specs=pl.BlockSpec((1, bq, 2 * HEAD_DIM), lambda b, h, i: (b, i, h)),
        out_shape=jax.ShapeDtypeStruct((B, L, D_ATTN), bf16),
        scratch_shapes=[
            pltpu.VMEM((bq // LANES, 2 * HEAD_DIM, 2 * LANES), bf16),
            pltpu.VMEM((bq // LANES, V_ROWS, 2 * LANES), f32),
            pltpu.VMEM((bq // LANES, bk, 2 * LANES), bf16),
            pltpu.VMEM((bq // LANES, F32_ROWS, 2 * LANES), f32),
            pltpu.VMEM((bq // LANES, F32_ROWS, 2 * LANES), f32),
        ],
        compiler_params=_params("parallel", "parallel", "arbitrary"),
        name="attention",
    )(lam, qt, k, vt, subln_g, za)


def _out_proj_kernel(x_ref, yssm_ref, zs_ref, ya_ref, g_ref, wg_ref, wglu_ref, bglu_ref,
                     wb_ref, wout_ref, fg_ref, o_ref):
    tm = x_ref.shape[1]
    hm = tm // OUT_PROJ_CHAINS
    for c in range(OUT_PROJ_CHAINS):
        rs = slice(c * hm, (c + 1) * hm)
        x = x_ref[0, rs]
        r = lax.rsqrt(jnp.mean(x * x, axis=-1, keepdims=True) + NORM_EPS)
        h = (x * r * g_ref[...]).astype(bf16)

        ys = jax.nn.gelu(yssm_ref[0, rs].astype(f32))
        ys = ys * jax.nn.sigmoid(_dot(ys.astype(bf16), wglu_ref[...]) + bglu_ref[...])
        zs = zs_ref[0, rs].astype(f32)
        ys = ys * (zs * jax.nn.sigmoid(zs))

        ps = _dot(ys.astype(bf16), wb_ref[0])
        merged = jax.nn.sigmoid(_dot(h, wg_ref[:, 0:D_MODEL])) * ps
        pa = _dot(ya_ref[0, rs], wb_ref[1])
        merged = merged + jax.nn.sigmoid(_dot(h, wg_ref[:, D_MODEL:2 * D_MODEL])) * pa
        out = x + _dot(merged.astype(bf16), wout_ref[...])
        o_ref[0, rs] = out * lax.rsqrt(jnp.mean(out * out, axis=-1, keepdims=True) + NORM_EPS) * fg_ref[...]


def _out_proj(x, yssm, zs, ya, norm_g, wg, wglu, bglu, wb, wout, final_g, *, tm):
    B, L, _ = x.shape
    tok = lambda b, i: (b, i, 0)
    cst = lambda b, i: (0, 0)
    return pl.pallas_call(
        _out_proj_kernel,
        grid=(B, L // tm),
        in_specs=[
            pl.BlockSpec((1, tm, D_MODEL), tok),
            pl.BlockSpec((1, tm, 512), tok),
            pl.BlockSpec((1, tm, 512), tok),
            pl.BlockSpec((1, tm, 512), tok),
            pl.BlockSpec((1, D_MODEL), cst),
            pl.BlockSpec(wg.shape, cst),
            pl.BlockSpec(wglu.shape, cst),
            pl.BlockSpec((1, D_SSM), cst),
            pl.BlockSpec(wb.shape, lambda b, i: (0, 0, 0)),
            pl.BlockSpec(wout.shape, cst),
            pl.BlockSpec((1, D_MODEL), cst),
        ],
        out_specs=pl.BlockSpec((1, tm, D_MODEL), tok),
        out_shape=jax.ShapeDtypeStruct((B, L, D_MODEL), x.dtype),
        compiler_params=_params("parallel", "parallel"),
        name="out_proj",
    )(x, yssm, zs, ya, norm_g, wg, wglu, bglu, wb, wout, final_g)


def _rotary_tables(L):
    half = HEAD_DIM // 2
    inv_freq = 1.0 / (ROPE_THETA ** (jnp.arange(0, half, dtype=f32) * 2.0 / HEAD_DIM))
    ang = jnp.arange(L, dtype=f32)[:, None] * inv_freq[None, :]
    cos, sin = jnp.cos(ang), jnp.sin(ang)
    ck = jnp.tile(cos, (1, LANES // half))
    sk = jnp.tile(sin, (1, LANES // half))
    scale = math.log2(math.e) / math.sqrt(HEAD_DIM)
    return ck, sk, cos.T * scale, sin.T * scale


def _trunk(x, w):
    B, L, _ = x.shape
    assert B == 8 and L % ATTN_BQ == 0, "scan state vregs hold one row per batch element"
    ck, sk, cq, sq = _rotary_tables(L)
    u, zs, k, za, qt, vt = _in_proj(x, w["norm_g"], w["wa"], w["wbt"], ck, sk, cq, sq, tb=PROJ_TOKENS)
    hf, gb = _ssm_state(u, w["bst_f"], w["bst_b"], w["coef"], nb=B)
    yssm = _ssm_out(u, hf, gb, w["m"], w["cst_f"], w["cst_b"], nb=B)
    ya = _attention(w["lam"], qt, k, vt, w["subln_g"], za, bq=ATTN_BQ, bk=ATTN_BK)
    return _out_proj(x, yssm, zs, ya, w["norm_g"], w["wg"], w["wglu"], w["bglu"], w["wb"], w["wout"],
                     w["final_g"], tm=TOKEN_BLOCK)


def _rotate_half_columns(wk):
    d = wk.shape[0]
    w4 = wk.reshape(d, D_ATTN // HEAD_DIM, 2, HEAD_DIM // 2)
    return jnp.stack([-w4[:, :, 1], w4[:, :, 0]], axis=2).reshape(d, D_ATTN)


def kernel(x_prompt, x_sample, norm_g, w_in, ssm_a_re, ssm_a_im, ssm_log_dt, ssm_b_re, ssm_b_im, ssm_c_re, ssm_c_im, ssm_d, w_glu, b_glu, lambda_q1, lambda_k1, lambda_q2, lambda_k2, subln_g, w_branch, w_out, final_g):
    li = 0
    wi = w_in[li].astype(f32)
    w_xs, w_zs = wi[:, 0:512], wi[:, 512:1024]
    w_q, w_k, w_v, w_za = wi[:, 1024:1536], wi[:, 1536:2048], wi[:, 2048:2560], wi[:, 2560:3072]
    m, bst_f, bst_b, cst_f, cst_b, coef = _ssm_tables(
        ssm_a_re[li], ssm_a_im[li], ssm_log_dt[li], ssm_b_re[li], ssm_b_im[li],
        ssm_c_re[li], ssm_c_im[li], ssm_d[li])
    nb = x_prompt.shape[0]
    lam = (jnp.exp(jnp.sum(lambda_q1[li].astype(f32) * lambda_k1[li].astype(f32)))
           - jnp.exp(jnp.sum(lambda_q2[li].astype(f32) * lambda_k2[li].astype(f32))) + LAM_INIT)
    w = dict(
        norm_g=norm_g[li].astype(f32).reshape(1, D_MODEL),
        wa=jnp.concatenate([w_xs, w_zs, w_k, _rotate_half_columns(w_k), w_za], axis=1).astype(bf16),
        wbt=jnp.concatenate([w_q, w_v], axis=1).T.astype(bf16),
        wg=wi[:, 3072:5120].astype(bf16),
        m=m, bst_f=bst_f, bst_b=bst_b, cst_f=cst_f, cst_b=cst_b,
        coef=jnp.broadcast_to(coef[:, :, :, None, :], (2, N_GROUPS, 3, nb, LANES)),
        lam=lam.reshape(1).astype(f32),
        subln_g=subln_g[li].astype(f32).reshape(1, 2 * HEAD_DIM),
        wglu=w_glu[li].astype(bf16),
        bglu=b_glu[li].astype(f32).reshape(1, D_SSM),
        wb=w_branch[li].astype(bf16),
        wout=w_out[li].astype(bf16),
        final_g=final_g.astype(f32).reshape(1, D_MODEL),
    )
    return (_trunk(x_prompt, w), _trunk(x_sample, w))
```

```python
import functools
import math

import jax
import jax.numpy as jnp
from jax import lax
from jax.experimental import pallas as pl
from jax.experimental.pallas import tpu as pltpu

D_MODEL = 1024
D_SSM = 512
SSM_GROUP = 16
N_GROUPS = 32
STATE = 64
D_ATTN = 512
N_HEADS = 4
HEAD_DIM = 64
ROPE_THETA = 10000.0
NORM_EPS = 1e-6
SUBLN_EPS = 1e-5
LAM_INIT = 0.8 - 0.6 * math.exp(-0.3 * 0)

CHUNK = 16
CHUNK_LANES = CHUNK * SSM_GROUP
LANES = 128
VMEM_LIMIT = 56 * 1024 * 1024

TOKEN_BLOCK = 1024
OUT_PROJ_CHAINS = 4
ATTN_BQ = 2048
ATTN_UNROLL = 16
ATTN_BK = 256
PROJ_TOKENS = ATTN_BK // 2
F32_ROWS = 8
ROW_PAD = 4
BF16_ROWS = 16
V_ROWS = 2 * HEAD_DIM + BF16_ROWS
SSM_SEG_CHUNKS = 16
SSM_GROUP_BLOCK = 4
SSM_OUT_ROWS = 128

f32 = jnp.float32
bf16 = jnp.bfloat16


def _params(*sem):
    return pltpu.CompilerParams(dimension_semantics=sem, vmem_limit_bytes=VMEM_LIMIT)


def _dot(a, b):
    return jnp.dot(a, b, preferred_element_type=f32)


def _dot_nt(a, b):
    return lax.dot_general(a, b, (((1,), (1,)), ((), ())), preferred_element_type=f32)


def _segment_transpose8(vs):
    slot = lax.broadcasted_iota(jnp.int32, vs[0].shape, 1) // SSM_GROUP
    for d in (4, 2, 1):
        keep = (slot & d) == 0
        new = list(vs)
        for i in range(8):
            if i & d == 0:
                a, b = vs[i], vs[i + d]
                new[i] = jnp.where(keep, a, pltpu.roll(b, d * SSM_GROUP, 1))
                new[i + d] = jnp.where(keep, pltpu.roll(a, LANES - d * SSM_GROUP, 1), b)
        vs = new
    return vs


def _in_proj_kernel(x_ref, g_ref, wa_ref, wbt_ref, ck_ref, sk_ref, cq_ref, sq_ref,
                    u_ref, zs_ref, k_ref, za_ref, qt_ref, vt_ref, xs_s, *, nb, tb):
    rows = nb * tb
    x = x_ref[...].reshape(rows, D_MODEL)
    r = lax.rsqrt(jnp.mean(x * x, axis=-1, keepdims=True) + NORM_EPS)
    h = (x * r * g_ref[...]).astype(bf16)

    for tile in range(D_SSM // LANES):
        xs = _dot(h, wa_ref[:, tile * LANES:(tile + 1) * LANES])
        for b in range(nb):
            xs_s[tile, b * (tb + ROW_PAD):b * (tb + ROW_PAD) + tb] = xs[b * tb:(b + 1) * tb]
    zs_ref[...] = _dot(h, wa_ref[:, 512:1024]).astype(bf16).reshape(nb, tb, D_SSM)
    kk = _dot(h, wa_ref[:, 1024:1536]).reshape(nb, tb, D_ATTN)
    kr = _dot(h, wa_ref[:, 1536:2048]).reshape(nb, tb, D_ATTN)
    ck = ck_ref[...]
    sk = sk_ref[...]
    for j in range(D_ATTN // LANES):
        sl = slice(j * LANES, (j + 1) * LANES)
        k_ref[:, :, sl] = (kk[:, :, sl] * ck + kr[:, :, sl] * sk).astype(bf16)
    za_ref[...] = _dot(h, wa_ref[:, 2048:2560]).astype(bf16).reshape(nb, tb, D_ATTN)

    pq = _dot_nt(wbt_ref[0:512, :], h)
    cq = jnp.tile(cq_ref[...], (1, nb))
    sq = jnp.tile(sq_ref[...], (1, nb))
    half = HEAD_DIM // 2
    for hb in range(D_ATTN // HEAD_DIM):
        x1 = pq[hb * HEAD_DIM:hb * HEAD_DIM + half]
        x2 = pq[hb * HEAD_DIM + half:(hb + 1) * HEAD_DIM]
        q1 = (x1 * cq - x2 * sq).astype(bf16)
        q2 = (x2 * cq + x1 * sq).astype(bf16)
        for b in range(nb):
            qt_ref[b, hb * HEAD_DIM:hb * HEAD_DIM + half, :] = q1[:, b * tb:(b + 1) * tb]
            qt_ref[b, hb * HEAD_DIM + half:(hb + 1) * HEAD_DIM, :] = q2[:, b * tb:(b + 1) * tb]
    pv = _dot_nt(wbt_ref[512:1024, :], h).astype(bf16)
    dv = 2 * HEAD_DIM
    ones = jnp.ones((V_ROWS - dv, tb), bf16)
    for b in range(nb):
        for hd in range(N_HEADS):
            vt_ref[b, 0, hd * V_ROWS:hd * V_ROWS + dv, :] = pv[hd * dv:(hd + 1) * dv, b * tb:(b + 1) * tb]
            vt_ref[b, 0, hd * V_ROWS + dv:(hd + 1) * V_ROWS, :] = ones

    nchunk = tb // CHUNK
    for tile in range(D_SSM // LANES):
        for hf in range(CHUNK_LANES // LANES):
            vs = []
            for t8 in range(8):
                t = 8 * hf + t8
                vs.append(jnp.concatenate(
                    [xs_s[tile, pl.ds(t + CHUNK * ch, nb, stride=tb + ROW_PAD), :] for ch in range(nchunk)],
                    axis=0))
            out = _segment_transpose8(vs)
            for g8 in range(8):
                u_ref[8 * tile + g8, :, hf * LANES:(hf + 1) * LANES] = out[g8].astype(bf16)


def _in_proj(x, norm_g, wa, wbt, ck, sk, cq, sq, *, tb):
    B, L, _ = x.shape
    nchunk = tb // CHUNK
    tok = lambda i: (0, i, 0)
    cst = lambda i: (0, 0)
    out_tok = jax.ShapeDtypeStruct((B, L, 512), bf16)
    return pl.pallas_call(
        functools.partial(_in_proj_kernel, nb=B, tb=tb),
        grid=(L // tb,),
        in_specs=[
            pl.BlockSpec((B, tb, D_MODEL), tok),
            pl.BlockSpec((1, D_MODEL), cst),
            pl.BlockSpec(wa.shape, cst),
            pl.BlockSpec(wbt.shape, cst),
            pl.BlockSpec((tb, LANES), lambda i: (i, 0)),
            pl.BlockSpec((tb, LANES), lambda i: (i, 0)),
            pl.BlockSpec((HEAD_DIM // 2, tb), lambda i: (0, i)),
            pl.BlockSpec((HEAD_DIM // 2, tb), lambda i: (0, i)),
        ],
        out_specs=[
            pl.BlockSpec((N_GROUPS, nchunk * B, CHUNK_LANES), tok),
            pl.BlockSpec((B, tb, 512), tok),
            pl.BlockSpec((B, tb, 512), tok),
            pl.BlockSpec((B, tb, 512), tok),
            pl.BlockSpec((B, 512, tb), lambda i: (0, 0, i)),
            pl.BlockSpec((B, 1, N_HEADS * V_ROWS, tb), lambda i: (0, i, 0, 0)),
        ],
        out_shape=[jax.ShapeDtypeStruct((N_GROUPS, (L // CHUNK) * B, CHUNK_LANES), bf16),
                   out_tok, out_tok, out_tok,
                   jax.ShapeDtypeStruct((B, 512, L), bf16),
                   jax.ShapeDtypeStruct((B, L // tb, N_HEADS * V_ROWS, tb), bf16)],
        scratch_shapes=[pltpu.VMEM((D_SSM // LANES, B * (tb + ROW_PAD), LANES), f32)],
        compiler_params=_params("parallel"),
        name="in_proj",
    )(x, norm_g, wa, wbt, ck, sk, cq, sq)


def _ssm_state_kernel(uf_ref, ub_ref, bf_ref, bb_ref, coef_ref, hf_ref, gb_ref, s_ref, st_ref,
                      *, nc, nb, gblk):
    @pl.when(pl.program_id(0) == 0)
    def _():
        st_ref[...] = jnp.zeros_like(st_ref)

    for g0 in range(0, N_GROUPS, gblk):
        for gi in range(gblk):
            s_ref[0, gi] = _dot(uf_ref[g0 + gi], bf_ref[g0 + gi])
            s_ref[1, gi] = _dot(ub_ref[g0 + gi], bb_ref[g0 + gi])
        gs = slice(g0, g0 + gblk)
        af1, af2, af3 = coef_ref[0, gs, 0], coef_ref[0, gs, 1], coef_ref[0, gs, 2]
        ab1, ab2, ab3 = coef_ref[1, gs, 0], coef_ref[1, gs, 1], coef_ref[1, gs, 2]

        def body(i, carry):
            hf, wf, hb, wb = carry
            rf = pl.multiple_of(i * nb, nb)
            rb = pl.multiple_of((nc - 1 - i) * nb, nb)
            hf_ref[gs, pl.ds(rf, nb), :] = hf
            gb_ref[gs, pl.ds(rb, nb), :] = hb
            sf = s_ref[0, :, pl.ds(rf, nb), :]
            sb = s_ref[1, :, pl.ds(rb, nb), :]
            hf2 = af1 * hf + af2 * wf + sf[..., :LANES]
            wf2 = af1 * wf + af3 * hf + sf[..., LANES:]
            hb2 = ab1 * hb + ab2 * wb + sb[..., :LANES]
            wb2 = ab1 * wb + ab3 * hb + sb[..., LANES:]
            return hf2, wf2, hb2, wb2

        init = (st_ref[0, 0, gs], st_ref[0, 1, gs], st_ref[1, 0, gs], st_ref[1, 1, gs])
        hf, wf, hb, wb = lax.fori_loop(0, nc, body, init)
        st_ref[0, 0, gs] = hf
        st_ref[0, 1, gs] = wf
        st_ref[1, 0, gs] = hb
        st_ref[1, 1, gs] = wb


def _ssm_state(u, bst_f, bst_b, coef, *, nb):
    G, rows, _ = u.shape
    nc = SSM_SEG_CHUNKS
    seg_rows = nc * nb
    nseg = rows // seg_rows
    gblk = SSM_GROUP_BLOCK
    fwd = lambda i: (0, i, 0)
    bwd = lambda i: (0, nseg - 1 - i, 0)
    cst3 = lambda i: (0, 0, 0)
    return pl.pallas_call(
        functools.partial(_ssm_state_kernel, nc=nc, nb=nb, gblk=gblk),
        grid=(nseg,),
        in_specs=[
            pl.BlockSpec((G, seg_rows, CHUNK_LANES), fwd),
            pl.BlockSpec((G, seg_rows, CHUNK_LANES), bwd),
            pl.BlockSpec(bst_f.shape, cst3),
            pl.BlockSpec(bst_b.shape, cst3),
            pl.BlockSpec(coef.shape, lambda i: (0, 0, 0, 0, 0)),
        ],
        out_specs=[
            pl.BlockSpec((G, seg_rows, LANES), fwd),
            pl.BlockSpec((G, seg_rows, LANES), bwd),
        ],
        out_shape=[jax.ShapeDtypeStruct((G, rows, LANES), f32),
                   jax.ShapeDtypeStruct((G, rows, LANES), f32)],
        scratch_shapes=[
            pltpu.VMEM((2, gblk, seg_rows, CHUNK_LANES), f32),
            pltpu.VMEM((2, 2, G, nb, LANES), f32),
        ],
        compiler_params=_params("arbitrary"),
        name="ssm_state",
    )(u, u, bst_f, bst_b, coef)


def _ssm_out_kernel(u_ref, hf_ref, gb_ref, m_ref, cf_ref, cb_ref, y_ref, nat_s, *, nb, nchunk):
    tok = nchunk * CHUNK
    for tile in range(D_SSM // LANES):
        ys = []
        for g8 in range(8):
            g = 8 * tile + g8
            y = _dot(u_ref[g], m_ref[g])
            y = y + _dot(hf_ref[g].astype(bf16), cf_ref[g])
            y = y + _dot(gb_ref[g].astype(bf16), cb_ref[g])
            ys.append(y)
        for hf in range(CHUNK_LANES // LANES):
            out = _segment_transpose8([y[:, hf * LANES:(hf + 1) * LANES] for y in ys])
            for t8 in range(8):
                t = 8 * hf + t8
                for ch in range(nchunk):
                    nat_s[tile, pl.ds(t + CHUNK * ch, nb, stride=tok + ROW_PAD), :] = out[t8][ch * nb:(ch + 1) * nb]
        for b in range(nb):
            y_ref[b, :, tile * LANES:(tile + 1) * LANES] = (
                nat_s[tile, b * (tok + ROW_PAD):b * (tok + ROW_PAD) + tok].astype(bf16))


def _ssm_out(u, hf, gb, m, cst_f, cst_b, *, nb):
    G, rows, _ = u.shape
    rb = min(SSM_OUT_ROWS, rows)
    nchunk = rb // nb
    tok = nchunk * CHUNK
    blk = lambda i: (0, i, 0)
    cst3 = lambda i: (0, 0, 0)
    return pl.pallas_call(
        functools.partial(_ssm_out_kernel, nb=nb, nchunk=nchunk),
        grid=(rows // rb,),
        in_specs=[
            pl.BlockSpec((G, rb, CHUNK_LANES), blk),
            pl.BlockSpec((G, rb, LANES), blk),
            pl.BlockSpec((G, rb, LANES), blk),
            pl.BlockSpec(m.shape, cst3),
            pl.BlockSpec(cst_f.shape, cst3),
            pl.BlockSpec(cst_b.shape, cst3),
        ],
        out_specs=pl.BlockSpec((nb, tok, D_SSM), blk),
        out_shape=jax.ShapeDtypeStruct((nb, (rows // nb) * CHUNK, D_SSM), bf16),
        scratch_shapes=[pltpu.VMEM((D_SSM // LANES, nb * (tok + ROW_PAD), LANES), f32)],
        compiler_params=_params("parallel"),
        name="ssm_out",
    )(u, hf, gb, m, cst_f, cst_b)


def _ssm_tables(a_re, a_im, log_dt, b_re, b_im, c_re, c_im, d_skip):
    T, G, P, C = CHUNK, N_GROUPS, STATE, SSM_GROUP
    hi = lax.Precision.HIGHEST
    a_re, a_im, log_dt = a_re.astype(f32), a_im.astype(f32), log_dt.astype(f32)
    dt = jnp.exp(log_dt)[..., None]
    H = T // 2
    ks = jnp.arange(-H, T + 1, dtype=f32)[:, None, None, None]
    mag = jnp.exp(ks * (a_re * dt))
    pw_re = mag * jnp.cos(ks * (a_im * dt))
    pw_im = mag * jnp.sin(ks * (a_im * dt))
    n_re, n_im = pw_re[H + 1] - 1.0, pw_im[H + 1]
    den = a_re * a_re + a_im * a_im
    co_re = (n_re * a_re + n_im * a_im) / den
    co_im = (n_im * a_re - n_re * a_im) / den
    b_re, b_im = b_re.astype(f32), b_im.astype(f32)
    bb_re = co_re[..., None] * b_re - co_im[..., None] * b_im
    bb_im = co_re[..., None] * b_im + co_im[..., None] * b_re
    c_re, c_im = c_re.astype(f32), c_im.astype(f32)

    def xtab(n, pr, pi):
        re = pr[..., None] * bb_re[n][None] - pi[..., None] * bb_im[n][None]
        im = pr[..., None] * bb_im[n][None] + pi[..., None] * bb_re[n][None]
        return (re.transpose(1, 0, 3, 2).reshape(G, T * C, P), im.transpose(1, 0, 3, 2).reshape(G, T * C, P))

    def ytab(n, pr, pi):
        re = c_re[n][None] * pr[:, :, None, :] - c_im[n][None] * pi[:, :, None, :]
        im = c_re[n][None] * pi[:, :, None, :] + c_im[n][None] * pr[:, :, None, :]
        return (re.transpose(1, 3, 0, 2).reshape(G, P, T * C), im.transpose(1, 3, 0, 2).reshape(G, P, T * C))

    def lag_kernels(x, y):
        return (jnp.einsum('gjp,gpt->gjt', x[0], y[0], precision=hi)
                - jnp.einsum('gjp,gpt->gjt', x[1], y[1], precision=hi))

    def powers(n, lo, reverse=False):
        pr, pi = pw_re[lo + H:lo + H + T, n], pw_im[lo + H:lo + H + T, n]
        return (pr[::-1], pi[::-1]) if reverse else (pr, pi)

    kf = lag_kernels(xtab(0, *powers(0, -H, reverse=True)), ytab(0, *powers(0, 1 - H)))
    kb = lag_kernels(xtab(1, *powers(1, -H)), ytab(1, *powers(1, 1 - H, reverse=True)))
    step = jnp.arange(T * C) // C
    causal = step[None, :] >= step[:, None]
    anti = step[None, :] <= step[:, None]
    d_diag = jnp.tile(d_skip.astype(f32).reshape(G, 1, C), (1, T, 1)).reshape(G, 1, T * C)
    m = (jnp.where(causal[None], kf, 0.0) + jnp.where(anti[None], kb, 0.0)
         + jnp.eye(T * C, dtype=f32)[None] * d_diag)

    xf = xtab(0, *powers(0, 0, reverse=True))
    xb = xtab(1, *powers(1, 0))
    yf = ytab(0, *powers(0, 1))
    yb = ytab(1, *powers(1, 1, reverse=True))

    bst_f = jnp.concatenate([xf[0], xf[1], xf[1], xf[0]], axis=-1)
    bst_b = jnp.concatenate([xb[0], xb[1], xb[1], xb[0]], axis=-1)
    cst_f = jnp.concatenate([yf[0], -yf[1]], axis=1)
    cst_b = jnp.concatenate([yb[0], -yb[1]], axis=1)
    ar, ai = pw_re[H + T], pw_im[H + T]
    coef = jnp.stack([jnp.concatenate([ar, ar], -1),
                      jnp.concatenate([-ai, ai], -1),
                      jnp.concatenate([ai, -ai], -1)], axis=2)
    return m.astype(bf16), bst_f.astype(bf16), bst_b.astype(bf16), cst_f.astype(bf16), cst_b.astype(bf16), coef


def _attn_kernel(lam_ref, qt_ref, k_ref, vt_ref, g_ref, za_ref, o_ref,
                 qbd_ref, acc_ref, p_ref, m_ref, a_ref, *, nkv, bq, bk):
    lam = lam_ref[0]
    half = HEAD_DIM
    ns = bq // LANES
    dv = 2 * HEAD_DIM
    zero = jnp.zeros((half, LANES), bf16)
    for st in range(ns):
        qs = qt_ref[0, :, st * LANES:(st + 1) * LANES]
        qbd_ref[st, 0:half, 0:LANES] = qs[0:half]
        qbd_ref[st, 0:half, LANES:2 * LANES] = zero
        qbd_ref[st, half:2 * half, 0:LANES] = zero
        qbd_ref[st, half:2 * half, LANES:2 * LANES] = qs[half:2 * half]
    acc_ref[...] = jnp.zeros_like(acc_ref)
    p_ref[...] = jnp.zeros_like(p_ref)
    a_ref[...] = jnp.ones_like(a_ref)
    m_ref[...] = jnp.full(m_ref.shape, -jnp.inf, f32)

    def values(j):
        return jnp.concatenate([vt_ref[0, 2 * j], vt_ref[0, 2 * j + 1]], axis=1)

    nsub = F32_ROWS

    def fold(st, vb):
        acc = acc_ref[st].reshape(V_ROWS // nsub, nsub, 2 * LANES) * a_ref[st][None]
        acc_ref[st] = acc.reshape(V_ROWS, 2 * LANES) + _dot(vb, p_ref[st])

    def colmax(s):
        mx = jnp.max(s.reshape(-1, nsub, 2 * LANES), axis=0)
        for sh in (4, 2, 1):
            mx = jnp.maximum(mx, pltpu.roll(mx, sh, 0))
        return mx

    def probs(s, m):
        x = s.reshape(-1, nsub, 2 * LANES) - m[None]
        return jnp.exp2(x.reshape(s.shape).astype(bf16))

    def body(j, carry):
        off = pl.multiple_of(j * bk, bk)
        ka = k_ref[0, pl.ds(off, bk // 2), :]
        kb = k_ref[0, pl.ds(off + bk // 2, bk // 2), :]
        vb = values(jnp.maximum(j - 1, 0))
        for st in range(ns):
            fold(st, vb)
            m_old = m_ref[st]
            s_a = _dot(ka, qbd_ref[st])
            m_a = jnp.maximum(m_old, colmax(s_a))
            p_a = probs(s_a, m_a)
            s_b = _dot(kb, qbd_ref[st])
            m_b = jnp.maximum(m_a, colmax(s_b))
            p_b = probs(s_b, m_b)
            corr = jnp.exp2(m_a - m_b)
            corr = jnp.concatenate([corr, corr], axis=0).astype(bf16)
            p_a = p_a.reshape(-1, BF16_ROWS, 2 * LANES) * corr[None]
            p_ref[st, 0:bk // 2] = p_a.reshape(bk // 2, 2 * LANES)
            p_ref[st, bk // 2:bk] = p_b
            a_ref[st] = jnp.exp2(m_old - m_b)
            m_ref[st] = m_b
        return carry

    lax.fori_loop(0, nkv, body, 0, unroll=min(ATTN_UNROLL, nkv))

    vb_last = values(nkv - 1)
    for st in range(ns):
        fold(st, vb_last)
        acc = acc_ref[st, 0:dv]
        inv = 1.0 / acc_ref[st, dv:dv + 1]
        ot = acc[:, :LANES] * inv[:, :LANES] - lam * (acc[:, LANES:] * inv[:, LANES:])
        o = ot.T
        y = o * lax.rsqrt(jnp.mean(o * o, axis=-1, keepdims=True) + SUBLN_EPS) * g_ref[...] * (1.0 - LAM_INIT)
        za = za_ref[0, st * LANES:(st + 1) * LANES, :].astype(f32)
        o_ref[0, st * LANES:(st + 1) * LANES, :] = (y * (za * jax.nn.sigmoid(za))).astype(bf16)


def _attention(lam, qt, k, vt, subln_g, za, *, bq, bk):
    B, L, _ = k.shape
    nkv = L // bk
    nvt = vt.shape[1]
    return pl.pallas_call(
        functools.partial(_attn_kernel, nkv=nkv, bq=bq, bk=bk),
        grid=(B, N_HEADS, L // bq),
        in_specs=[
            pl.BlockSpec(memory_space=pltpu.SMEM),
            pl.BlockSpec((1, 2 * HEAD_DIM, bq), lambda b, h, i: (b, h, i)),
            pl.BlockSpec((1, L, 2 * HEAD_DIM), lambda b, h, i: (b, 0, h)),
            pl.BlockSpec((1, nvt, V_ROWS, bk // 2), lambda b, h, i: (b, 0, h, 0)),
            pl.BlockSpec((1, 2 * HEAD_DIM), lambda b, h, i: (0, 0)),
            pl.BlockSpec((1, bq, 2 * HEAD_DIM), lambda b, h, i: (b, i, h)),
        ],
        out_specs=pl.BlockSpec((1, bq, 2 * HEAD_DIM), lambda b, h, i: (b, i, h)),
        out_shape=jax.ShapeDtypeStruct((B, L, D_ATTN), bf16),
        scratch_shapes=[
            pltpu.VMEM((bq // LANES, 2 * HEAD_DIM, 2 * LANES), bf16),
            pltpu.VMEM((bq // LANES, V_ROWS, 2 * LANES), f32),
            pltpu.VMEM((bq // LANES, bk, 2 * LANES), bf16),
            pltpu.VMEM((bq // LANES, F32_ROWS, 2 * LANES), f32),
            pltpu.VMEM((bq // LANES, F32_ROWS, 2 * LANES), f32),
        ],
        compiler_params=_params("parallel", "parallel", "arbitrary"),
        name="attention",
    )(lam, qt, k, vt, subln_g, za)


def _out_proj_kernel(x_ref, yssm_ref, zs_ref, ya_ref, g_ref, wg_ref, wglu_ref, bglu_ref,
                     wb_ref, wout_ref, fg_ref, o_ref):
    tm = x_ref.shape[1]
    hm = tm // OUT_PROJ_CHAINS
    for c in range(OUT_PROJ_CHAINS):
        rs = slice(c * hm, (c + 1) * hm)
        x = x_ref[0, rs]
        r = lax.rsqrt(jnp.mean(x * x, axis=-1, keepdims=True) + NORM_EPS)
        h = (x * r * g_ref[...]).astype(bf16)

        ys = jax.nn.gelu(yssm_ref[0, rs].astype(f32))
        ys = ys * jax.nn.sigmoid(_dot(ys.astype(bf16), wglu_ref[...]) + bglu_ref[...])
        zs = zs_ref[0, rs].astype(f32)
        ys = ys * (zs * jax.nn.sigmoid(zs))

        ps = _dot(ys.astype(bf16), wb_ref[0])
        merged = jax.nn.sigmoid(_dot(h, wg_ref[:, 0:D_MODEL])) * ps
        pa = _dot(ya_ref[0, rs], wb_ref[1])
        merged = merged + jax.nn.sigmoid(_dot(h, wg_ref[:, D_MODEL:2 * D_MODEL])) * pa
        out = x + _dot(merged.astype(bf16), wout_ref[...])
        o_ref[0, rs] = out * lax.rsqrt(jnp.mean(out * out, axis=-1, keepdims=True) + NORM_EPS) * fg_ref[...]


def _out_proj(x, yssm, zs, ya, norm_g, wg, wglu, bglu, wb, wout, final_g, *, tm):
    B, L, _ = x.shape
    tok = lambda b, i: (b, i, 0)
    cst = lambda b, i: (0, 0)
    return pl.pallas_call(
        _out_proj_kernel,
        grid=(B, L // tm),
        in_specs=[
            pl.BlockSpec((1, tm, D_MODEL), tok),
            pl.BlockSpec((1, tm, 512), tok),
            pl.BlockSpec((1, tm, 512), tok),
            pl.BlockSpec((1, tm, 512), tok),
            pl.BlockSpec((1, D_MODEL), cst),
            pl.BlockSpec(wg.shape, cst),
            pl.BlockSpec(wglu.shape, cst),
            pl.BlockSpec((1, D_SSM), cst),
            pl.BlockSpec(wb.shape, lambda b, i: (0, 0, 0)),
            pl.BlockSpec(wout.shape, cst),
            pl.BlockSpec((1, D_MODEL), cst),
        ],
        out_specs=pl.BlockSpec((1, tm, D_MODEL), tok),
        out_shape=jax.ShapeDtypeStruct((B, L, D_MODEL), x.dtype),
        compiler_params=_params("parallel", "parallel"),
        name="out_proj",
    )(x, yssm, zs, ya, norm_g, wg, wglu, bglu, wb, wout, final_g)


def _rotary_tables(L):
    half = HEAD_DIM // 2
    inv_freq = 1.0 / (ROPE_THETA ** (jnp.arange(0, half, dtype=f32) * 2.0 / HEAD_DIM))
    ang = jnp.arange(L, dtype=f32)[:, None] * inv_freq[None, :]
    cos, sin = jnp.cos(ang), jnp.sin(ang)
    ck = jnp.tile(cos, (1, LANES // half))
    sk = jnp.tile(sin, (1, LANES // half))
    scale = math.log2(math.e) / math.sqrt(HEAD_DIM)
    return ck, sk, cos.T * scale, sin.T * scale


def _trunk(x, w):
    B, L, _ = x.shape
    assert B == 8 and L % ATTN_BQ == 0, "scan state vregs hold one row per batch element"
    ck, sk, cq, sq = _rotary_tables(L)
    u, zs, k, za, qt, vt = _in_proj(x, w["norm_g"], w["wa"], w["wbt"], ck, sk, cq, sq, tb=PROJ_TOKENS)
    hf, gb = _ssm_state(u, w["bst_f"], w["bst_b"], w["coef"], nb=B)
    yssm = _ssm_out(u, hf, gb, w["m"], w["cst_f"], w["cst_b"], nb=B)
    ya = _attention(w["lam"], qt, k, vt, w["subln_g"], za, bq=ATTN_BQ, bk=ATTN_BK)
    return _out_proj(x, yssm, zs, ya, w["norm_g"], w["wg"], w["wglu"], w["bglu"], w["wb"], w["wout"],
                     w["final_g"], tm=TOKEN_BLOCK)


def _rotate_half_columns(wk):
    d = wk.shape[0]
    w4 = wk.reshape(d, D_ATTN // HEAD_DIM, 2, HEAD_DIM // 2)
    return jnp.stack([-w4[:, :, 1], w4[:, :, 0]], axis=2).reshape(d, D_ATTN)


def kernel(x_prompt, x_sample, norm_g, w_in, ssm_a_re, ssm_a_im, ssm_log_dt, ssm_b_re, ssm_b_im, ssm_c_re, ssm_c_im, ssm_d, w_glu, b_glu, lambda_q1, lambda_k1, lambda_q2, lambda_k2, subln_g, w_branch, w_out, final_g):
    li = 0
    wi = w_in[li].astype(f32)
    w_xs, w_zs = wi[:, 0:512], wi[:, 512:1024]
    w_q, w_k, w_v, w_za = wi[:, 1024:1536], wi[:, 1536:2048], wi[:, 2048:2560], wi[:, 2560:3072]
    m, bst_f, bst_b, cst_f, cst_b, coef = _ssm_tables(
        ssm_a_re[li], ssm_a_im[li], ssm_log_dt[li], ssm_b_re[li], ssm_b_im[li],
        ssm_c_re[li], ssm_c_im[li], ssm_d[li])
    nb = x_prompt.shape[0]
    lam = (jnp.exp(jnp.sum(lambda_q1[li].astype(f32) * lambda_k1[li].astype(f32)))
           - jnp.exp(jnp.sum(lambda_q2[li].astype(f32) * lambda_k2[li].astype(f32))) + LAM_INIT)
    w = dict(
        norm_g=norm_g[li].astype(f32).reshape(1, D_MODEL),
        wa=jnp.concatenate([w_xs, w_zs, w_k, _rotate_half_columns(w_k), w_za], axis=1).astype(bf16),
        wbt=jnp.concatenate([w_q, w_v], axis=1).T.astype(bf16),
        wg=wi[:, 3072:5120].astype(bf16),
        m=m, bst_f=bst_f, bst_b=bst_b, cst_f=cst_f, cst_b=cst_b,
        coef=jnp.broadcast_to(coef[:, :, :, None, :], (2, N_GROUPS, 3, nb, LANES)),
        lam=lam.reshape(1).astype(f32),
        subln_g=subln_g[li].astype(f32).reshape(1, 2 * HEAD_DIM),
        wglu=w_glu[li].astype(bf16),
        bglu=b_glu[li].astype(f32).reshape(1, D_SSM),
        wb=w_branch[li].astype(bf16),
        wout=w_out[li].astype(bf16),
        final_g=final_g.astype(f32).reshape(1, D_MODEL),
    )
    return (_trunk(x_prompt, w), _trunk(x_sample, w))
```

```python
import functools
import math

import jax
import jax.numpy as jnp
from jax import lax
from jax.experimental import pallas as pl
from jax.experimental.pallas import tpu as pltpu

D_MODEL = 1024
D_SSM = 512
SSM_GROUP = 16
N_GROUPS = 32
STATE = 64
D_ATTN = 512
N_HEADS = 4
HEAD_DIM = 64
ROPE_THETA = 10000.0
NORM_EPS = 1e-6
SUBLN_EPS = 1e-5
LAM_INIT = 0.8 - 0.6 * math.exp(-0.3 * 0)

CHUNK = 16
CHUNK_LANES = CHUNK * SSM_GROUP
LANES = 128
VMEM_LIMIT = 56 * 1024 * 1024

TOKEN_BLOCK = 1024
OUT_PROJ_CHAINS = 4
ATTN_BQ = 2048
ATTN_UNROLL = 16
ATTN_BK = 256
PROJ_TOKENS = ATTN_BK // 2
F32_ROWS = 8
ROW_PAD = 4
BF16_ROWS = 16
V_ROWS = 2 * HEAD_DIM + BF16_ROWS
SSM_SEG_CHUNKS = 16
SSM_GROUP_BLOCK = 4
SSM_OUT_ROWS = 128

f32 = jnp.float32
bf16 = jnp.bfloat16


def _params(*sem):
    return pltpu.CompilerParams(dimension_semantics=sem, vmem_limit_bytes=VMEM_LIMIT)


def _dot(a, b):
    return jnp.dot(a, b, preferred_element_type=f32)


def _dot_nt(a, b):
    return lax.dot_general(a, b, (((1,), (1,)), ((), ())), preferred_element_type=f32)


def _segment_transpose8(vs):
    slot = lax.broadcasted_iota(jnp.int32, vs[0].shape, 1) // SSM_GROUP
    for d in (4, 2, 1):
        keep = (slot & d) == 0
        new = list(vs)
        for i in range(8):
            if i & d == 0:
                a, b = vs[i], vs[i + d]
                new[i] = jnp.where(keep, a, pltpu.roll(b, d * SSM_GROUP, 1))
                new[i + d] = jnp.where(keep, pltpu.roll(a, LANES - d * SSM_GROUP, 1), b)
        vs = new
    return vs


def _in_proj_kernel(x_ref, g_ref, wa_ref, wbt_ref, ck_ref, sk_ref, cq_ref, sq_ref,
                    u_ref, zs_ref, k_ref, za_ref, qt_ref, vt_ref, xs_s, *, nb, tb):
    rows = nb * tb
    x = x_ref[...].reshape(rows, D_MODEL)
    r = lax.rsqrt(jnp.mean(x * x, axis=-1, keepdims=True) + NORM_EPS)
    h = (x * r * g_ref[...]).astype(bf16)

    for tile in range(D_SSM // LANES):
        xs = _dot(h, wa_ref[:, tile * LANES:(tile + 1) * LANES])
        for b in range(nb):
            xs_s[tile, b * (tb + ROW_PAD):b * (tb + ROW_PAD) + tb] = xs[b * tb:(b + 1) * tb]
    zs_ref[...] = _dot(h, wa_ref[:, 512:1024]).astype(bf16).reshape(nb, tb, D_SSM)
    kk = _dot(h, wa_ref[:, 1024:1536]).reshape(nb, tb, D_ATTN)
    kr = _dot(h, wa_ref[:, 1536:2048]).reshape(nb, tb, D_ATTN)
    ck = ck_ref[...]
    sk = sk_ref[...]
    for j in range(D_ATTN // LANES):
        sl = slice(j * LANES, (j + 1) * LANES)
        k_ref[:, :, sl] = (kk[:, :, sl] * ck + kr[:, :, sl] * sk).astype(bf16)
    za_ref[...] = _dot(h, wa_ref[:, 2048:2560]).astype(bf16).reshape(nb, tb, D_ATTN)

    pq = _dot_nt(wbt_ref[0:512, :], h)
    cq = jnp.tile(cq_ref[...], (1, nb))
    sq = jnp.tile(sq_ref[...], (1, nb))
    half = HEAD_DIM // 2
    for hb in range(D_ATTN // HEAD_DIM):
        x1 = pq[hb * HEAD_DIM:hb * HEAD_DIM + half]
        x2 = pq[hb * HEAD_DIM + half:(hb + 1) * HEAD_DIM]
        q1 = (x1 * cq - x2 * sq).astype(bf16)
        q2 = (x2 * cq + x1 * sq).astype(bf16)
        for b in range(nb):
            qt_ref[b, hb * HEAD_DIM:hb * HEAD_DIM + half, :] = q1[:, b * tb:(b + 1) * tb]
            qt_ref[b, hb * HEAD_DIM + half:(hb + 1) * HEAD_DIM, :] = q2[:, b * tb:(b + 1) * tb]
    pv = _dot_nt(wbt_ref[512:1024, :], h).astype(bf16)
    dv = 2 * HEAD_DIM
    ones = jnp.ones((V_ROWS - dv, tb), bf16)
    for b in range(nb):
        for hd in range(N_HEADS):
            vt_ref[b, 0, hd * V_ROWS:hd * V_ROWS + dv, :] = pv[hd * dv:(hd + 1) * dv, b * tb:(b + 1) * tb]
            vt_ref[b, 0, hd * V_ROWS + dv:(hd + 1) * V_ROWS, :] = ones

    nchunk = tb // CHUNK
    for tile in range(D_SSM // LANES):
        for hf in range(CHUNK_LANES // LANES):
            vs = []
            for t8 in range(8):
                t = 8 * hf + t8
                vs.append(jnp.concatenate(
                    [xs_s[tile, pl.ds(t + CHUNK * ch, nb, stride=tb + ROW_PAD), :] for ch in range(nchunk)],
                    axis=0))
            out = _segment_transpose8(vs)
            for g8 in range(8):
                u_ref[8 * tile + g8, :, hf * LANES:(hf + 1) * LANES] = out[g8].astype(bf16)


def _in_proj(x, norm_g, wa, wbt, ck, sk, cq, sq, *, tb):
    B, L, _ = x.shape
    nchunk = tb // CHUNK
    tok = lambda i: (0, i, 0)
    cst = lambda i: (0, 0)
    out_tok = jax.ShapeDtypeStruct((B, L, 512), bf16)
    return pl.pallas_call(
        functools.partial(_in_proj_kernel, nb=B, tb=tb),
        grid=(L // tb,),
        in_specs=[
            pl.BlockSpec((B, tb, D_MODEL), tok),
            pl.BlockSpec((1, D_MODEL), cst),
            pl.BlockSpec(wa.shape, cst),
            pl.BlockSpec(wbt.shape, cst),
            pl.BlockSpec((tb, LANES), lambda i: (i, 0)),
            pl.BlockSpec((tb, LANES), lambda i: (i, 0)),
            pl.BlockSpec((HEAD_DIM // 2, tb), lambda i: (0, i)),
            pl.BlockSpec((HEAD_DIM // 2, tb), lambda i: (0, i)),
        ],
        out_specs=[
            pl.BlockSpec((N_GROUPS, nchunk * B, CHUNK_LANES), tok),
            pl.BlockSpec((B, tb, 512), tok),
            pl.BlockSpec((B, tb, 512), tok),
            pl.BlockSpec((B, tb, 512), tok),
            pl.BlockSpec((B, 512, tb), lambda i: (0, 0, i)),
            pl.BlockSpec((B, 1, N_HEADS * V_ROWS, tb), lambda i: (0, i, 0, 0)),
        ],
        out_shape=[jax.ShapeDtypeStruct((N_GROUPS, (L // CHUNK) * B, CHUNK_LANES), bf16),
                   out_tok, out_tok, out_tok,
                   jax.ShapeDtypeStruct((B, 512, L), bf16),
                   jax.ShapeDtypeStruct((B, L // tb, N_HEADS * V_ROWS, tb), bf16)],
        scratch_shapes=[pltpu.VMEM((D_SSM // LANES, B * (tb + ROW_PAD), LANES), f32)],
        compiler_params=_params("parallel"),
        name="in_proj",
    )(x, norm_g, wa, wbt, ck, sk, cq, sq)


def _ssm_state_kernel(uf_ref, ub_ref, bf_ref, bb_ref, coef_ref, hf_ref, gb_ref, s_ref, st_ref,
                      *, nc, nb, gblk):
    @pl.when(pl.program_id(0) == 0)
    def _():
        st_ref[...] = jnp.zeros_like(st_ref)

    for g0 in range(0, N_GROUPS, gblk):
        for gi in range(gblk):
            s_ref[0, gi] = _dot(uf_ref[g0 + gi], bf_ref[g0 + gi])
            s_ref[1, gi] = _dot(ub_ref[g0 + gi], bb_ref[g0 + gi])
        gs = slice(g0, g0 + gblk)
        af1, af2, af3 = coef_ref[0, gs, 0], coef_ref[0, gs, 1], coef_ref[0, gs, 2]
        ab1, ab2, ab3 = coef_ref[1, gs, 0], coef_ref[1, gs, 1], coef_ref[1, gs, 2]

        def body(i, carry):
            hf, wf, hb, wb = carry
            rf = pl.multiple_of(i * nb, nb)
            rb = pl.multiple_of((nc - 1 - i) * nb, nb)
            hf_ref[gs, pl.ds(rf, nb), :] = hf
            gb_ref[gs, pl.ds(rb, nb), :] = hb
            sf = s_ref[0, :, pl.ds(rf, nb), :]
            sb = s_ref[1, :, pl.ds(rb, nb), :]
            hf2 = af1 * hf + af2 * wf + sf[..., :LANES]
            wf2 = af1 * wf + af3 * hf + sf[..., LANES:]
            hb2 = ab1 * hb + ab2 * wb + sb[..., :LANES]
            wb2 = ab1 * wb + ab3 * hb + sb[..., LANES:]
            return hf2, wf2, hb2, wb2

        init = (st_ref[0, 0, gs], st_ref[0, 1, gs], st_ref[1, 0, gs], st_ref[1, 1, gs])
        hf, wf, hb, wb = lax.fori_loop(0, nc, body, init, unroll=True)
        st_ref[0, 0, gs] = hf
        st_ref[0, 1, gs] = wf
        st_ref[1, 0, gs] = hb
        st_ref[1, 1, gs] = wb


def _ssm_state(u, bst_f, bst_b, coef, *, nb):
    G, rows, _ = u.shape
    nc = SSM_SEG_CHUNKS
    seg_rows = nc * nb
    nseg = rows // seg_rows
    gblk = SSM_GROUP_BLOCK
    fwd = lambda i: (0, i, 0)
    bwd = lambda i: (0, nseg - 1 - i, 0)
    cst3 = lambda i: (0, 0, 0)
    return pl.pallas_call(
        functools.partial(_ssm_state_kernel, nc=nc, nb=nb, gblk=gblk),
        grid=(nseg,),
        in_specs=[
            pl.BlockSpec((G, seg_rows, CHUNK_LANES), fwd),
            pl.BlockSpec((G, seg_rows, CHUNK_LANES), bwd),
            pl.BlockSpec(bst_f.shape, cst3),
            pl.BlockSpec(bst_b.shape, cst3),
            pl.BlockSpec(coef.shape, lambda i: (0, 0, 0, 0, 0)),
        ],
        out_specs=[
            pl.BlockSpec((G, seg_rows, LANES), fwd),
            pl.BlockSpec((G, seg_rows, LANES), bwd),
        ],
        out_shape=[jax.ShapeDtypeStruct((G, rows, LANES), f32),
                   jax.ShapeDtypeStruct((G, rows, LANES), f32)],
        scratch_shapes=[
            pltpu.VMEM((2, gblk, seg_rows, CHUNK_LANES), f32),
            pltpu.VMEM((2, 2, G, nb, LANES), f32),
        ],
        compiler_params=_params("arbitrary"),
        name="ssm_state",
    )(u, u, bst_f, bst_b, coef)


def _ssm_out_kernel(u_ref, hf_ref, gb_ref, m_ref, cf_ref, cb_ref, y_ref, nat_s, *, nb, nchunk):
    tok = nchunk * CHUNK
    for tile in range(D_SSM // LANES):
        ys = []
        for g8 in range(8):
            g = 8 * tile + g8
            y = _dot(u_ref[g], m_ref[g])
            y = y + _dot(hf_ref[g].astype(bf16), cf_ref[g])
            y = y + _dot(gb_ref[g].astype(bf16), cb_ref[g])
            ys.append(y)
        for hf in range(CHUNK_LANES // LANES):
            out = _segment_transpose8([y[:, hf * LANES:(hf + 1) * LANES] for y in ys])
            for t8 in range(8):
                t = 8 * hf + t8
                for ch in range(nchunk):
                    nat_s[tile, pl.ds(t + CHUNK * ch, nb, stride=tok + ROW_PAD), :] = out[t8][ch * nb:(ch + 1) * nb]
        for b in range(nb):
            y_ref[b, :, tile * LANES:(tile + 1) * LANES] = (
                nat_s[tile, b * (tok + ROW_PAD):b * (tok + ROW_PAD) + tok].astype(bf16))


def _ssm_out(u, hf, gb, m, cst_f, cst_b, *, nb):
    G, rows, _ = u.shape
    rb = min(SSM_OUT_ROWS, rows)
    nchunk = rb // nb
    tok = nchunk * CHUNK
    blk = lambda i: (0, i, 0)
    cst3 = lambda i: (0, 0, 0)
    return pl.pallas_call(
        functools.partial(_ssm_out_kernel, nb=nb, nchunk=nchunk),
        grid=(rows // rb,),
        in_specs=[
            pl.BlockSpec((G, rb, CHUNK_LANES), blk),
            pl.BlockSpec((G, rb, LANES), blk),
            pl.BlockSpec((G, rb, LANES), blk),
            pl.BlockSpec(m.shape, cst3),
            pl.BlockSpec(cst_f.shape, cst3),
            pl.BlockSpec(cst_b.shape, cst3),
        ],
        out_specs=pl.BlockSpec((nb, tok, D_SSM), blk),
        out_shape=jax.ShapeDtypeStruct((nb, (rows // nb) * CHUNK, D_SSM), bf16),
        scratch_shapes=[pltpu.VMEM((D_SSM // LANES, nb * (tok + ROW_PAD), LANES), f32)],
        compiler_params=_params("parallel"),
        name="ssm_out",
    )(u, hf, gb, m, cst_f, cst_b)


def _ssm_tables(a_re, a_im, log_dt, b_re, b_im, c_re, c_im, d_skip):
    T, G, P, C = CHUNK, N_GROUPS, STATE, SSM_GROUP
    hi = lax.Precision.HIGHEST
    a_re, a_im, log_dt = a_re.astype(f32), a_im.astype(f32), log_dt.astype(f32)
    dt = jnp.exp(log_dt)[..., None]
    H = T // 2
    ks = jnp.arange(-H, T + 1, dtype=f32)[:, None, None, None]
    mag = jnp.exp(ks * (a_re * dt))
    pw_re = mag * jnp.cos(ks * (a_im * dt))
    pw_im = mag * jnp.sin(ks * (a_im * dt))
    n_re, n_im = pw_re[H + 1] - 1.0, pw_im[H + 1]
    den = a_re * a_re + a_im * a_im
    co_re = (n_re * a_re + n_im * a_im) / den
    co_im = (n_im * a_re - n_re * a_im) / den
    b_re, b_im = b_re.astype(f32), b_im.astype(f32)
    bb_re = co_re[..., None] * b_re - co_im[..., None] * b_im
    bb_im = co_re[..., None] * b_im + co_im[..., None] * b_re
    c_re, c_im = c_re.astype(f32), c_im.astype(f32)

    def xtab(n, pr, pi):
        re = pr[..., None] * bb_re[n][None] - pi[..., None] * bb_im[n][None]
        im = pr[..., None] * bb_im[n][None] + pi[..., None] * bb_re[n][None]
        return (re.transpose(1, 0, 3, 2).reshape(G, T * C, P), im.transpose(1, 0, 3, 2).reshape(G, T * C, P))

    def ytab(n, pr, pi):
        re = c_re[n][None] * pr[:, :, None, :] - c_im[n][None] * pi[:, :, None, :]
        im = c_re[n][None] * pi[:, :, None, :] + c_im[n][None] * pr[:, :, None, :]
        return (re.transpose(1, 3, 0, 2).reshape(G, P, T * C), im.transpose(1, 3, 0, 2).reshape(G, P, T * C))

    def lag_kernels(x, y):
        return (jnp.einsum('gjp,gpt->gjt', x[0], y[0], precision=hi)
                - jnp.einsum('gjp,gpt->gjt', x[1], y[1], precision=hi))

    def powers(n, lo, reverse=False):
        pr, pi = pw_re[lo + H:lo + H + T, n], pw_im[lo + H:lo + H + T, n]
        return (pr[::-1], pi[::-1]) if reverse else (pr, pi)

    kf = lag_kernels(xtab(0, *powers(0, -H, reverse=True)), ytab(0, *powers(0, 1 - H)))
    kb = lag_kernels(xtab(1, *powers(1, -H)), ytab(1, *powers(1, 1 - H, reverse=True)))
    step = jnp.arange(T * C) // C
    causal = step[None, :] >= step[:, None]
    anti = step[None, :] <= step[:, None]
    d_diag = jnp.tile(d_skip.astype(f32).reshape(G, 1, C), (1, T, 1)).reshape(G, 1, T * C)
    m = (jnp.where(causal[None], kf, 0.0) + jnp.where(anti[None], kb, 0.0)
         + jnp.eye(T * C, dtype=f32)[None] * d_diag)

    xf = xtab(0, *powers(0, 0, reverse=True))
    xb = xtab(1, *powers(1, 0))
    yf = ytab(0, *powers(0, 1))
    yb = ytab(1, *powers(1, 1, reverse=True))

    bst_f = jnp.concatenate([xf[0], xf[1], xf[1], xf[0]], axis=-1)
    bst_b = jnp.concatenate([xb[0], xb[1], xb[1], xb[0]], axis=-1)
    cst_f = jnp.concatenate([yf[0], -yf[1]], axis=1)
    cst_b = jnp.concatenate([yb[0], -yb[1]], axis=1)
    ar, ai = pw_re[H + T], pw_im[H + T]
    coef = jnp.stack([jnp.concatenate([ar, ar], -1),
                      jnp.concatenate([-ai, ai], -1),
                      jnp.concatenate([ai, -ai], -1)], axis=2)
    return m.astype(bf16), bst_f.astype(bf16), bst_b.astype(bf16), cst_f.astype(bf16), cst_b.astype(bf16), coef


def _attn_kernel(lam_ref, qt_ref, k_ref, vt_ref, g_ref, za_ref, o_ref,
                 qbd_ref, acc_ref, p_ref, m_ref, a_ref, *, nkv, bq, bk):
    lam = lam_ref[0]
    half = HEAD_DIM
    ns = bq // LANES
    dv = 2 * HEAD_DIM
    zero = jnp.zeros((half, LANES), bf16)
    for st in range(ns):
        qs = qt_ref[0, :, st * LANES:(st + 1) * LANES]
        qbd_ref[st, 0:half, 0:LANES] = qs[0:half]
        qbd_ref[st, 0:half, LANES:2 * LANES] = zero
        qbd_ref[st, half:2 * half, 0:LANES] = zero
        qbd_ref[st, half:2 * half, LANES:2 * LANES] = qs[half:2 * half]
    acc_ref[...] = jnp.zeros_like(acc_ref)
    p_ref[...] = jnp.zeros_like(p_ref)
    a_ref[...] = jnp.ones_like(a_ref)
    m_ref[...] = jnp.full(m_ref.shape, -jnp.inf, f32)

    def values(j):
        return jnp.concatenate([vt_ref[0, 2 * j], vt_ref[0, 2 * j + 1]], axis=1)

    nsub = F32_ROWS

    def fold(st, vb):
        acc = acc_ref[st].reshape(V_ROWS // nsub, nsub, 2 * LANES) * a_ref[st][None]
        acc_ref[st] = acc.reshape(V_ROWS, 2 * LANES) + _dot(vb, p_ref[st])

    def colmax(s):
        mx = jnp.max(s.reshape(-1, nsub, 2 * LANES), axis=0)
        for sh in (4, 2, 1):
            mx = jnp.maximum(mx, pltpu.roll(mx, sh, 0))
        return mx

    def probs(s, m):
        x = s.reshape(-1, nsub, 2 * LANES) - m[None]
        return jnp.exp2(x.reshape(s.shape).astype(bf16))

    def body(j, carry):
        off = pl.multiple_of(j * bk, bk)
        ka = k_ref[0, pl.ds(off, bk // 2), :]
        kb = k_ref[0, pl.ds(off + bk // 2, bk // 2), :]
        vb = values(jnp.maximum(j - 1, 0))
        for st in range(ns):
            fold(st, vb)
            m_old = m_ref[st]
            s_a = _dot(ka, qbd_ref[st])
            m_a = jnp.maximum(m_old, colmax(s_a))
            p_a = probs(s_a, m_a)
            s_b = _dot(kb, qbd_ref[st])
            m_b = jnp.maximum(m_a, colmax(s_b))
            p_b = probs(s_b, m_b)
            corr = jnp.exp2(m_a - m_b)
            corr = jnp.concatenate([corr, corr], axis=0).astype(bf16)
            p_a = p_a.reshape(-1, BF16_ROWS, 2 * LANES) * corr[None]
            p_ref[st, 0:bk // 2] = p_a.reshape(bk // 2, 2 * LANES)
            p_ref[st, bk // 2:bk] = p_b
            a_ref[st] = jnp.exp2(m_old - m_b)
            m_ref[st] = m_b
        return carry

    lax.fori_loop(0, nkv, body, 0, unroll=min(ATTN_UNROLL, nkv))

    vb_last = values(nkv - 1)
    for st in range(ns):
        fold(st, vb_last)
        acc = acc_ref[st, 0:dv]
        inv = 1.0 / acc_ref[st, dv:dv + 1]
        ot = acc[:, :LANES] * inv[:, :LANES] - lam * (acc[:, LANES:] * inv[:, LANES:])
        o = ot.T
        y = o * lax.rsqrt(jnp.mean(o * o, axis=-1, keepdims=True) + SUBLN_EPS) * g_ref[...] * (1.0 - LAM_INIT)
        za = za_ref[0, st * LANES:(st + 1) * LANES, :].astype(f32)
        o_ref[0, st * LANES:(st + 1) * LANES, :] = (y * (za * jax.nn.sigmoid(za))).astype(bf16)


def _attention(lam, qt, k, vt, subln_g, za, *, bq, bk):
    B, L, _ = k.shape
    nkv = L // bk
    nvt = vt.shape[1]
    return pl.pallas_call(
        functools.partial(_attn_kernel, nkv=nkv, bq=bq, bk=bk),
        grid=(B, N_HEADS, L // bq),
        in_specs=[
            pl.BlockSpec(memory_space=pltpu.SMEM),
            pl.BlockSpec((1, 2 * HEAD_DIM, bq), lambda b, h, i: (b, h, i)),
            pl.BlockSpec((1, L, 2 * HEAD_DIM), lambda b, h, i: (b, 0, h)),
            pl.BlockSpec((1, nvt, V_ROWS, bk // 2), lambda b, h, i: (b, 0, h, 0)),
            pl.BlockSpec((1, 2 * HEAD_DIM), lambda b, h, i: (0, 0)),
            pl.BlockSpec((1, bq, 2 * HEAD_DIM), lambda b, h, i: (b, i, h)),
        ],
        out_specs=pl.BlockSpec((1, bq, 2 * HEAD_DIM), lambda b, h, i: (b, i, h)),
        out_shape=jax.ShapeDtypeStruct((B, L, D_ATTN), bf16),
        scratch_shapes=[
            pltpu.VMEM((bq // LANES, 2 * HEAD_DIM, 2 * LANES), bf16),
            pltpu.VMEM((bq // LANES, V_ROWS, 2 * LANES), f32),
            pltpu.VMEM((bq // LANES, bk, 2 * LANES), bf16),
            pltpu.VMEM((bq // LANES, F32_ROWS, 2 * LANES), f32),
            pltpu.VMEM((bq // LANES, F32_ROWS, 2 * LANES), f32),
        ],
        compiler_params=_params("parallel", "parallel", "arbitrary"),
        name="attention",
    )(lam, qt, k, vt, subln_g, za)


def _out_proj_kernel(x_ref, yssm_ref, zs_ref, ya_ref, g_ref, wg_ref, wglu_ref, bglu_ref,
                     wb_ref, wout_ref, fg_ref, o_ref):
    tm = x_ref.shape[1]
    hm = tm // OUT_PROJ_CHAINS
    for c in range(OUT_PROJ_CHAINS):
        rs = slice(c * hm, (c + 1) * hm)
        x = x_ref[0, rs]
        r = lax.rsqrt(jnp.mean(x * x, axis=-1, keepdims=True) + NORM_EPS)
        h = (x * r * g_ref[...]).astype(bf16)

        ys = jax.nn.gelu(yssm_ref[0, rs].astype(f32))
        ys = ys * jax.nn.sigmoid(_dot(ys.astype(bf16), wglu_ref[...]) + bglu_ref[...])
        zs = zs_ref[0, rs].astype(f32)
        ys = ys * (zs * jax.nn.sigmoid(zs))

        ps = _dot(ys.astype(bf16), wb_ref[0])
        merged = jax.nn.sigmoid(_dot(h, wg_ref[:, 0:D_MODEL])) * ps
        pa = _dot(ya_ref[0, rs], wb_ref[1])
        merged = merged + jax.nn.sigmoid(_dot(h, wg_ref[:, D_MODEL:2 * D_MODEL])) * pa
        out = x + _dot(merged.astype(bf16), wout_ref[...])
        o_ref[0, rs] = out * lax.rsqrt(jnp.mean(out * out, axis=-1, keepdims=True) + NORM_EPS) * fg_ref[...]


def _out_proj(x, yssm, zs, ya, norm_g, wg, wglu, bglu, wb, wout, final_g, *, tm):
    B, L, _ = x.shape
    tok = lambda b, i: (b, i, 0)
    cst = lambda b, i: (0, 0)
    return pl.pallas_call(
        _out_proj_kernel,
        grid=(B, L // tm),
        in_specs=[
            pl.BlockSpec((1, tm, D_MODEL), tok),
            pl.BlockSpec((1, tm, 512), tok),
            pl.BlockSpec((1, tm, 512), tok),
            pl.BlockSpec((1, tm, 512), tok),
            pl.BlockSpec((1, D_MODEL), cst),
            pl.BlockSpec(wg.shape, cst),
            pl.BlockSpec(wglu.shape, cst),
            pl.BlockSpec((1, D_SSM), cst),
            pl.BlockSpec(wb.shape, lambda b, i: (0, 0, 0)),
            pl.BlockSpec(wout.shape, cst),
            pl.BlockSpec((1, D_MODEL), cst),
        ],
        out_specs=pl.BlockSpec((1, tm, D_MODEL), tok),
        out_shape=jax.ShapeDtypeStruct((B, L, D_MODEL), x.dtype),
        compiler_params=_params("parallel", "parallel"),
        name="out_proj",
    )(x, yssm, zs, ya, norm_g, wg, wglu, bglu, wb, wout, final_g)


def _rotary_tables(L):
    half = HEAD_DIM // 2
    inv_freq = 1.0 / (ROPE_THETA ** (jnp.arange(0, half, dtype=f32) * 2.0 / HEAD_DIM))
    ang = jnp.arange(L, dtype=f32)[:, None] * inv_freq[None, :]
    cos, sin = jnp.cos(ang), jnp.sin(ang)
    ck = jnp.tile(cos, (1, LANES // half))
    sk = jnp.tile(sin, (1, LANES // half))
    scale = math.log2(math.e) / math.sqrt(HEAD_DIM)
    return ck, sk, cos.T * scale, sin.T * scale


def _trunk(x, w):
    B, L, _ = x.shape
    assert B == 8 and L % ATTN_BQ == 0, "scan state vregs hold one row per batch element"
    ck, sk, cq, sq = _rotary_tables(L)
    u, zs, k, za, qt, vt = _in_proj(x, w["norm_g"], w["wa"], w["wbt"], ck, sk, cq, sq, tb=PROJ_TOKENS)
    hf, gb = _ssm_state(u, w["bst_f"], w["bst_b"], w["coef"], nb=B)
    yssm = _ssm_out(u, hf, gb, w["m"], w["cst_f"], w["cst_b"], nb=B)
    ya = _attention(w["lam"], qt, k, vt, w["subln_g"], za, bq=ATTN_BQ, bk=ATTN_BK)
    return _out_proj(x, yssm, zs, ya, w["norm_g"], w["wg"], w["wglu"], w["bglu"], w["wb"], w["wout"],
                     w["final_g"], tm=TOKEN_BLOCK)


def _rotate_half_columns(wk):
    d = wk.shape[0]
    w4 = wk.reshape(d, D_ATTN // HEAD_DIM, 2, HEAD_DIM // 2)
    return jnp.stack([-w4[:, :, 1], w4[:, :, 0]], axis=2).reshape(d, D_ATTN)


def kernel(x_prompt, x_sample, norm_g, w_in, ssm_a_re, ssm_a_im, ssm_log_dt, ssm_b_re, ssm_b_im, ssm_c_re, ssm_c_im, ssm_d, w_glu, b_glu, lambda_q1, lambda_k1, lambda_q2, lambda_k2, subln_g, w_branch, w_out, final_g):
    li = 0
    wi = w_in[li].astype(f32)
    w_xs, w_zs = wi[:, 0:512], wi[:, 512:1024]
    w_q, w_k, w_v, w_za = wi[:, 1024:1536], wi[:, 1536:2048], wi[:, 2048:2560], wi[:, 2560:3072]
    m, bst_f, bst_b, cst_f, cst_b, coef = _ssm_tables(
        ssm_a_re[li], ssm_a_im[li], ssm_log_dt[li], ssm_b_re[li], ssm_b_im[li],
        ssm_c_re[li], ssm_c_im[li], ssm_d[li])
    nb = x_prompt.shape[0]
    lam = (jnp.exp(jnp.sum(lambda_q1[li].astype(f32) * lambda_k1[li].astype(f32)))
           - jnp.exp(jnp.sum(lambda_q2[li].astype(f32) * lambda_k2[li].astype(f32))) + LAM_INIT)
    w = dict(
        norm_g=norm_g[li].astype(f32).reshape(1, D_MODEL),
        wa=jnp.concatenate([w_xs, w_zs, w_k, _rotate_half_columns(w_k), w_za], axis=1).astype(bf16),
        wbt=jnp.concatenate([w_q, w_v], axis=1).T.astype(bf16),
        wg=wi[:, 3072:5120].astype(bf16),
        m=m, bst_f=bst_f, bst_b=bst_b, cst_f=cst_f, cst_b=cst_b,
        coef=jnp.broadcast_to(coef[:, :, :, None, :], (2, N_GROUPS, 3, nb, LANES)),
        lam=lam.reshape(1).astype(f32),
        subln_g=subln_g[li].astype(f32).reshape(1, 2 * HEAD_DIM),
        wglu=w_glu[li].astype(bf16),
        bglu=b_glu[li].astype(f32).reshape(1, D_SSM),
        wb=w_branch[li].astype(bf16),
        wout=w_out[li].astype(bf16),
        final_g=final_g.astype(f32).reshape(1, D_MODEL),
    )
    return (_trunk(x_prompt, w), _trunk(x_sample, w))
```

```python
import functools
import math

import jax
import jax.numpy as jnp
from jax import lax
from jax.experimental import pallas as pl
from jax.experimental.pallas import tpu as pltpu

D_MODEL = 1024
D_SSM = 512
SSM_GROUP = 16
N_GROUPS = 32
STATE = 64
D_ATTN = 512
N_HEADS = 4
HEAD_DIM = 64
ROPE_THETA = 10000.0
NORM_EPS = 1e-6
SUBLN_EPS = 1e-5
LAM_INIT = 0.8 - 0.6 * math.exp(-0.3 * 0)

CHUNK = 16
CHUNK_LANES = CHUNK * SSM_GROUP
LANES = 128
VMEM_LIMIT = 56 * 1024 * 1024

TOKEN_BLOCK = 1024
OUT_PROJ_CHAINS = 4
ATTN_BQ = 2048
ATTN_UNROLL = 16
ATTN_BK = 256
PROJ_TOKENS = ATTN_BK // 2
F32_ROWS = 8
ROW_PAD = 4
BF16_ROWS = 16
V_ROWS = 2 * HEAD_DIM + BF16_ROWS
SSM_SEG_CHUNKS = 16
SSM_GROUP_BLOCK = 4
SSM_OUT_ROWS = 128

f32 = jnp.float32
bf16 = jnp.bfloat16


def _params(*sem):
    return pltpu.CompilerParams(dimension_semantics=sem, vmem_limit_bytes=VMEM_LIMIT)


def _dot(a, b):
    return jnp.dot(a, b, preferred_element_type=f32)


def _dot_nt(a, b):
    return lax.dot_general(a, b, (((1,), (1,)), ((), ())), preferred_element_type=f32)


def _segment_transpose8(vs):
    slot = lax.broadcasted_iota(jnp.int32, vs[0].shape, 1) // SSM_GROUP
    for d in (4, 2, 1):
        keep = (slot & d) == 0
        new = list(vs)
        for i in range(8):
            if i & d == 0:
                a, b = vs[i], vs[i + d]
                new[i] = jnp.where(keep, a, pltpu.roll(b, d * SSM_GROUP, 1))
                new[i + d] = jnp.where(keep, pltpu.roll(a, LANES - d * SSM_GROUP, 1), b)
        vs = new
    return vs


def _in_proj_kernel(x_ref, g_ref, wa_ref, wbt_ref, ck_ref, sk_ref, cq_ref, sq_ref,
                    u_ref, zs_ref, k_ref, za_ref, qt_ref, vt_ref, xs_s, *, nb, tb):
    rows = nb * tb
    x = x_ref[...].reshape(rows, D_MODEL)
    r = lax.rsqrt(jnp.mean(x * x, axis=-1, keepdims=True) + NORM_EPS)
    h = (x * r * g_ref[...]).astype(bf16)

    xs = _dot(h, wa_ref[:, 0:D_SSM])
    for tile in range(D_SSM // LANES):
        for b in range(nb):
            xs_s[tile, b * (tb + ROW_PAD):b * (tb + ROW_PAD) + tb] = xs[b * tb:(b + 1) * tb, tile * LANES:(tile + 1) * LANES]
    zs_ref[...] = _dot(h, wa_ref[:, 512:1024]).astype(bf16).reshape(nb, tb, D_SSM)
    kk = _dot(h, wa_ref[:, 1024:1536]).reshape(nb, tb, D_ATTN)
    kr = _dot(h, wa_ref[:, 1536:2048]).reshape(nb, tb, D_ATTN)
    ck = ck_ref[...]
    sk = sk_ref[...]
    for j in range(D_ATTN // LANES):
        sl = slice(j * LANES, (j + 1) * LANES)
        k_ref[:, :, sl] = (kk[:, :, sl] * ck + kr[:, :, sl] * sk).astype(bf16)
    za_ref[...] = _dot(h, wa_ref[:, 2048:2560]).astype(bf16).reshape(nb, tb, D_ATTN)

    pq = _dot_nt(wbt_ref[0:512, :], h)
    cq = jnp.tile(cq_ref[...], (1, nb))
    sq = jnp.tile(sq_ref[...], (1, nb))
    half = HEAD_DIM // 2
    for hb in range(D_ATTN // HEAD_DIM):
        x1 = pq[hb * HEAD_DIM:hb * HEAD_DIM + half]
        x2 = pq[hb * HEAD_DIM + half:(hb + 1) * HEAD_DIM]
        q1 = (x1 * cq - x2 * sq).astype(bf16)
        q2 = (x2 * cq + x1 * sq).astype(bf16)
        for b in range(nb):
            qt_ref[b, hb * HEAD_DIM:hb * HEAD_DIM + half, :] = q1[:, b * tb:(b + 1) * tb]
            qt_ref[b, hb * HEAD_DIM + half:(hb + 1) * HEAD_DIM, :] = q2[:, b * tb:(b + 1) * tb]
    pv = _dot_nt(wbt_ref[512:1024, :], h).astype(bf16)
    dv = 2 * HEAD_DIM
    ones = jnp.ones((V_ROWS - dv, tb), bf16)
    for b in range(nb):
        for hd in range(N_HEADS):
            vt_ref[b, 0, hd * V_ROWS:hd * V_ROWS + dv, :] = pv[hd * dv:(hd + 1) * dv, b * tb:(b + 1) * tb]
            vt_ref[b, 0, hd * V_ROWS + dv:(hd + 1) * V_ROWS, :] = ones

    nchunk = tb // CHUNK
    for tile in range(D_SSM // LANES):
        for hf in range(CHUNK_LANES // LANES):
            vs = []
            for t8 in range(8):
                t = 8 * hf + t8
                vs.append(jnp.concatenate(
                    [xs_s[tile, pl.ds(t + CHUNK * ch, nb, stride=tb + ROW_PAD), :] for ch in range(nchunk)],
                    axis=0))
            out = _segment_transpose8(vs)
            for g8 in range(8):
                u_ref[8 * tile + g8, :, hf * LANES:(hf + 1) * LANES] = out[g8].astype(bf16)


def _in_proj(x, norm_g, wa, wbt, ck, sk, cq, sq, *, tb):
    B, L, _ = x.shape
    nchunk = tb // CHUNK
    tok = lambda i: (0, i, 0)
    cst = lambda i: (0, 0)
    out_tok = jax.ShapeDtypeStruct((B, L, 512), bf16)
    return pl.pallas_call(
        functools.partial(_in_proj_kernel, nb=B, tb=tb),
        grid=(L // tb,),
        in_specs=[
            pl.BlockSpec((B, tb, D_MODEL), tok),
            pl.BlockSpec((1, D_MODEL), cst),
            pl.BlockSpec(wa.shape, cst),
            pl.BlockSpec(wbt.shape, cst),
            pl.BlockSpec((tb, LANES), lambda i: (i, 0)),
            pl.BlockSpec((tb, LANES), lambda i: (i, 0)),
            pl.BlockSpec((HEAD_DIM // 2, tb), lambda i: (0, i)),
            pl.BlockSpec((HEAD_DIM // 2, tb), lambda i: (0, i)),
        ],
        out_specs=[
            pl.BlockSpec((N_GROUPS, nchunk * B, CHUNK_LANES), tok),
            pl.BlockSpec((B, tb, 512), tok),
            pl.BlockSpec((B, tb, 512), tok),
            pl.BlockSpec((B, tb, 512), tok),
            pl.BlockSpec((B, 512, tb), lambda i: (0, 0, i)),
            pl.BlockSpec((B, 1, N_HEADS * V_ROWS, tb), lambda i: (0, i, 0, 0)),
        ],
        out_shape=[jax.ShapeDtypeStruct((N_GROUPS, (L // CHUNK) * B, CHUNK_LANES), bf16),
                   out_tok, out_tok, out_tok,
                   jax.ShapeDtypeStruct((B, 512, L), bf16),
                   jax.ShapeDtypeStruct((B, L // tb, N_HEADS * V_ROWS, tb), bf16)],
        scratch_shapes=[pltpu.VMEM((D_SSM // LANES, B * (tb + ROW_PAD), LANES), f32)],
        compiler_params=_params("parallel"),
        name="in_proj",
    )(x, norm_g, wa, wbt, ck, sk, cq, sq)


def _ssm_state_kernel(uf_ref, ub_ref, bf_ref, bb_ref, coef_ref, hf_ref, gb_ref, s_ref, st_ref,
                      *, nc, nb, gblk):
    @pl.when(pl.program_id(0) == 0)
    def _():
        st_ref[...] = jnp.zeros_like(st_ref)

    for g0 in range(0, N_GROUPS, gblk):
        for gi in range(gblk):
            s_ref[0, gi] = _dot(uf_ref[g0 + gi], bf_ref[g0 + gi])
            s_ref[1, gi] = _dot(ub_ref[g0 + gi], bb_ref[g0 + gi])
        gs = slice(g0, g0 + gblk)
        af1, af2, af3 = coef_ref[0, gs, 0], coef_ref[0, gs, 1], coef_ref[0, gs, 2]
        ab1, ab2, ab3 = coef_ref[1, gs, 0], coef_ref[1, gs, 1], coef_ref[1, gs, 2]

        def body(i, carry):
            hf, wf, hb, wb = carry
            rf = pl.multiple_of(i * nb, nb)
            rb = pl.multiple_of((nc - 1 - i) * nb, nb)
            hf_ref[gs, pl.ds(rf, nb), :] = hf
            gb_ref[gs, pl.ds(rb, nb), :] = hb
            sf = s_ref[0, :, pl.ds(rf, nb), :]
            sb = s_ref[1, :, pl.ds(rb, nb), :]
            hf2 = af1 * hf + af2 * wf + sf[..., :LANES]
            wf2 = af1 * wf + af3 * hf + sf[..., LANES:]
            hb2 = ab1 * hb + ab2 * wb + sb[..., :LANES]
            wb2 = ab1 * wb + ab3 * hb + sb[..., LANES:]
            return hf2, wf2, hb2, wb2

        init = (st_ref[0, 0, gs], st_ref[0, 1, gs], st_ref[1, 0, gs], st_ref[1, 1, gs])
        hf, wf, hb, wb = lax.fori_loop(0, nc, body, init, unroll=True)
        st_ref[0, 0, gs] = hf
        st_ref[0, 1, gs] = wf
        st_ref[1, 0, gs] = hb
        st_ref[1, 1, gs] = wb


def _ssm_state(u, bst_f, bst_b, coef, *, nb):
    G, rows, _ = u.shape
    nc = SSM_SEG_CHUNKS
    seg_rows = nc * nb
    nseg = rows // seg_rows
    gblk = SSM_GROUP_BLOCK
    fwd = lambda i: (0, i, 0)
    bwd = lambda i: (0, nseg - 1 - i, 0)
    cst3 = lambda i: (0, 0, 0)
    return pl.pallas_call(
        functools.partial(_ssm_state_kernel, nc=nc, nb=nb, gblk=gblk),
        grid=(nseg,),
        in_specs=[
            pl.BlockSpec((G, seg_rows, CHUNK_LANES), fwd),
            pl.BlockSpec((G, seg_rows, CHUNK_LANES), bwd),
            pl.BlockSpec(bst_f.shape, cst3),
            pl.BlockSpec(bst_b.shape, cst3),
            pl.BlockSpec(coef.shape, lambda i: (0, 0, 0, 0, 0)),
        ],
        out_specs=[
            pl.BlockSpec((G, seg_rows, LANES), fwd),
            pl.BlockSpec((G, seg_rows, LANES), bwd),
        ],
        out_shape=[jax.ShapeDtypeStruct((G, rows, LANES), f32),
                   jax.ShapeDtypeStruct((G, rows, LANES), f32)],
        scratch_shapes=[
            pltpu.VMEM((2, gblk, seg_rows, CHUNK_LANES), f32),
            pltpu.VMEM((2, 2, G, nb, LANES), f32),
        ],
        compiler_params=_params("arbitrary"),
        name="ssm_state",
    )(u, u, bst_f, bst_b, coef)


def _ssm_out_kernel(u_ref, hf_ref, gb_ref, m_ref, cf_ref, cb_ref, y_ref, nat_s, *, nb, nchunk):
    tok = nchunk * CHUNK
    for tile in range(D_SSM // LANES):
        ys = []
        for g8 in range(8):
            g = 8 * tile + g8
            y = _dot(u_ref[g], m_ref[g])
            y = y + _dot(hf_ref[g].astype(bf16), cf_ref[g])
            y = y + _dot(gb_ref[g].astype(bf16), cb_ref[g])
            ys.append(y)
        for hf in range(CHUNK_LANES // LANES):
            out = _segment_transpose8([y[:, hf * LANES:(hf + 1) * LANES] for y in ys])
            for t8 in range(8):
                t = 8 * hf + t8
                for ch in range(nchunk):
                    nat_s[tile, pl.ds(t + CHUNK * ch, nb, stride=tok + ROW_PAD), :] = out[t8][ch * nb:(ch + 1) * nb]
        for b in range(nb):
            y_ref[b, :, tile * LANES:(tile + 1) * LANES] = (
                nat_s[tile, b * (tok + ROW_PAD):b * (tok + ROW_PAD) + tok].astype(bf16))


def _ssm_out(u, hf, gb, m, cst_f, cst_b, *, nb):
    G, rows, _ = u.shape
    rb = min(SSM_OUT_ROWS, rows)
    nchunk = rb // nb
    tok = nchunk * CHUNK
    blk = lambda i: (0, i, 0)
    cst3 = lambda i: (0, 0, 0)
    return pl.pallas_call(
        functools.partial(_ssm_out_kernel, nb=nb, nchunk=nchunk),
        grid=(rows // rb,),
        in_specs=[
            pl.BlockSpec((G, rb, CHUNK_LANES), blk),
            pl.BlockSpec((G, rb, LANES), blk),
            pl.BlockSpec((G, rb, LANES), blk),
            pl.BlockSpec(m.shape, cst3),
            pl.BlockSpec(cst_f.shape, cst3),
            pl.BlockSpec(cst_b.shape, cst3),
        ],
        out_specs=pl.BlockSpec((nb, tok, D_SSM), blk),
        out_shape=jax.ShapeDtypeStruct((nb, (rows // nb) * CHUNK, D_SSM), bf16),
        scratch_shapes=[pltpu.VMEM((D_SSM // LANES, nb * (tok + ROW_PAD), LANES), f32)],
        compiler_params=_params("parallel"),
        name="ssm_out",
    )(u, hf, gb, m, cst_f, cst_b)


def _ssm_tables(a_re, a_im, log_dt, b_re, b_im, c_re, c_im, d_skip):
    T, G, P, C = CHUNK, N_GROUPS, STATE, SSM_GROUP
    hi = lax.Precision.HIGHEST
    a_re, a_im, log_dt = a_re.astype(f32), a_im.astype(f32), log_dt.astype(f32)
    dt = jnp.exp(log_dt)[..., None]
    H = T // 2
    ks = jnp.arange(-H, T + 1, dtype=f32)[:, None, None, None]
    mag = jnp.exp(ks * (a_re * dt))
    pw_re = mag * jnp.cos(ks * (a_im * dt))
    pw_im = mag * jnp.sin(ks * (a_im * dt))
    n_re, n_im = pw_re[H + 1] - 1.0, pw_im[H + 1]
    den = a_re * a_re + a_im * a_im
    co_re = (n_re * a_re + n_im * a_im) / den
    co_im = (n_im * a_re - n_re * a_im) / den
    b_re, b_im = b_re.astype(f32), b_im.astype(f32)
    bb_re = co_re[..., None] * b_re - co_im[..., None] * b_im
    bb_im = co_re[..., None] * b_im + co_im[..., None] * b_re
    c_re, c_im = c_re.astype(f32), c_im.astype(f32)

    def xtab(n, pr, pi):
        re = pr[..., None] * bb_re[n][None] - pi[..., None] * bb_im[n][None]
        im = pr[..., None] * bb_im[n][None] + pi[..., None] * bb_re[n][None]
        return (re.transpose(1, 0, 3, 2).reshape(G, T * C, P), im.transpose(1, 0, 3, 2).reshape(G, T * C, P))

    def ytab(n, pr, pi):
        re = c_re[n][None] * pr[:, :, None, :] - c_im[n][None] * pi[:, :, None, :]
        im = c_re[n][None] * pi[:, :, None, :] + c_im[n][None] * pr[:, :, None, :]
        return (re.transpose(1, 3, 0, 2).reshape(G, P, T * C), im.transpose(1, 3, 0, 2).reshape(G, P, T * C))

    def lag_kernels(x, y):
        return (jnp.einsum('gjp,gpt->gjt', x[0], y[0], precision=hi)
                - jnp.einsum('gjp,gpt->gjt', x[1], y[1], precision=hi))

    def powers(n, lo, reverse=False):
        pr, pi = pw_re[lo + H:lo + H + T, n], pw_im[lo + H:lo + H + T, n]
        return (pr[::-1], pi[::-1]) if reverse else (pr, pi)

    kf = lag_kernels(xtab(0, *powers(0, -H, reverse=True)), ytab(0, *powers(0, 1 - H)))
    kb = lag_kernels(xtab(1, *powers(1, -H)), ytab(1, *powers(1, 1 - H, reverse=True)))
    step = jnp.arange(T * C) // C
    causal = step[None, :] >= step[:, None]
    anti = step[None, :] <= step[:, None]
    d_diag = jnp.tile(d_skip.astype(f32).reshape(G, 1, C), (1, T, 1)).reshape(G, 1, T * C)
    m = (jnp.where(causal[None], kf, 0.0) + jnp.where(anti[None], kb, 0.0)
         + jnp.eye(T * C, dtype=f32)[None] * d_diag)

    xf = xtab(0, *powers(0, 0, reverse=True))
    xb = xtab(1, *powers(1, 0))
    yf = ytab(0, *powers(0, 1))
    yb = ytab(1, *powers(1, 1, reverse=True))

    bst_f = jnp.concatenate([xf[0], xf[1], xf[1], xf[0]], axis=-1)
    bst_b = jnp.concatenate([xb[0], xb[1], xb[1], xb[0]], axis=-1)
    cst_f = jnp.concatenate([yf[0], -yf[1]], axis=1)
    cst_b = jnp.concatenate([yb[0], -yb[1]], axis=1)
    ar, ai = pw_re[H + T], pw_im[H + T]
    coef = jnp.stack([jnp.concatenate([ar, ar], -1),
                      jnp.concatenate([-ai, ai], -1),
                      jnp.concatenate([ai, -ai], -1)], axis=2)
    return m.astype(bf16), bst_f.astype(bf16), bst_b.astype(bf16), cst_f.astype(bf16), cst_b.astype(bf16), coef


def _attn_kernel(lam_ref, qt_ref, k_ref, vt_ref, g_ref, za_ref, o_ref,
                 qbd_ref, acc_ref, p_ref, m_ref, a_ref, *, nkv, bq, bk):
    lam = lam_ref[0]
    half = HEAD_DIM
    ns = bq // LANES
    dv = 2 * HEAD_DIM
    zero = jnp.zeros((half, LANES), bf16)
    for st in range(ns):
        qs = qt_ref[0, :, st * LANES:(st + 1) * LANES]
        qbd_ref[st, 0:half, 0:LANES] = qs[0:half]
        qbd_ref[st, 0:half, LANES:2 * LANES] = zero
        qbd_ref[st, half:2 * half, 0:LANES] = zero
        qbd_ref[st, half:2 * half, LANES:2 * LANES] = qs[half:2 * half]
    acc_ref[...] = jnp.zeros_like(acc_ref)
    p_ref[...] = jnp.zeros_like(p_ref)
    a_ref[...] = jnp.ones_like(a_ref)
    m_ref[...] = jnp.full(m_ref.shape, -jnp.inf, f32)

    def values(j):
        return jnp.concatenate([vt_ref[0, 2 * j], vt_ref[0, 2 * j + 1]], axis=1)

    nsub = F32_ROWS

    def fold(st, vb):
        acc = acc_ref[st].reshape(V_ROWS // nsub, nsub, 2 * LANES) * a_ref[st][None]
        acc_ref[st] = acc.reshape(V_ROWS, 2 * LANES) + _dot(vb, p_ref[st])

    def colmax(s):
        mx = jnp.max(s.reshape(-1, nsub, 2 * LANES), axis=0)
        for sh in (4, 2, 1):
            mx = jnp.maximum(mx, pltpu.roll(mx, sh, 0))
        return mx

    def probs(s, m):
        x = s.reshape(-1, nsub, 2 * LANES) - m[None]
        return jnp.exp2(x.reshape(s.shape).astype(bf16))

    def body(j, carry):
        off = pl.multiple_of(j * bk, bk)
        ka = k_ref[0, pl.ds(off, bk // 2), :]
        kb = k_ref[0, pl.ds(off + bk // 2, bk // 2), :]
        vb = values(jnp.maximum(j - 1, 0))
        for st in range(ns):
            fold(st, vb)
            m_old = m_ref[st]
            s_a = _dot(ka, qbd_ref[st])
            m_a = jnp.maximum(m_old, colmax(s_a))
            p_a = probs(s_a, m_a)
            s_b = _dot(kb, qbd_ref[st])
            m_b = jnp.maximum(m_a, colmax(s_b))
            p_b = probs(s_b, m_b)
            corr = jnp.exp2(m_a - m_b)
            corr = jnp.concatenate([corr, corr], axis=0).astype(bf16)
            p_a = p_a.reshape(-1, BF16_ROWS, 2 * LANES) * corr[None]
            p_ref[st, 0:bk // 2] = p_a.reshape(bk // 2, 2 * LANES)
            p_ref[st, bk // 2:bk] = p_b
            a_ref[st] = jnp.exp2(m_old - m_b)
            m_ref[st] = m_b
        return carry

    lax.fori_loop(0, nkv, body, 0, unroll=min(ATTN_UNROLL, nkv))

    vb_last = values(nkv - 1)
    for st in range(ns):
        fold(st, vb_last)
        acc = acc_ref[st, 0:dv]
        inv = 1.0 / acc_ref[st, dv:dv + 1]
        ot = acc[:, :LANES] * inv[:, :LANES] - lam * (acc[:, LANES:] * inv[:, LANES:])
        o = ot.T
        y = o * lax.rsqrt(jnp.mean(o * o, axis=-1, keepdims=True) + SUBLN_EPS) * g_ref[...] * (1.0 - LAM_INIT)
        za = za_ref[0, st * LANES:(st + 1) * LANES, :].astype(f32)
        o_ref[0, st * LANES:(st + 1) * LANES, :] = (y * (za * jax.nn.sigmoid(za))).astype(bf16)


def _attention(lam, qt, k, vt, subln_g, za, *, bq, bk):
    B, L, _ = k.shape
    nkv = L // bk
    nvt = vt.shape[1]
    return pl.pallas_call(
        functools.partial(_attn_kernel, nkv=nkv, bq=bq, bk=bk),
        grid=(B, N_HEADS, L // bq),
        in_specs=[
            pl.BlockSpec(memory_space=pltpu.SMEM),
            pl.BlockSpec((1, 2 * HEAD_DIM, bq), lambda b, h, i: (b, h, i)),
            pl.BlockSpec((1, L, 2 * HEAD_DIM), lambda b, h, i: (b, 0, h)),
            pl.BlockSpec((1, nvt, V_ROWS, bk // 2), lambda b, h, i: (b, 0, h, 0)),
            pl.BlockSpec((1, 2 * HEAD_DIM), lambda b, h, i: (0, 0)),
            pl.BlockSpec((1, bq, 2 * HEAD_DIM), lambda b, h, i: (b, i, h)),
        ],
        out_specs=pl.BlockSpec((1, bq, 2 * HEAD_DIM), lambda b, h, i: (b, i, h)),
        out_shape=jax.ShapeDtypeStruct((B, L, D_ATTN), bf16),
        scratch_shapes=[
            pltpu.VMEM((bq // LANES, 2 * HEAD_DIM, 2 * LANES), bf16),
            pltpu.VMEM((bq // LANES, V_ROWS, 2 * LANES), f32),
            pltpu.VMEM((bq // LANES, bk, 2 * LANES), bf16),
            pltpu.VMEM((bq // LANES, F32_ROWS, 2 * LANES), f32),
            pltpu.VMEM((bq // LANES, F32_ROWS, 2 * LANES), f32),
        ],
        compiler_params=_params("parallel", "parallel", "arbitrary"),
        name="attention",
    )(lam, qt, k, vt, subln_g, za)


def _out_proj_kernel(x_ref, yssm_ref, zs_ref, ya_ref, g_ref, wg_ref, wglu_ref, bglu_ref,
                     wb_ref, wout_ref, fg_ref, o_ref):
    tm = x_ref.shape[1]
    hm = tm // OUT_PROJ_CHAINS
    for c in range(OUT_PROJ_CHAINS):
        rs = slice(c * hm, (c + 1) * hm)
        x = x_ref[0, rs]
        r = lax.rsqrt(jnp.mean(x * x, axis=-1, keepdims=True) + NORM_EPS)
        h = (x * r * g_ref[...]).astype(bf16)

        ys = jax.nn.gelu(yssm_ref[0, rs].astype(f32))
        ys = ys * jax.nn.sigmoid(_dot(ys.astype(bf16), wglu_ref[...]) + bglu_ref[...])
        zs = zs_ref[0, rs].astype(f32)
        ys = ys * (zs * jax.nn.sigmoid(zs))

        ps = _dot(ys.astype(bf16), wb_ref[0])
        merged = jax.nn.sigmoid(_dot(h, wg_ref[:, 0:D_MODEL])) * ps
        pa = _dot(ya_ref[0, rs], wb_ref[1])
        merged = merged + jax.nn.sigmoid(_dot(h, wg_ref[:, D_MODEL:2 * D_MODEL])) * pa
        out = x + _dot(merged.astype(bf16), wout_ref[...])
        o_ref[0, rs] = out * lax.rsqrt(jnp.mean(out * out, axis=-1, keepdims=True) + NORM_EPS) * fg_ref[...]


def _out_proj(x, yssm, zs, ya, norm_g, wg, wglu, bglu, wb, wout, final_g, *, tm):
    B, L, _ = x.shape
    tok = lambda b, i: (b, i, 0)
    cst = lambda b, i: (0, 0)
    return pl.pallas_call(
        _out_proj_kernel,
        grid=(B, L // tm),
        in_specs=[
            pl.BlockSpec((1, tm, D_MODEL), tok),
            pl.BlockSpec((1, tm, 512), tok),
            pl.BlockSpec((1, tm, 512), tok),
            pl.BlockSpec((1, tm, 512), tok),
            pl.BlockSpec((1, D_MODEL), cst),
            pl.BlockSpec(wg.shape, cst),
            pl.BlockSpec(wglu.shape, cst),
            pl.BlockSpec((1, D_SSM), cst),
            pl.BlockSpec(wb.shape, lambda b, i: (0, 0, 0)),
            pl.BlockSpec(wout.shape, cst),
            pl.BlockSpec((1, D_MODEL), cst),
        ],
        out_specs=pl.BlockSpec((1, tm, D_MODEL), tok),
        out_shape=jax.ShapeDtypeStruct((B, L, D_MODEL), x.dtype),
        compiler_params=_params("parallel", "parallel"),
        name="out_proj",
    )(x, yssm, zs, ya, norm_g, wg, wglu, bglu, wb, wout, final_g)


def _rotary_tables(L):
    half = HEAD_DIM // 2
    inv_freq = 1.0 / (ROPE_THETA ** (jnp.arange(0, half, dtype=f32) * 2.0 / HEAD_DIM))
    ang = jnp.arange(L, dtype=f32)[:, None] * inv_freq[None, :]
    cos, sin = jnp.cos(ang), jnp.sin(ang)
    ck = jnp.tile(cos, (1, LANES // half))
    sk = jnp.tile(sin, (1, LANES // half))
    scale = math.log2(math.e) / math.sqrt(HEAD_DIM)
    return ck, sk, cos.T * scale, sin.T * scale


def _trunk(x, w):
    B, L, _ = x.shape
    assert B == 8 and L % ATTN_BQ == 0, "scan state vregs hold one row per batch element"
    ck, sk, cq, sq = _rotary_tables(L)
    u, zs, k, za, qt, vt = _in_proj(x, w["norm_g"], w["wa"], w["wbt"], ck, sk, cq, sq, tb=PROJ_TOKENS)
    hf, gb = _ssm_state(u, w["bst_f"], w["bst_b"], w["coef"], nb=B)
    yssm = _ssm_out(u, hf, gb, w["m"], w["cst_f"], w["cst_b"], nb=B)
    ya = _attention(w["lam"], qt, k, vt, w["subln_g"], za, bq=ATTN_BQ, bk=ATTN_BK)
    return _out_proj(x, yssm, zs, ya, w["norm_g"], w["wg"], w["wglu"], w["bglu"], w["wb"], w["wout"],
                     w["final_g"], tm=TOKEN_BLOCK)


def _rotate_half_columns(wk):
    d = wk.shape[0]
    w4 = wk.reshape(d, D_ATTN // HEAD_DIM, 2, HEAD_DIM // 2)
    return jnp.stack([-w4[:, :, 1], w4[:, :, 0]], axis=2).reshape(d, D_ATTN)


def kernel(x_prompt, x_sample, norm_g, w_in, ssm_a_re, ssm_a_im, ssm_log_dt, ssm_b_re, ssm_b_im, ssm_c_re, ssm_c_im, ssm_d, w_glu, b_glu, lambda_q1, lambda_k1, lambda_q2, lambda_k2, subln_g, w_branch, w_out, final_g):
    li = 0
    wi = w_in[li].astype(f32)
    w_xs, w_zs = wi[:, 0:512], wi[:, 512:1024]
    w_q, w_k, w_v, w_za = wi[:, 1024:1536], wi[:, 1536:2048], wi[:, 2048:2560], wi[:, 2560:3072]
    m, bst_f, bst_b, cst_f, cst_b, coef = _ssm_tables(
        ssm_a_re[li], ssm_a_im[li], ssm_log_dt[li], ssm_b_re[li], ssm_b_im[li],
        ssm_c_re[li], ssm_c_im[li], ssm_d[li])
    nb = x_prompt.shape[0]
    lam = (jnp.exp(jnp.sum(lambda_q1[li].astype(f32) * lambda_k1[li].astype(f32)))
           - jnp.exp(jnp.sum(lambda_q2[li].astype(f32) * lambda_k2[li].astype(f32))) + LAM_INIT)
    w = dict(
        norm_g=norm_g[li].astype(f32).reshape(1, D_MODEL),
        wa=jnp.concatenate([w_xs, w_zs, w_k, _rotate_half_columns(w_k), w_za], axis=1).astype(bf16),
        wbt=jnp.concatenate([w_q, w_v], axis=1).T.astype(bf16),
        wg=wi[:, 3072:5120].astype(bf16),
        m=m, bst_f=bst_f, bst_b=bst_b, cst_f=cst_f, cst_b=cst_b,
        coef=jnp.broadcast_to(coef[:, :, :, None, :], (2, N_GROUPS, 3, nb, LANES)),
        lam=lam.reshape(1).astype(f32),
        subln_g=subln_g[li].astype(f32).reshape(1, 2 * HEAD_DIM),
        wglu=w_glu[li].astype(bf16),
        bglu=b_glu[li].astype(f32).reshape(1, D_SSM),
        wb=w_branch[li].astype(bf16),
        wout=w_out[li].astype(bf16),
        final_g=final_g.astype(f32).reshape(1, D_MODEL),
    )
    return (_trunk(x_prompt, w), _trunk(x_sample, w))
```

```python
import functools
import math

import jax
import jax.numpy as jnp
from jax import lax
from jax.experimental import pallas as pl
from jax.experimental.pallas import tpu as pltpu

D_MODEL = 1024
D_SSM = 512
SSM_GROUP = 16
N_GROUPS = 32
STATE = 64
D_ATTN = 512
N_HEADS = 4
HEAD_DIM = 64
ROPE_THETA = 10000.0
NORM_EPS = 1e-6
SUBLN_EPS = 1e-5
LAM_INIT = 0.8 - 0.6 * math.exp(-0.3 * 0)

CHUNK = 16
CHUNK_LANES = CHUNK * SSM_GROUP
LANES = 128
VMEM_LIMIT = 56 * 1024 * 1024

TOKEN_BLOCK = 1024
OUT_PROJ_CHAINS = 4
ATTN_BQ = 2048
ATTN_UNROLL = 16
ATTN_BK = 256
PROJ_TOKENS = ATTN_BK // 2
F32_ROWS = 8
ROW_PAD = 4
BF16_ROWS = 16
V_ROWS = 2 * HEAD_DIM + BF16_ROWS
SSM_SEG_CHUNKS = 16
SSM_GROUP_BLOCK = 4
SSM_OUT_ROWS = 128

f32 = jnp.float32
bf16 = jnp.bfloat16


def _params(*sem):
    return pltpu.CompilerParams(dimension_semantics=sem, vmem_limit_bytes=VMEM_LIMIT)


def _dot(a, b):
    return jnp.dot(a, b, preferred_element_type=f32)


def _dot_nt(a, b):
    return lax.dot_general(a, b, (((1,), (1,)), ((), ())), preferred_element_type=f32)


def _segment_transpose8(vs):
    n = len(vs)
    slot = lax.broadcasted_iota(jnp.int32, vs[0].shape, 1) // SSM_GROUP
    skew = [v if t == 0 else pltpu.roll(v, t * SSM_GROUP, 1) for t, v in enumerate(vs)]
    out = []
    for g in range(n):
        acc = skew[(0 - g) % n]
        for s in range(1, n):
            acc = jnp.where(slot == s, skew[(s - g) % n], acc)
        out.append(acc if g == 0 else pltpu.roll(acc, LANES - g * SSM_GROUP, 1))
    return out


def _in_proj_kernel(x_ref, g_ref, wa_ref, wbt_ref, ck_ref, sk_ref, cq_ref, sq_ref,
                    u_ref, zs_ref, k_ref, za_ref, qt_ref, vt_ref, xs_s, *, nb, tb):
    rows = nb * tb
    x = x_ref[...].reshape(rows, D_MODEL)
    r = lax.rsqrt(jnp.mean(x * x, axis=-1, keepdims=True) + NORM_EPS)
    h = (x * r * g_ref[...]).astype(bf16)

    xs = _dot(h, wa_ref[:, 0:D_SSM])
    for tile in range(D_SSM // LANES):
        for b in range(nb):
            xs_s[tile, b * (tb + ROW_PAD):b * (tb + ROW_PAD) + tb] = xs[b * tb:(b + 1) * tb, tile * LANES:(tile + 1) * LANES]
    zs_ref[...] = _dot(h, wa_ref[:, 512:1024]).astype(bf16).reshape(nb, tb, D_SSM)
    kk = _dot(h, wa_ref[:, 1024:1536]).reshape(nb, tb, D_ATTN)
    kr = _dot(h, wa_ref[:, 1536:2048]).reshape(nb, tb, D_ATTN)
    ck = ck_ref[...]
    sk = sk_ref[...]
    for j in range(D_ATTN // LANES):
        sl = slice(j * LANES, (j + 1) * LANES)
        k_ref[:, :, sl] = (kk[:, :, sl] * ck + kr[:, :, sl] * sk).astype(bf16)
    za_ref[...] = _dot(h, wa_ref[:, 2048:2560]).astype(bf16).reshape(nb, tb, D_ATTN)

    pq = _dot_nt(wbt_ref[0:512, :], h)
    cq = jnp.tile(cq_ref[...], (1, nb))
    sq = jnp.tile(sq_ref[...], (1, nb))
    half = HEAD_DIM // 2
    for hb in range(D_ATTN // HEAD_DIM):
        x1 = pq[hb * HEAD_DIM:hb * HEAD_DIM + half]
        x2 = pq[hb * HEAD_DIM + half:(hb + 1) * HEAD_DIM]
        q1 = (x1 * cq - x2 * sq).astype(bf16)
        q2 = (x2 * cq + x1 * sq).astype(bf16)
        for b in range(nb):
            qt_ref[b, hb * HEAD_DIM:hb * HEAD_DIM + half, :] = q1[:, b * tb:(b + 1) * tb]
            qt_ref[b, hb * HEAD_DIM + half:(hb + 1) * HEAD_DIM, :] = q2[:, b * tb:(b + 1) * tb]
    pv = _dot_nt(wbt_ref[512:1024, :], h).astype(bf16)
    dv = 2 * HEAD_DIM
    ones = jnp.ones((V_ROWS - dv, tb), bf16)
    for b in range(nb):
        for hd in range(N_HEADS):
            vt_ref[b, 0, hd * V_ROWS:hd * V_ROWS + dv, :] = pv[hd * dv:(hd + 1) * dv, b * tb:(b + 1) * tb]
            vt_ref[b, 0, hd * V_ROWS + dv:(hd + 1) * V_ROWS, :] = ones

    nchunk = tb // CHUNK
    for tile in range(D_SSM // LANES):
        for hf in range(CHUNK_LANES // LANES):
            vs = []
            for t8 in range(8):
                t = 8 * hf + t8
                vs.append(jnp.concatenate(
                    [xs_s[tile, pl.ds(t + CHUNK * ch, nb, stride=tb + ROW_PAD), :] for ch in range(nchunk)],
                    axis=0))
            out = _segment_transpose8(vs)
            for g8 in range(8):
                u_ref[8 * tile + g8, :, hf * LANES:(hf + 1) * LANES] = out[g8].astype(bf16)


def _in_proj(x, norm_g, wa, wbt, ck, sk, cq, sq, *, tb):
    B, L, _ = x.shape
    nchunk = tb // CHUNK
    tok = lambda i: (0, i, 0)
    cst = lambda i: (0, 0)
    out_tok = jax.ShapeDtypeStruct((B, L, 512), bf16)
    return pl.pallas_call(
        functools.partial(_in_proj_kernel, nb=B, tb=tb),
        grid=(L // tb,),
        in_specs=[
            pl.BlockSpec((B, tb, D_MODEL), tok),
            pl.BlockSpec((1, D_MODEL), cst),
            pl.BlockSpec(wa.shape, cst),
            pl.BlockSpec(wbt.shape, cst),
            pl.BlockSpec((tb, LANES), lambda i: (i, 0)),
            pl.BlockSpec((tb, LANES), lambda i: (i, 0)),
            pl.BlockSpec((HEAD_DIM // 2, tb), lambda i: (0, i)),
            pl.BlockSpec((HEAD_DIM // 2, tb), lambda i: (0, i)),
        ],
        out_specs=[
            pl.BlockSpec((N_GROUPS, nchunk * B, CHUNK_LANES), tok),
            pl.BlockSpec((B, tb, 512), tok),
            pl.BlockSpec((B, tb, 512), tok),
            pl.BlockSpec((B, tb, 512), tok),
            pl.BlockSpec((B, 512, tb), lambda i: (0, 0, i)),
            pl.BlockSpec((B, 1, N_HEADS * V_ROWS, tb), lambda i: (0, i, 0, 0)),
        ],
        out_shape=[jax.ShapeDtypeStruct((N_GROUPS, (L // CHUNK) * B, CHUNK_LANES), bf16),
                   out_tok, out_tok, out_tok,
                   jax.ShapeDtypeStruct((B, 512, L), bf16),
                   jax.ShapeDtypeStruct((B, L // tb, N_HEADS * V_ROWS, tb), bf16)],
        scratch_shapes=[pltpu.VMEM((D_SSM // LANES, B * (tb + ROW_PAD), LANES), f32)],
        compiler_params=_params("parallel"),
        name="in_proj",
    )(x, norm_g, wa, wbt, ck, sk, cq, sq)


def _ssm_state_kernel(uf_ref, ub_ref, bf_ref, bb_ref, coef_ref, hf_ref, gb_ref, s_ref, st_ref,
                      *, nc, nb, gblk):
    @pl.when(pl.program_id(0) == 0)
    def _():
        st_ref[...] = jnp.zeros_like(st_ref)

    for g0 in range(0, N_GROUPS, gblk):
        for gi in range(gblk):
            s_ref[0, gi] = _dot(uf_ref[g0 + gi], bf_ref[g0 + gi])
            s_ref[1, gi] = _dot(ub_ref[g0 + gi], bb_ref[g0 + gi])
        gs = slice(g0, g0 + gblk)
        af1, af2, af3 = coef_ref[0, gs, 0], coef_ref[0, gs, 1], coef_ref[0, gs, 2]
        ab1, ab2, ab3 = coef_ref[1, gs, 0], coef_ref[1, gs, 1], coef_ref[1, gs, 2]

        def body(i, carry):
            hf, wf, hb, wb = carry
            rf = pl.multiple_of(i * nb, nb)
            rb = pl.multiple_of((nc - 1 - i) * nb, nb)
            hf_ref[gs, pl.ds(rf, nb), :] = hf
            gb_ref[gs, pl.ds(rb, nb), :] = hb
            sf = s_ref[0, :, pl.ds(rf, nb), :]
            sb = s_ref[1, :, pl.ds(rb, nb), :]
            hf2 = af1 * hf + af2 * wf + sf[..., :LANES]
            wf2 = af1 * wf + af3 * hf + sf[..., LANES:]
            hb2 = ab1 * hb + ab2 * wb + sb[..., :LANES]
            wb2 = ab1 * wb + ab3 * hb + sb[..., LANES:]
            return hf2, wf2, hb2, wb2

        init = (st_ref[0, 0, gs], st_ref[0, 1, gs], st_ref[1, 0, gs], st_ref[1, 1, gs])
        hf, wf, hb, wb = lax.fori_loop(0, nc, body, init, unroll=True)
        st_ref[0, 0, gs] = hf
        st_ref[0, 1, gs] = wf
        st_ref[1, 0, gs] = hb
        st_ref[1, 1, gs] = wb


def _ssm_state(u, bst_f, bst_b, coef, *, nb):
    G, rows, _ = u.shape
    nc = SSM_SEG_CHUNKS
    seg_rows = nc * nb
    nseg = rows // seg_rows
    gblk = SSM_GROUP_BLOCK
    fwd = lambda i: (0, i, 0)
    bwd = lambda i: (0, nseg - 1 - i, 0)
    cst3 = lambda i: (0, 0, 0)
    return pl.pallas_call(
        functools.partial(_ssm_state_kernel, nc=nc, nb=nb, gblk=gblk),
        grid=(nseg,),
        in_specs=[
            pl.BlockSpec((G, seg_rows, CHUNK_LANES), fwd),
            pl.BlockSpec((G, seg_rows, CHUNK_LANES), bwd),
            pl.BlockSpec(bst_f.shape, cst3),
            pl.BlockSpec(bst_b.shape, cst3),
            pl.BlockSpec(coef.shape, lambda i: (0, 0, 0, 0, 0)),
        ],
        out_specs=[
            pl.BlockSpec((G, seg_rows, LANES), fwd),
            pl.BlockSpec((G, seg_rows, LANES), bwd),
        ],
        out_shape=[jax.ShapeDtypeStruct((G, rows, LANES), f32),
                   jax.ShapeDtypeStruct((G, rows, LANES), f32)],
        scratch_shapes=[
            pltpu.VMEM((2, gblk, seg_rows, CHUNK_LANES), f32),
            pltpu.VMEM((2, 2, G, nb, LANES), f32),
        ],
        compiler_params=_params("arbitrary"),
        name="ssm_state",
    )(u, u, bst_f, bst_b, coef)


def _ssm_out_kernel(u_ref, hf_ref, gb_ref, m_ref, cf_ref, cb_ref, y_ref, nat_s, *, nb, nchunk):
    tok = nchunk * CHUNK
    for tile in range(D_SSM // LANES):
        ys = []
        for g8 in range(8):
            g = 8 * tile + g8
            y = _dot(u_ref[g], m_ref[g])
            y = y + _dot(hf_ref[g].astype(bf16), cf_ref[g])
            y = y + _dot(gb_ref[g].astype(bf16), cb_ref[g])
            ys.append(y)
        for hf in range(CHUNK_LANES // LANES):
            out = _segment_transpose8([y[:, hf * LANES:(hf + 1) * LANES] for y in ys])
            for t8 in range(8):
                t = 8 * hf + t8
                for ch in range(nchunk):
                    nat_s[tile, pl.ds(t + CHUNK * ch, nb, stride=tok + ROW_PAD), :] = out[t8][ch * nb:(ch + 1) * nb]
        for b in range(nb):
            y_ref[b, :, tile * LANES:(tile + 1) * LANES] = (
                nat_s[tile, b * (tok + ROW_PAD):b * (tok + ROW_PAD) + tok].astype(bf16))


def _ssm_out(u, hf, gb, m, cst_f, cst_b, *, nb):
    G, rows, _ = u.shape
    rb = min(SSM_OUT_ROWS, rows)
    nchunk = rb // nb
    tok = nchunk * CHUNK
    blk = lambda i: (0, i, 0)
    cst3 = lambda i: (0, 0, 0)
    return pl.pallas_call(
        functools.partial(_ssm_out_kernel, nb=nb, nchunk=nchunk),
        grid=(rows // rb,),
        in_specs=[
            pl.BlockSpec((G, rb, CHUNK_LANES), blk),
            pl.BlockSpec((G, rb, LANES), blk),
            pl.BlockSpec((G, rb, LANES), blk),
            pl.BlockSpec(m.shape, cst3),
            pl.BlockSpec(cst_f.shape, cst3),
            pl.BlockSpec(cst_b.shape, cst3),
        ],
        out_specs=pl.BlockSpec((nb, tok, D_SSM), blk),
        out_shape=jax.ShapeDtypeStruct((nb, (rows // nb) * CHUNK, D_SSM), bf16),
        scratch_shapes=[pltpu.VMEM((D_SSM // LANES, nb * (tok + ROW_PAD), LANES), f32)],
        compiler_params=_params("parallel"),
        name="ssm_out",
    )(u, hf, gb, m, cst_f, cst_b)


def _ssm_tables(a_re, a_im, log_dt, b_re, b_im, c_re, c_im, d_skip):
    T, G, P, C = CHUNK, N_GROUPS, STATE, SSM_GROUP
    hi = lax.Precision.HIGHEST
    a_re, a_im, log_dt = a_re.astype(f32), a_im.astype(f32), log_dt.astype(f32)
    dt = jnp.exp(log_dt)[..., None]
    H = T // 2
    ks = jnp.arange(-H, T + 1, dtype=f32)[:, None, None, None]
    mag = jnp.exp(ks * (a_re * dt))
    pw_re = mag * jnp.cos(ks * (a_im * dt))
    pw_im = mag * jnp.sin(ks * (a_im * dt))
    n_re, n_im = pw_re[H + 1] - 1.0, pw_im[H + 1]
    den = a_re * a_re + a_im * a_im
    co_re = (n_re * a_re + n_im * a_im) / den
    co_im = (n_im * a_re - n_re * a_im) / den
    b_re, b_im = b_re.astype(f32), b_im.astype(f32)
    bb_re = co_re[..., None] * b_re - co_im[..., None] * b_im
    bb_im = co_re[..., None] * b_im + co_im[..., None] * b_re
    c_re, c_im = c_re.astype(f32), c_im.astype(f32)

    def xtab(n, pr, pi):
        re = pr[..., None] * bb_re[n][None] - pi[..., None] * bb_im[n][None]
        im = pr[..., None] * bb_im[n][None] + pi[..., None] * bb_re[n][None]
        return (re.transpose(1, 0, 3, 2).reshape(G, T * C, P), im.transpose(1, 0, 3, 2).reshape(G, T * C, P))

    def ytab(n, pr, pi):
        re = c_re[n][None] * pr[:, :, None, :] - c_im[n][None] * pi[:, :, None, :]
        im = c_re[n][None] * pi[:, :, None, :] + c_im[n][None] * pr[:, :, None, :]
        return (re.transpose(1, 3, 0, 2).reshape(G, P, T * C), im.transpose(1, 3, 0, 2).reshape(G, P, T * C))

    def lag_kernels(x, y):
        return (jnp.einsum('gjp,gpt->gjt', x[0], y[0], precision=hi)
                - jnp.einsum('gjp,gpt->gjt', x[1], y[1], precision=hi))

    def powers(n, lo, reverse=False):
        pr, pi = pw_re[lo + H:lo + H + T, n], pw_im[lo + H:lo + H + T, n]
        return (pr[::-1], pi[::-1]) if reverse else (pr, pi)

    kf = lag_kernels(xtab(0, *powers(0, -H, reverse=True)), ytab(0, *powers(0, 1 - H)))
    kb = lag_kernels(xtab(1, *powers(1, -H)), ytab(1, *powers(1, 1 - H, reverse=True)))
    step = jnp.arange(T * C) // C
    causal = step[None, :] >= step[:, None]
    anti = step[None, :] <= step[:, None]
    d_diag = jnp.tile(d_skip.astype(f32).reshape(G, 1, C), (1, T, 1)).reshape(G, 1, T * C)
    m = (jnp.where(causal[None], kf, 0.0) + jnp.where(anti[None], kb, 0.0)
         + jnp.eye(T * C, dtype=f32)[None] * d_diag)

    xf = xtab(0, *powers(0, 0, reverse=True))
    xb = xtab(1, *powers(1, 0))
    yf = ytab(0, *powers(0, 1))
    yb = ytab(1, *powers(1, 1, reverse=True))

    bst_f = jnp.concatenate([xf[0], xf[1], xf[1], xf[0]], axis=-1)
    bst_b = jnp.concatenate([xb[0], xb[1], xb[1], xb[0]], axis=-1)
    cst_f = jnp.concatenate([yf[0], -yf[1]], axis=1)
    cst_b = jnp.concatenate([yb[0], -yb[1]], axis=1)
    ar, ai = pw_re[H + T], pw_im[H + T]
    coef = jnp.stack([jnp.concatenate([ar, ar], -1),
                      jnp.concatenate([-ai, ai], -1),
                      jnp.concatenate([ai, -ai], -1)], axis=2)
    return m.astype(bf16), bst_f.astype(bf16), bst_b.astype(bf16), cst_f.astype(bf16), cst_b.astype(bf16), coef


def _attn_kernel(lam_ref, qt_ref, k_ref, vt_ref, g_ref, za_ref, o_ref,
                 qbd_ref, acc_ref, p_ref, m_ref, a_ref, *, nkv, bq, bk):
    lam = lam_ref[0]
    half = HEAD_DIM
    ns = bq // LANES
    dv = 2 * HEAD_DIM
    zero = jnp.zeros((half, LANES), bf16)
    for st in range(ns):
        qs = qt_ref[0, :, st * LANES:(st + 1) * LANES]
        qbd_ref[st, 0:half, 0:LANES] = qs[0:half]
        qbd_ref[st, 0:half, LANES:2 * LANES] = zero
        qbd_ref[st, half:2 * half, 0:LANES] = zero
        qbd_ref[st, half:2 * half, LANES:2 * LANES] = qs[half:2 * half]
    acc_ref[...] = jnp.zeros_like(acc_ref)
    p_ref[...] = jnp.zeros_like(p_ref)
    a_ref[...] = jnp.ones_like(a_ref)
    m_ref[...] = jnp.full(m_ref.shape, -jnp.inf, f32)

    def values(j):
        return jnp.concatenate([vt_ref[0, 2 * j], vt_ref[0, 2 * j + 1]], axis=1)

    nsub = F32_ROWS

    def fold(st, vb):
        acc = acc_ref[st].reshape(V_ROWS // nsub, nsub, 2 * LANES) * a_ref[st][None]
        acc_ref[st] = acc.reshape(V_ROWS, 2 * LANES) + _dot(vb, p_ref[st])

    def colmax(s):
        mx = jnp.max(s.reshape(-1, nsub, 2 * LANES), axis=0)
        for sh in (4, 2, 1):
            mx = jnp.maximum(mx, pltpu.roll(mx, sh, 0))
        return mx

    def probs(s, m):
        x = s.reshape(-1, nsub, 2 * LANES) - m[None]
        return jnp.exp2(x.reshape(s.shape).astype(bf16))

    def body(j, carry):
        off = pl.multiple_of(j * bk, bk)
        ka = k_ref[0, pl.ds(off, bk // 2), :]
        kb = k_ref[0, pl.ds(off + bk // 2, bk // 2), :]
        vb = values(jnp.maximum(j - 1, 0))
        for st in range(ns):
            fold(st, vb)
            m_old = m_ref[st]
            s_a = _dot(ka, qbd_ref[st])
            m_a = jnp.maximum(m_old, colmax(s_a))
            p_a = probs(s_a, m_a)
            s_b = _dot(kb, qbd_ref[st])
            m_b = jnp.maximum(m_a, colmax(s_b))
            p_b = probs(s_b, m_b)
            corr = jnp.exp2(m_a - m_b)
            corr = jnp.concatenate([corr, corr], axis=0).astype(bf16)
            p_a = p_a.reshape(-1, BF16_ROWS, 2 * LANES) * corr[None]
            p_ref[st, 0:bk // 2] = p_a.reshape(bk // 2, 2 * LANES)
            p_ref[st, bk // 2:bk] = p_b
            a_ref[st] = jnp.exp2(m_old - m_b)
            m_ref[st] = m_b
        return carry

    lax.fori_loop(0, nkv, body, 0, unroll=min(ATTN_UNROLL, nkv))

    vb_last = values(nkv - 1)
    for st in range(ns):
        fold(st, vb_last)
        acc = acc_ref[st, 0:dv]
        inv = 1.0 / acc_ref[st, dv:dv + 1]
        ot = acc[:, :LANES] * inv[:, :LANES] - lam * (acc[:, LANES:] * inv[:, LANES:])
        o = ot.T
        y = o * lax.rsqrt(jnp.mean(o * o, axis=-1, keepdims=True) + SUBLN_EPS) * g_ref[...] * (1.0 - LAM_INIT)
        za = za_ref[0, st * LANES:(st + 1) * LANES, :].astype(f32)
        o_ref[0, st * LANES:(st + 1) * LANES, :] = (y * (za * jax.nn.sigmoid(za))).astype(bf16)


def _attention(lam, qt, k, vt, subln_g, za, *, bq, bk):
    B, L, _ = k.shape
    nkv = L // bk
    nvt = vt.shape[1]
    return pl.pallas_call(
        functools.partial(_attn_kernel, nkv=nkv, bq=bq, bk=bk),
        grid=(B, N_HEADS, L // bq),
        in_specs=[
            pl.BlockSpec(memory_space=pltpu.SMEM),
            pl.BlockSpec((1, 2 * HEAD_DIM, bq), lambda b, h, i: (b, h, i)),
            pl.BlockSpec((1, L, 2 * HEAD_DIM), lambda b, h, i: (b, 0, h)),
            pl.BlockSpec((1, nvt, V_ROWS, bk // 2), lambda b, h, i: (b, 0, h, 0)),
            pl.BlockSpec((1, 2 * HEAD_DIM), lambda b, h, i: (0, 0)),
            pl.BlockSpec((1, bq, 2 * HEAD_DIM), lambda b, h, i: (b, i, h)),
        ],
        out_specs=pl.BlockSpec((1, bq, 2 * HEAD_DIM), lambda b, h, i: (b, i, h)),
        out_shape=jax.ShapeDtypeStruct((B, L, D_ATTN), bf16),
        scratch_shapes=[
            pltpu.VMEM((bq // LANES, 2 * HEAD_DIM, 2 * LANES), bf16),
            pltpu.VMEM((bq // LANES, V_ROWS, 2 * LANES), f32),
            pltpu.VMEM((bq // LANES, bk, 2 * LANES), bf16),
            pltpu.VMEM((bq // LANES, F32_ROWS, 2 * LANES), f32),
            pltpu.VMEM((bq // LANES, F32_ROWS, 2 * LANES), f32),
        ],
        compiler_params=_params("parallel", "parallel", "arbitrary"),
        name="attention",
    )(lam, qt, k, vt, subln_g, za)


def _out_proj_kernel(x_ref, yssm_ref, zs_ref, ya_ref, g_ref, wg_ref, wglu_ref, bglu_ref,
                     wb_ref, wout_ref, fg_ref, o_ref):
    tm = x_ref.shape[1]
    hm = tm // OUT_PROJ_CHAINS
    for c in range(OUT_PROJ_CHAINS):
        rs = slice(c * hm, (c + 1) * hm)
        x = x_ref[0, rs]
        r = lax.rsqrt(jnp.mean(x * x, axis=-1, keepdims=True) + NORM_EPS)
        h = (x * r * g_ref[...]).astype(bf16)

        ys = jax.nn.gelu(yssm_ref[0, rs].astype(f32))
        ys = ys * jax.nn.sigmoid(_dot(ys.astype(bf16), wglu_ref[...]) + bglu_ref[...])
        zs = zs_ref[0, rs].astype(f32)
        ys = ys * (zs * jax.nn.sigmoid(zs))

        ps = _dot(ys.astype(bf16), wb_ref[0])
        merged = jax.nn.sigmoid(_dot(h, wg_ref[:, 0:D_MODEL])) * ps
        pa = _dot(ya_ref[0, rs], wb_ref[1])
        merged = merged + jax.nn.sigmoid(_dot(h, wg_ref[:, D_MODEL:2 * D_MODEL])) * pa
        out = x + _dot(merged.astype(bf16), wout_ref[...])
        o_ref[0, rs] = out * lax.rsqrt(jnp.mean(out * out, axis=-1, keepdims=True) + NORM_EPS) * fg_ref[...]


def _out_proj(x, yssm, zs, ya, norm_g, wg, wglu, bglu, wb, wout, final_g, *, tm):
    B, L, _ = x.shape
    tok = lambda b, i: (b, i, 0)
    cst = lambda b, i: (0, 0)
    return pl.pallas_call(
        _out_proj_kernel,
        grid=(B, L // tm),
        in_specs=[
            pl.BlockSpec((1, tm, D_MODEL), tok),
            pl.BlockSpec((1, tm, 512), tok),
            pl.BlockSpec((1, tm, 512), tok),
            pl.BlockSpec((1, tm, 512), tok),
            pl.BlockSpec((1, D_MODEL), cst),
            pl.BlockSpec(wg.shape, cst),
            pl.BlockSpec(wglu.shape, cst),
            pl.BlockSpec((1, D_SSM), cst),
            pl.BlockSpec(wb.shape, lambda b, i: (0, 0, 0)),
            pl.BlockSpec(wout.shape, cst),
            pl.BlockSpec((1, D_MODEL), cst),
        ],
        out_specs=pl.BlockSpec((1, tm, D_MODEL), tok),
        out_shape=jax.ShapeDtypeStruct((B, L, D_MODEL), x.dtype),
        compiler_params=_params("parallel", "parallel"),
        name="out_proj",
    )(x, yssm, zs, ya, norm_g, wg, wglu, bglu, wb, wout, final_g)


def _rotary_tables(L):
    half = HEAD_DIM // 2
    inv_freq = 1.0 / (ROPE_THETA ** (jnp.arange(0, half, dtype=f32) * 2.0 / HEAD_DIM))
    ang = jnp.arange(L, dtype=f32)[:, None] * inv_freq[None, :]
    cos, sin = jnp.cos(ang), jnp.sin(ang)
    ck = jnp.tile(cos, (1, LANES // half))
    sk = jnp.tile(sin, (1, LANES // half))
    scale = math.log2(math.e) / math.sqrt(HEAD_DIM)
    return ck, sk, cos.T * scale, sin.T * scale


def _trunk(x, w):
    B, L, _ = x.shape
    assert B == 8 and L % ATTN_BQ == 0, "scan state vregs hold one row per batch element"
    ck, sk, cq, sq = _rotary_tables(L)
    u, zs, k, za, qt, vt = _in_proj(x, w["norm_g"], w["wa"], w["wbt"], ck, sk, cq, sq, tb=PROJ_TOKENS)
    hf, gb = _ssm_state(u, w["bst_f"], w["bst_b"], w["coef"], nb=B)
    yssm = _ssm_out(u, hf, gb, w["m"], w["cst_f"], w["cst_b"], nb=B)
    ya = _attention(w["lam"], qt, k, vt, w["subln_g"], za, bq=ATTN_BQ, bk=ATTN_BK)
    return _out_proj(x, yssm, zs, ya, w["norm_g"], w["wg"], w["wglu"], w["bglu"], w["wb"], w["wout"],
                     w["final_g"], tm=TOKEN_BLOCK)


def _rotate_half_columns(wk):
    d = wk.shape[0]
    w4 = wk.reshape(d, D_ATTN // HEAD_DIM, 2, HEAD_DIM // 2)
    return jnp.stack([-w4[:, :, 1], w4[:, :, 0]], axis=2).reshape(d, D_ATTN)


def kernel(x_prompt, x_sample, norm_g, w_in, ssm_a_re, ssm_a_im, ssm_log_dt, ssm_b_re, ssm_b_im, ssm_c_re, ssm_c_im, ssm_d, w_glu, b_glu, lambda_q1, lambda_k1, lambda_q2, lambda_k2, subln_g, w_branch, w_out, final_g):
    li = 0
    wi = w_in[li].astype(f32)
    w_xs, w_zs = wi[:, 0:512], wi[:, 512:1024]
    w_q, w_k, w_v, w_za = wi[:, 1024:1536], wi[:, 1536:2048], wi[:, 2048:2560], wi[:, 2560:3072]
    m, bst_f, bst_b, cst_f, cst_b, coef = _ssm_tables(
        ssm_a_re[li], ssm_a_im[li], ssm_log_dt[li], ssm_b_re[li], ssm_b_im[li],
        ssm_c_re[li], ssm_c_im[li], ssm_d[li])
    nb = x_prompt.shape[0]
    lam = (jnp.exp(jnp.sum(lambda_q1[li].astype(f32) * lambda_k1[li].astype(f32)))
           - jnp.exp(jnp.sum(lambda_q2[li].astype(f32) * lambda_k2[li].astype(f32))) + LAM_INIT)
    w = dict(
        norm_g=norm_g[li].astype(f32).reshape(1, D_MODEL),
        wa=jnp.concatenate([w_xs, w_zs, w_k, _rotate_half_columns(w_k), w_za], axis=1).astype(bf16),
        wbt=jnp.concatenate([w_q, w_v], axis=1).T.astype(bf16),
        wg=wi[:, 3072:5120].astype(bf16),
        m=m, bst_f=bst_f, bst_b=bst_b, cst_f=cst_f, cst_b=cst_b,
        coef=jnp.broadcast_to(coef[:, :, :, None, :], (2, N_GROUPS, 3, nb, LANES)),
        lam=lam.reshape(1).astype(f32),
        subln_g=subln_g[li].astype(f32).reshape(1, 2 * HEAD_DIM),
        wglu=w_glu[li].astype(bf16),
        bglu=b_glu[li].astype(f32).reshape(1, D_SSM),
        wb=w_branch[li].astype(bf16),
        wout=w_out[li].astype(bf16),
        final_g=final_g.astype(f32).reshape(1, D_MODEL),
    )
    return (_trunk(x_prompt, w), _trunk(x_sample, w))
```

```python
import functools
import math

import jax
import jax.numpy as jnp
from jax import lax
from jax.experimental import pallas as pl
from jax.experimental.pallas import tpu as pltpu

D_MODEL = 1024
D_SSM = 512
SSM_GROUP = 16
N_GROUPS = 32
STATE = 64
D_ATTN = 512
N_HEADS = 4
HEAD_DIM = 64
ROPE_THETA = 10000.0
NORM_EPS = 1e-6
SUBLN_EPS = 1e-5
LAM_INIT = 0.8 - 0.6 * math.exp(-0.3 * 0)

CHUNK = 16
CHUNK_LANES = CHUNK * SSM_GROUP
LANES = 128
VMEM_LIMIT = 56 * 1024 * 1024

TOKEN_BLOCK = 1024
OUT_PROJ_CHAINS = 4
ATTN_BQ = 2048
ATTN_UNROLL = 16
ATTN_BK = 256
PROJ_TOKENS = ATTN_BK // 2
F32_ROWS = 8
ROW_PAD = 4
BF16_ROWS = 16
V_ROWS = 2 * HEAD_DIM + BF16_ROWS
SSM_SEG_CHUNKS = 16
SSM_GROUP_BLOCK = 4
SSM_OUT_ROWS = 128

f32 = jnp.float32
bf16 = jnp.bfloat16


def _params(*sem):
    return pltpu.CompilerParams(dimension_semantics=sem, vmem_limit_bytes=VMEM_LIMIT)


def _dot(a, b):
    return jnp.dot(a, b, preferred_element_type=f32)


def _dot_nt(a, b):
    return lax.dot_general(a, b, (((1,), (1,)), ((), ())), preferred_element_type=f32)


def _segment_transpose8(vs):
    n = len(vs)
    slot = lax.broadcasted_iota(jnp.int32, vs[0].shape, 1) // SSM_GROUP
    skew = [v if t == 0 else pltpu.roll(v, t * SSM_GROUP, 1) for t, v in enumerate(vs)]
    out = []
    for g in range(n):
        acc = skew[(0 - g) % n]
        for s in range(1, n):
            acc = jnp.where(slot == s, skew[(s - g) % n], acc)
        out.append(acc if g == 0 else pltpu.roll(acc, LANES - g * SSM_GROUP, 1))
    return out


def _in_proj_kernel(x_ref, g_ref, wa_ref, wbt_ref, ck_ref, sk_ref, cq_ref, sq_ref,
                    u_ref, zs_ref, k_ref, za_ref, qt_ref, vt_ref, xs_s, *, nb, tb):
    rows = nb * tb
    x = x_ref[...].reshape(rows, D_MODEL)
    r = lax.rsqrt(jnp.mean(x * x, axis=-1, keepdims=True) + NORM_EPS)
    h = (x * r * g_ref[...]).astype(bf16)

    xs = _dot(h, wa_ref[:, 0:D_SSM])
    for tile in range(D_SSM // LANES):
        for b in range(nb):
            xs_s[tile, b * (tb + ROW_PAD):b * (tb + ROW_PAD) + tb] = xs[b * tb:(b + 1) * tb, tile * LANES:(tile + 1) * LANES]
    zs_ref[...] = _dot(h, wa_ref[:, 512:1024]).astype(bf16).reshape(nb, tb, D_SSM)
    kk = _dot(h, wa_ref[:, 1024:1536]).reshape(nb, tb, D_ATTN)
    kr = _dot(h, wa_ref[:, 1536:2048]).reshape(nb, tb, D_ATTN)
    ck = ck_ref[...]
    sk = sk_ref[...]
    for j in range(D_ATTN // LANES):
        sl = slice(j * LANES, (j + 1) * LANES)
        k_ref[:, :, sl] = (kk[:, :, sl] * ck + kr[:, :, sl] * sk).astype(bf16)
    za_ref[...] = _dot(h, wa_ref[:, 2048:2560]).astype(bf16).reshape(nb, tb, D_ATTN)

    pq = _dot_nt(wbt_ref[0:512, :], h)
    cq = jnp.tile(cq_ref[...], (1, nb))
    sq = jnp.tile(sq_ref[...], (1, nb))
    half = HEAD_DIM // 2
    for hb in range(D_ATTN // HEAD_DIM):
        x1 = pq[hb * HEAD_DIM:hb * HEAD_DIM + half]
        x2 = pq[hb * HEAD_DIM + half:(hb + 1) * HEAD_DIM]
        q1 = (x1 * cq - x2 * sq).astype(bf16)
        q2 = (x2 * cq + x1 * sq).astype(bf16)
        for b in range(nb):
            qt_ref[b, hb * HEAD_DIM:hb * HEAD_DIM + half, :] = q1[:, b * tb:(b + 1) * tb]
            qt_ref[b, hb * HEAD_DIM + half:(hb + 1) * HEAD_DIM, :] = q2[:, b * tb:(b + 1) * tb]
    pv = _dot_nt(wbt_ref[512:1024, :], h).astype(bf16)
    dv = 2 * HEAD_DIM
    ones = jnp.ones((V_ROWS - dv, tb), bf16)
    for b in range(nb):
        for hd in range(N_HEADS):
            vt_ref[b, 0, hd * V_ROWS:hd * V_ROWS + dv, :] = pv[hd * dv:(hd + 1) * dv, b * tb:(b + 1) * tb]
            vt_ref[b, 0, hd * V_ROWS + dv:(hd + 1) * V_ROWS, :] = ones

    nchunk = tb // CHUNK
    for tile in range(D_SSM // LANES):
        for hf in range(CHUNK_LANES // LANES):
            vs = []
            for t8 in range(8):
                t = 8 * hf + t8
                vs.append(jnp.concatenate(
                    [xs_s[tile, pl.ds(t + CHUNK * ch, nb, stride=tb + ROW_PAD), :] for ch in range(nchunk)],
                    axis=0))
            out = _segment_transpose8(vs)
            for g8 in range(8):
                u_ref[8 * tile + g8, :, hf * LANES:(hf + 1) * LANES] = out[g8].astype(bf16)


def _in_proj(x, norm_g, wa, wbt, ck, sk, cq, sq, *, tb):
    B, L, _ = x.shape
    nchunk = tb // CHUNK
    tok = lambda i: (0, i, 0)
    cst = lambda i: (0, 0)
    out_tok = jax.ShapeDtypeStruct((B, L, 512), bf16)
    return pl.pallas_call(
        functools.partial(_in_proj_kernel, nb=B, tb=tb),
        grid=(L // tb,),
        in_specs=[
            pl.BlockSpec((B, tb, D_MODEL), tok),
            pl.BlockSpec((1, D_MODEL), cst),
            pl.BlockSpec(wa.shape, cst),
            pl.BlockSpec(wbt.shape, cst),
            pl.BlockSpec((tb, LANES), lambda i: (i, 0)),
            pl.BlockSpec((tb, LANES), lambda i: (i, 0)),
            pl.BlockSpec((HEAD_DIM // 2, tb), lambda i: (0, i)),
            pl.BlockSpec((HEAD_DIM // 2, tb), lambda i: (0, i)),
        ],
        out_specs=[
            pl.BlockSpec((N_GROUPS, nchunk * B, CHUNK_LANES), tok),
            pl.BlockSpec((B, tb, 512), tok),
            pl.BlockSpec((B, tb, 512), tok),
            pl.BlockSpec((B, tb, 512), tok),
            pl.BlockSpec((B, 512, tb), lambda i: (0, 0, i)),
            pl.BlockSpec((B, 1, N_HEADS * V_ROWS, tb), lambda i: (0, i, 0, 0)),
        ],
        out_shape=[jax.ShapeDtypeStruct((N_GROUPS, (L // CHUNK) * B, CHUNK_LANES), bf16),
                   out_tok, out_tok, out_tok,
                   jax.ShapeDtypeStruct((B, 512, L), bf16),
                   jax.ShapeDtypeStruct((B, L // tb, N_HEADS * V_ROWS, tb), bf16)],
        scratch_shapes=[pltpu.VMEM((D_SSM // LANES, B * (tb + ROW_PAD), LANES), f32)],
        compiler_params=_params("parallel"),
        name="in_proj",
    )(x, norm_g, wa, wbt, ck, sk, cq, sq)


def _ssm_state_kernel(uf_ref, ub_ref, bf_ref, bb_ref, coef_ref, hf_ref, gb_ref, s_ref, st_ref,
                      *, nc, nb, gblk):
    @pl.when(pl.program_id(0) == 0)
    def _():
        st_ref[...] = jnp.zeros_like(st_ref)

    for g0 in range(0, N_GROUPS, gblk):
        for gi in range(gblk):
            s_ref[0, gi] = _dot(uf_ref[g0 + gi], bf_ref[g0 + gi])
            s_ref[1, gi] = _dot(ub_ref[g0 + gi], bb_ref[g0 + gi])
        gs = slice(g0, g0 + gblk)
        af1, af2, af3 = coef_ref[0, gs, 0], coef_ref[0, gs, 1], coef_ref[0, gs, 2]
        ab1, ab2, ab3 = coef_ref[1, gs, 0], coef_ref[1, gs, 1], coef_ref[1, gs, 2]

        hf, wf, hb, wb = st_ref[0, 0, gs], st_ref[0, 1, gs], st_ref[1, 0, gs], st_ref[1, 1, gs]
        held_f = held_b = None
        for i in range(nc):
            cf, cb = i, nc - 1 - i
            if i % 2 == 0:
                held_f, held_b = hf, hb
            else:
                hf_ref[gs, (cf - 1) * nb:(cf + 1) * nb, :] = jnp.concatenate([held_f, hf], axis=1).astype(bf16)
                gb_ref[gs, cb * nb:(cb + 2) * nb, :] = jnp.concatenate([hb, held_b], axis=1).astype(bf16)
            sf = s_ref[0, :, cf * nb:(cf + 1) * nb, :]
            sb = s_ref[1, :, cb * nb:(cb + 1) * nb, :]
            hf, wf = (af1 * hf + af2 * wf + sf[..., :LANES], af1 * wf + af3 * hf + sf[..., LANES:])
            hb, wb = (ab1 * hb + ab2 * wb + sb[..., :LANES], ab1 * wb + ab3 * hb + sb[..., LANES:])
        st_ref[0, 0, gs] = hf
        st_ref[0, 1, gs] = wf
        st_ref[1, 0, gs] = hb
        st_ref[1, 1, gs] = wb


def _ssm_state(u, bst_f, bst_b, coef, *, nb):
    G, rows, _ = u.shape
    nc = SSM_SEG_CHUNKS
    seg_rows = nc * nb
    nseg = rows // seg_rows
    gblk = SSM_GROUP_BLOCK
    fwd = lambda i: (0, i, 0)
    bwd = lambda i: (0, nseg - 1 - i, 0)
    cst3 = lambda i: (0, 0, 0)
    return pl.pallas_call(
        functools.partial(_ssm_state_kernel, nc=nc, nb=nb, gblk=gblk),
        grid=(nseg,),
        in_specs=[
            pl.BlockSpec((G, seg_rows, CHUNK_LANES), fwd),
            pl.BlockSpec((G, seg_rows, CHUNK_LANES), bwd),
            pl.BlockSpec(bst_f.shape, cst3),
            pl.BlockSpec(bst_b.shape, cst3),
            pl.BlockSpec(coef.shape, lambda i: (0, 0, 0, 0, 0)),
        ],
        out_specs=[
            pl.BlockSpec((G, seg_rows, LANES), fwd),
            pl.BlockSpec((G, seg_rows, LANES), bwd),
        ],
        out_shape=[jax.ShapeDtypeStruct((G, rows, LANES), bf16),
                   jax.ShapeDtypeStruct((G, rows, LANES), bf16)],
        scratch_shapes=[
            pltpu.VMEM((2, gblk, seg_rows, CHUNK_LANES), f32),
            pltpu.VMEM((2, 2, G, nb, LANES), f32),
        ],
        compiler_params=_params("arbitrary"),
        name="ssm_state",
    )(u, u, bst_f, bst_b, coef)


def _ssm_out_kernel(u_ref, hf_ref, gb_ref, m_ref, cf_ref, cb_ref, y_ref, nat_s, *, nb, nchunk):
    tok = nchunk * CHUNK
    for tile in range(D_SSM // LANES):
        ys = []
        for g8 in range(8):
            g = 8 * tile + g8
            y = _dot(u_ref[g], m_ref[g])
            y = y + _dot(hf_ref[g], cf_ref[g])
            y = y + _dot(gb_ref[g], cb_ref[g])
            ys.append(y)
        for hf in range(CHUNK_LANES // LANES):
            out = _segment_transpose8([y[:, hf * LANES:(hf + 1) * LANES] for y in ys])
            for t8 in range(8):
                t = 8 * hf + t8
                for ch in range(nchunk):
                    nat_s[tile, pl.ds(t + CHUNK * ch, nb, stride=tok + ROW_PAD), :] = out[t8][ch * nb:(ch + 1) * nb]
        for b in range(nb):
            y_ref[b, :, tile * LANES:(tile + 1) * LANES] = (
                nat_s[tile, b * (tok + ROW_PAD):b * (tok + ROW_PAD) + tok].astype(bf16))


def _ssm_out(u, hf, gb, m, cst_f, cst_b, *, nb):
    G, rows, _ = u.shape
    rb = min(SSM_OUT_ROWS, rows)
    nchunk = rb // nb
    tok = nchunk * CHUNK
    blk = lambda i: (0, i, 0)
    cst3 = lambda i: (0, 0, 0)
    return pl.pallas_call(
        functools.partial(_ssm_out_kernel, nb=nb, nchunk=nchunk),
        grid=(rows // rb,),
        in_specs=[
            pl.BlockSpec((G, rb, CHUNK_LANES), blk),
            pl.BlockSpec((G, rb, LANES), blk),
            pl.BlockSpec((G, rb, LANES), blk),
            pl.BlockSpec(m.shape, cst3),
            pl.BlockSpec(cst_f.shape, cst3),
            pl.BlockSpec(cst_b.shape, cst3),
        ],
        out_specs=pl.BlockSpec((nb, tok, D_SSM), blk),
        out_shape=jax.ShapeDtypeStruct((nb, (rows // nb) * CHUNK, D_SSM), bf16),
        scratch_shapes=[pltpu.VMEM((D_SSM // LANES, nb * (tok + ROW_PAD), LANES), f32)],
        compiler_params=_params("parallel"),
        name="ssm_out",
    )(u, hf, gb, m, cst_f, cst_b)


def _ssm_tables(a_re, a_im, log_dt, b_re, b_im, c_re, c_im, d_skip):
    T, G, P, C = CHUNK, N_GROUPS, STATE, SSM_GROUP
    hi = lax.Precision.HIGHEST
    a_re, a_im, log_dt = a_re.astype(f32), a_im.astype(f32), log_dt.astype(f32)
    dt = jnp.exp(log_dt)[..., None]
    H = T // 2
    ks = jnp.arange(-H, T + 1, dtype=f32)[:, None, None, None]
    mag = jnp.exp(ks * (a_re * dt))
    pw_re = mag * jnp.cos(ks * (a_im * dt))
    pw_im = mag * jnp.sin(ks * (a_im * dt))
    n_re, n_im = pw_re[H + 1] - 1.0, pw_im[H + 1]
    den = a_re * a_re + a_im * a_im
    co_re = (n_re * a_re + n_im * a_im) / den
    co_im = (n_im * a_re - n_re * a_im) / den
    b_re, b_im = b_re.astype(f32), b_im.astype(f32)
    bb_re = co_re[..., None] * b_re - co_im[..., None] * b_im
    bb_im = co_re[..., None] * b_im + co_im[..., None] * b_re
    c_re, c_im = c_re.astype(f32), c_im.astype(f32)

    def xtab(n, pr, pi):
        re = pr[..., None] * bb_re[n][None] - pi[..., None] * bb_im[n][None]
        im = pr[..., None] * bb_im[n][None] + pi[..., None] * bb_re[n][None]
        return (re.transpose(1, 0, 3, 2).reshape(G, T * C, P), im.transpose(1, 0, 3, 2).reshape(G, T * C, P))

    def ytab(n, pr, pi):
        re = c_re[n][None] * pr[:, :, None, :] - c_im[n][None] * pi[:, :, None, :]
        im = c_re[n][None] * pi[:, :, None, :] + c_im[n][None] * pr[:, :, None, :]
        return (re.transpose(1, 3, 0, 2).reshape(G, P, T * C), im.transpose(1, 3, 0, 2).reshape(G, P, T * C))

    def lag_kernels(x, y):
        return (jnp.einsum('gjp,gpt->gjt', x[0], y[0], precision=hi)
                - jnp.einsum('gjp,gpt->gjt', x[1], y[1], precision=hi))

    def powers(n, lo, reverse=False):
        pr, pi = pw_re[lo + H:lo + H + T, n], pw_im[lo + H:lo + H + T, n]
        return (pr[::-1], pi[::-1]) if reverse else (pr, pi)

    kf = lag_kernels(xtab(0, *powers(0, -H, reverse=True)), ytab(0, *powers(0, 1 - H)))
    kb = lag_kernels(xtab(1, *powers(1, -H)), ytab(1, *powers(1, 1 - H, reverse=True)))
    step = jnp.arange(T * C) // C
    causal = step[None, :] >= step[:, None]
    anti = step[None, :] <= step[:, None]
    d_diag = jnp.tile(d_skip.astype(f32).reshape(G, 1, C), (1, T, 1)).reshape(G, 1, T * C)
    m = (jnp.where(causal[None], kf, 0.0) + jnp.where(anti[None], kb, 0.0)
         + jnp.eye(T * C, dtype=f32)[None] * d_diag)

    xf = xtab(0, *powers(0, 0, reverse=True))
    xb = xtab(1, *powers(1, 0))
    yf = ytab(0, *powers(0, 1))
    yb = ytab(1, *powers(1, 1, reverse=True))

    bst_f = jnp.concatenate([xf[0], xf[1], xf[1], xf[0]], axis=-1)
    bst_b = jnp.concatenate([xb[0], xb[1], xb[1], xb[0]], axis=-1)
    cst_f = jnp.concatenate([yf[0], -yf[1]], axis=1)
    cst_b = jnp.concatenate([yb[0], -yb[1]], axis=1)
    ar, ai = pw_re[H + T], pw_im[H + T]
    coef = jnp.stack([jnp.concatenate([ar, ar], -1),
                      jnp.concatenate([-ai, ai], -1),
                      jnp.concatenate([ai, -ai], -1)], axis=2)
    return m.astype(bf16), bst_f.astype(bf16), bst_b.astype(bf16), cst_f.astype(bf16), cst_b.astype(bf16), coef


def _attn_kernel(lam_ref, qt_ref, k_ref, vt_ref, g_ref, za_ref, o_ref,
                 qbd_ref, acc_ref, p_ref, m_ref, a_ref, *, nkv, bq, bk):
    lam = lam_ref[0]
    half = HEAD_DIM
    ns = bq // LANES
    dv = 2 * HEAD_DIM
    zero = jnp.zeros((half, LANES), bf16)
    for st in range(ns):
        qs = qt_ref[0, :, st * LANES:(st + 1) * LANES]
        qbd_ref[st, 0:half, 0:LANES] = qs[0:half]
        qbd_ref[st, 0:half, LANES:2 * LANES] = zero
        qbd_ref[st, half:2 * half, 0:LANES] = zero
        qbd_ref[st, half:2 * half, LANES:2 * LANES] = qs[half:2 * half]
    acc_ref[...] = jnp.zeros_like(acc_ref)
    p_ref[...] = jnp.zeros_like(p_ref)
    a_ref[...] = jnp.ones_like(a_ref)
    m_ref[...] = jnp.full(m_ref.shape, -jnp.inf, f32)

    def values(j):
        return jnp.concatenate([vt_ref[0, 2 * j], vt_ref[0, 2 * j + 1]], axis=1)

    nsub = F32_ROWS

    def fold(st, vb):
        acc = acc_ref[st].reshape(V_ROWS // nsub, nsub, 2 * LANES) * a_ref[st][None]
        acc_ref[st] = acc.reshape(V_ROWS, 2 * LANES) + _dot(vb, p_ref[st])

    def colmax(s):
        mx = jnp.max(s.reshape(-1, nsub, 2 * LANES), axis=0)
        for sh in (4, 2, 1):
            mx = jnp.maximum(mx, pltpu.roll(mx, sh, 0))
        return mx

    def probs(s, m):
        x = s.reshape(-1, nsub, 2 * LANES) - m[None]
        return jnp.exp2(x.reshape(s.shape).astype(bf16))

    def body(j, carry):
        off = pl.multiple_of(j * bk, bk)
        ka = k_ref[0, pl.ds(off, bk // 2), :]
        kb = k_ref[0, pl.ds(off + bk // 2, bk // 2), :]
        vb = values(jnp.maximum(j - 1, 0))
        for st in range(ns):
            fold(st, vb)
            m_old = m_ref[st]
            s_a = _dot(ka, qbd_ref[st])
            m_a = jnp.maximum(m_old, colmax(s_a))
            p_a = probs(s_a, m_a)
            s_b = _dot(kb, qbd_ref[st])
            m_b = jnp.maximum(m_a, colmax(s_b))
            p_b = probs(s_b, m_b)
            corr = jnp.exp2(m_a - m_b)
            corr = jnp.concatenate([corr, corr], axis=0).astype(bf16)
            p_a = p_a.reshape(-1, BF16_ROWS, 2 * LANES) * corr[None]
            p_ref[st, 0:bk // 2] = p_a.reshape(bk // 2, 2 * LANES)
            p_ref[st, bk // 2:bk] = p_b
            a_ref[st] = jnp.exp2(m_old - m_b)
            m_ref[st] = m_b
        return carry

    lax.fori_loop(0, nkv, body, 0, unroll=min(ATTN_UNROLL, nkv))

    vb_last = values(nkv - 1)
    for st in range(ns):
        fold(st, vb_last)
        acc = acc_ref[st, 0:dv]
        inv = 1.0 / acc_ref[st, dv:dv + 1]
        ot = acc[:, :LANES] * inv[:, :LANES] - lam * (acc[:, LANES:] * inv[:, LANES:])
        o = ot.T
        y = o * lax.rsqrt(jnp.mean(o * o, axis=-1, keepdims=True) + SUBLN_EPS) * g_ref[...] * (1.0 - LAM_INIT)
        za = za_ref[0, st * LANES:(st + 1) * LANES, :].astype(f32)
        o_ref[0, st * LANES:(st + 1) * LANES, :] = (y * (za * jax.nn.sigmoid(za))).astype(bf16)


def _attention(lam, qt, k, vt, subln_g, za, *, bq, bk):
    B, L, _ = k.shape
    nkv = L // bk
    nvt = vt.shape[1]
    return pl.pallas_call(
        functools.partial(_attn_kernel, nkv=nkv, bq=bq, bk=bk),
        grid=(B, N_HEADS, L // bq),
        in_specs=[
            pl.BlockSpec(memory_space=pltpu.SMEM),
            pl.BlockSpec((1, 2 * HEAD_DIM, bq), lambda b, h, i: (b, h, i)),
            pl.BlockSpec((1, L, 2 * HEAD_DIM), lambda b, h, i: (b, 0, h)),
            pl.BlockSpec((1, nvt, V_ROWS, bk // 2), lambda b, h, i: (b, 0, h, 0)),
            pl.BlockSpec((1, 2 * HEAD_DIM), lambda b, h, i: (0, 0)),
            pl.BlockSpec((1, bq, 2 * HEAD_DIM), lambda b, h, i: (b, i, h)),
        ],
        out_specs=pl.BlockSpec((1, bq, 2 * HEAD_DIM), lambda b, h, i: (b, i, h)),
        out_shape=jax.ShapeDtypeStruct((B, L, D_ATTN), bf16),
        scratch_shapes=[
            pltpu.VMEM((bq // LANES, 2 * HEAD_DIM, 2 * LANES), bf16),
            pltpu.VMEM((bq // LANES, V_ROWS, 2 * LANES), f32),
            pltpu.VMEM((bq // LANES, bk, 2 * LANES), bf16),
            pltpu.VMEM((bq // LANES, F32_ROWS, 2 * LANES), f32),
            pltpu.VMEM((bq // LANES, F32_ROWS, 2 * LANES), f32),
        ],
        compiler_params=_params("parallel", "parallel", "arbitrary"),
        name="attention",
    )(lam, qt, k, vt, subln_g, za)


def _out_proj_kernel(x_ref, yssm_ref, zs_ref, ya_ref, g_ref, wg_ref, wglu_ref, bglu_ref,
                     wb_ref, wout_ref, fg_ref, o_ref):
    tm = x_ref.shape[1]
    hm = tm // OUT_PROJ_CHAINS
    for c in range(OUT_PROJ_CHAINS):
        rs = slice(c * hm, (c + 1) * hm)
        x = x_ref[0, rs]
        r = lax.rsqrt(jnp.mean(x * x, axis=-1, keepdims=True) + NORM_EPS)
        h = (x * r * g_ref[...]).astype(bf16)

        ys = jax.nn.gelu(yssm_ref[0, rs].astype(f32))
        ys = ys * jax.nn.sigmoid(_dot(ys.astype(bf16), wglu_ref[...]) + bglu_ref[...])
        zs = zs_ref[0, rs].astype(f32)
        ys = ys * (zs * jax.nn.sigmoid(zs))

        ps = _dot(ys.astype(bf16), wb_ref[0])
        merged = jax.nn.sigmoid(_dot(h, wg_ref[:, 0:D_MODEL])) * ps
        pa = _dot(ya_ref[0, rs], wb_ref[1])
        merged = merged + jax.nn.sigmoid(_dot(h, wg_ref[:, D_MODEL:2 * D_MODEL])) * pa
        out = x + _dot(merged.astype(bf16), wout_ref[...])
        o_ref[0, rs] = out * lax.rsqrt(jnp.mean(out * out, axis=-1, keepdims=True) + NORM_EPS) * fg_ref[...]


def _out_proj(x, yssm, zs, ya, norm_g, wg, wglu, bglu, wb, wout, final_g, *, tm):
    B, L, _ = x.shape
    tok = lambda b, i: (b, i, 0)
    cst = lambda b, i: (0, 0)
    return pl.pallas_call(
        _out_proj_kernel,
        grid=(B, L // tm),
        in_specs=[
            pl.BlockSpec((1, tm, D_MODEL), tok),
            pl.BlockSpec((1, tm, 512), tok),
            pl.BlockSpec((1, tm, 512), tok),
            pl.BlockSpec((1, tm, 512), tok),
            pl.BlockSpec((1, D_MODEL), cst),
            pl.BlockSpec(wg.shape, cst),
            pl.BlockSpec(wglu.shape, cst),
            pl.BlockSpec((1, D_SSM), cst),
            pl.BlockSpec(wb.shape, lambda b, i: (0, 0, 0)),
            pl.BlockSpec(wout.shape, cst),
            pl.BlockSpec((1, D_MODEL), cst),
        ],
        out_specs=pl.BlockSpec((1, tm, D_MODEL), tok),
        out_shape=jax.ShapeDtypeStruct((B, L, D_MODEL), x.dtype),
        compiler_params=_params("parallel", "parallel"),
        name="out_proj",
    )(x, yssm, zs, ya, norm_g, wg, wglu, bglu, wb, wout, final_g)


def _rotary_tables(L):
    half = HEAD_DIM // 2
    inv_freq = 1.0 / (ROPE_THETA ** (jnp.arange(0, half, dtype=f32) * 2.0 / HEAD_DIM))
    ang = jnp.arange(L, dtype=f32)[:, None] * inv_freq[None, :]
    cos, sin = jnp.cos(ang), jnp.sin(ang)
    ck = jnp.tile(cos, (1, LANES // half))
    sk = jnp.tile(sin, (1, LANES // half))
    scale = math.log2(math.e) / math.sqrt(HEAD_DIM)
    return ck, sk, cos.T * scale, sin.T * scale


def _trunk(x, w):
    B, L, _ = x.shape
    assert B == 8 and L % ATTN_BQ == 0, "scan state vregs hold one row per batch element"
    ck, sk, cq, sq = _rotary_tables(L)
    u, zs, k, za, qt, vt = _in_proj(x, w["norm_g"], w["wa"], w["wbt"], ck, sk, cq, sq, tb=PROJ_TOKENS)
    hf, gb = _ssm_state(u, w["bst_f"], w["bst_b"], w["coef"], nb=B)
    yssm = _ssm_out(u, hf, gb, w["m"], w["cst_f"], w["cst_b"], nb=B)
    ya = _attention(w["lam"], qt, k, vt, w["subln_g"], za, bq=ATTN_BQ, bk=ATTN_BK)
    return _out_proj(x, yssm, zs, ya, w["norm_g"], w["wg"], w["wglu"], w["bglu"], w["wb"], w["wout"],
                     w["final_g"], tm=TOKEN_BLOCK)


def _rotate_half_columns(wk):
    d = wk.shape[0]
    w4 = wk.reshape(d, D_ATTN // HEAD_DIM, 2, HEAD_DIM // 2)
    return jnp.stack([-w4[:, :, 1], w4[:, :, 0]], axis=2).reshape(d, D_ATTN)


def kernel(x_prompt, x_sample, norm_g, w_in, ssm_a_re, ssm_a_im, ssm_log_dt, ssm_b_re, ssm_b_im, ssm_c_re, ssm_c_im, ssm_d, w_glu, b_glu, lambda_q1, lambda_k1, lambda_q2, lambda_k2, subln_g, w_branch, w_out, final_g):
    li = 0
    wi = w_in[li].astype(f32)
    w_xs, w_zs = wi[:, 0:512], wi[:, 512:1024]
    w_q, w_k, w_v, w_za = wi[:, 1024:1536], wi[:, 1536:2048], wi[:, 2048:2560], wi[:, 2560:3072]
    m, bst_f, bst_b, cst_f, cst_b, coef = _ssm_tables(
        ssm_a_re[li], ssm_a_im[li], ssm_log_dt[li], ssm_b_re[li], ssm_b_im[li],
        ssm_c_re[li], ssm_c_im[li], ssm_d[li])
    nb = x_prompt.shape[0]
    lam = (jnp.exp(jnp.sum(lambda_q1[li].astype(f32) * lambda_k1[li].astype(f32)))
           - jnp.exp(jnp.sum(lambda_q2[li].astype(f32) * lambda_k2[li].astype(f32))) + LAM_INIT)
    w = dict(
        norm_g=norm_g[li].astype(f32).reshape(1, D_MODEL),
        wa=jnp.concatenate([w_xs, w_zs, w_k, _rotate_half_columns(w_k), w_za], axis=1).astype(bf16),
        wbt=jnp.concatenate([w_q, w_v], axis=1).T.astype(bf16),
        wg=wi[:, 3072:5120].astype(bf16),
        m=m, bst_f=bst_f, bst_b=bst_b, cst_f=cst_f, cst_b=cst_b,
        coef=jnp.broadcast_to(coef[:, :, :, None, :], (2, N_GROUPS, 3, nb, LANES)),
        lam=lam.reshape(1).astype(f32),
        subln_g=subln_g[li].astype(f32).reshape(1, 2 * HEAD_DIM),
        wglu=w_glu[li].astype(bf16),
        bglu=b_glu[li].astype(f32).reshape(1, D_SSM),
        wb=w_branch[li].astype(bf16),
        wout=w_out[li].astype(bf16),
        final_g=final_g.astype(f32).reshape(1, D_MODEL),
    )
    return (_trunk(x_prompt, w), _trunk(x_sample, w))
```

```python
import functools
import math

import jax
import jax.numpy as jnp
from jax import lax
from jax.experimental import pallas as pl
from jax.experimental.pallas import tpu as pltpu

D_MODEL = 1024
D_SSM = 512
SSM_GROUP = 16
N_GROUPS = 32
STATE = 64
D_ATTN = 512
N_HEADS = 4
HEAD_DIM = 64
ROPE_THETA = 10000.0
NORM_EPS = 1e-6
SUBLN_EPS = 1e-5
LAM_INIT = 0.8 - 0.6 * math.exp(-0.3 * 0)

CHUNK = 16
CHUNK_LANES = CHUNK * SSM_GROUP
LANES = 128
VMEM_LIMIT = 56 * 1024 * 1024

TOKEN_BLOCK = 1024
OUT_PROJ_CHAINS = 4
ATTN_BQ = 2048
ATTN_UNROLL = 16
ATTN_BK = 256
PROJ_TOKENS = ATTN_BK // 2
F32_ROWS = 8
ROW_PAD = 4
BF16_ROWS = 16
V_ROWS = 2 * HEAD_DIM + BF16_ROWS
SSM_SEG_CHUNKS = 16
SSM_GROUP_BLOCK = 4
SSM_OUT_ROWS = 128

f32 = jnp.float32
bf16 = jnp.bfloat16


def _params(*sem):
    return pltpu.CompilerParams(dimension_semantics=sem, vmem_limit_bytes=VMEM_LIMIT)


def _dot(a, b):
    return jnp.dot(a, b, preferred_element_type=f32)


def _dot_nt(a, b):
    return lax.dot_general(a, b, (((1,), (1,)), ((), ())), preferred_element_type=f32)


def _segment_transpose8(vs):
    n = len(vs)
    slot = lax.broadcasted_iota(jnp.int32, vs[0].shape, 1) // SSM_GROUP
    skew = [v if t == 0 else pltpu.roll(v, t * SSM_GROUP, 1) for t, v in enumerate(vs)]
    out = []
    for g in range(n):
        acc = skew[(0 - g) % n]
        for s in range(1, n):
            acc = jnp.where(slot == s, skew[(s - g) % n], acc)
        out.append(acc if g == 0 else pltpu.roll(acc, LANES - g * SSM_GROUP, 1))
    return out


def _in_proj_kernel(x_ref, wa_ref, wbt_ref, ck_ref, sk_ref, cq_ref, sq_ref,
                    u_ref, zs_ref, k_ref, za_ref, qt_ref, vt_ref, xs_s, *, nb, tb):
    rows = nb * tb
    x = x_ref[...].reshape(rows, D_MODEL)
    r = lax.rsqrt(jnp.mean(x * x, axis=-1, keepdims=True) + NORM_EPS)
    h = (x * r).astype(bf16)

    xs = _dot(h, wa_ref[:, 0:D_SSM])
    for tile in range(D_SSM // LANES):
        for b in range(nb):
            xs_s[tile, b * (tb + ROW_PAD):b * (tb + ROW_PAD) + tb] = xs[b * tb:(b + 1) * tb, tile * LANES:(tile + 1) * LANES]
    zs_ref[...] = _dot(h, wa_ref[:, 512:1024]).astype(bf16).reshape(nb, tb, D_SSM)
    kk = _dot(h, wa_ref[:, 1024:1536]).reshape(nb, tb, D_ATTN)
    kr = _dot(h, wa_ref[:, 1536:2048]).reshape(nb, tb, D_ATTN)
    ck = ck_ref[...]
    sk = sk_ref[...]
    for j in range(D_ATTN // LANES):
        sl = slice(j * LANES, (j + 1) * LANES)
        k_ref[:, :, sl] = (kk[:, :, sl] * ck + kr[:, :, sl] * sk).astype(bf16)
    za_ref[...] = _dot(h, wa_ref[:, 2048:2560]).astype(bf16).reshape(nb, tb, D_ATTN)

    pq = _dot_nt(wbt_ref[0:512, :], h)
    cq = jnp.tile(cq_ref[...], (1, nb))
    sq = jnp.tile(sq_ref[...], (1, nb))
    half = HEAD_DIM // 2
    for hb in range(D_ATTN // HEAD_DIM):
        x1 = pq[hb * HEAD_DIM:hb * HEAD_DIM + half]
        x2 = pq[hb * HEAD_DIM + half:(hb + 1) * HEAD_DIM]
        q1 = (x1 * cq - x2 * sq).astype(bf16)
        q2 = (x2 * cq + x1 * sq).astype(bf16)
        for b in range(nb):
            qt_ref[b, hb * HEAD_DIM:hb * HEAD_DIM + half, :] = q1[:, b * tb:(b + 1) * tb]
            qt_ref[b, hb * HEAD_DIM + half:(hb + 1) * HEAD_DIM, :] = q2[:, b * tb:(b + 1) * tb]
    pv = _dot_nt(wbt_ref[512:1024, :], h).astype(bf16)
    dv = 2 * HEAD_DIM
    ones = jnp.ones((V_ROWS - dv, tb), bf16)
    for b in range(nb):
        for hd in range(N_HEADS):
            vt_ref[b, 0, hd * V_ROWS:hd * V_ROWS + dv, :] = pv[hd * dv:(hd + 1) * dv, b * tb:(b + 1) * tb]
            vt_ref[b, 0, hd * V_ROWS + dv:(hd + 1) * V_ROWS, :] = ones

    nchunk = tb // CHUNK
    for tile in range(D_SSM // LANES):
        for hf in range(CHUNK_LANES // LANES):
            vs = []
            for t8 in range(8):
                t = 8 * hf + t8
                vs.append(jnp.concatenate(
                    [xs_s[tile, pl.ds(t + CHUNK * ch, nb, stride=tb + ROW_PAD), :] for ch in range(nchunk)],
                    axis=0))
            out = _segment_transpose8(vs)
            for g8 in range(8):
                u_ref[8 * tile + g8, :, hf * LANES:(hf + 1) * LANES] = out[g8].astype(bf16)


def _in_proj(x, wa, wbt, ck, sk, cq, sq, *, tb):
    B, L, _ = x.shape
    nchunk = tb // CHUNK
    tok = lambda i: (0, i, 0)
    cst = lambda i: (0, 0)
    out_tok = jax.ShapeDtypeStruct((B, L, 512), bf16)
    return pl.pallas_call(
        functools.partial(_in_proj_kernel, nb=B, tb=tb),
        grid=(L // tb,),
        in_specs=[
            pl.BlockSpec((B, tb, D_MODEL), tok),
            pl.BlockSpec(wa.shape, cst),
            pl.BlockSpec(wbt.shape, cst),
            pl.BlockSpec((tb, LANES), lambda i: (i, 0)),
            pl.BlockSpec((tb, LANES), lambda i: (i, 0)),
            pl.BlockSpec((HEAD_DIM // 2, tb), lambda i: (0, i)),
            pl.BlockSpec((HEAD_DIM // 2, tb), lambda i: (0, i)),
        ],
        out_specs=[
            pl.BlockSpec((N_GROUPS, nchunk * B, CHUNK_LANES), tok),
            pl.BlockSpec((B, tb, 512), tok),
            pl.BlockSpec((B, tb, 512), tok),
            pl.BlockSpec((B, tb, 512), tok),
            pl.BlockSpec((B, 512, tb), lambda i: (0, 0, i)),
            pl.BlockSpec((B, 1, N_HEADS * V_ROWS, tb), lambda i: (0, i, 0, 0)),
        ],
        out_shape=[jax.ShapeDtypeStruct((N_GROUPS, (L // CHUNK) * B, CHUNK_LANES), bf16),
                   out_tok, out_tok, out_tok,
                   jax.ShapeDtypeStruct((B, 512, L), bf16),
                   jax.ShapeDtypeStruct((B, L // tb, N_HEADS * V_ROWS, tb), bf16)],
        scratch_shapes=[pltpu.VMEM((D_SSM // LANES, B * (tb + ROW_PAD), LANES), f32)],
        compiler_params=_params("parallel"),
        name="in_proj",
    )(x, wa, wbt, ck, sk, cq, sq)


def _ssm_state_kernel(uf_ref, ub_ref, bf_ref, bb_ref, coef_ref, hf_ref, gb_ref, s_ref, st_ref,
                      *, nc, nb, gblk):
    @pl.when(pl.program_id(0) == 0)
    def _():
        st_ref[...] = jnp.zeros_like(st_ref)

    for g0 in range(0, N_GROUPS, gblk):
        for gi in range(gblk):
            s_ref[0, gi] = _dot(uf_ref[g0 + gi], bf_ref[g0 + gi])
            s_ref[1, gi] = _dot(ub_ref[g0 + gi], bb_ref[g0 + gi])
        gs = slice(g0, g0 + gblk)
        af1, af2, af3 = coef_ref[0, gs, 0], coef_ref[0, gs, 1], coef_ref[0, gs, 2]
        ab1, ab2, ab3 = coef_ref[1, gs, 0], coef_ref[1, gs, 1], coef_ref[1, gs, 2]

        hf, wf, hb, wb = st_ref[0, 0, gs], st_ref[0, 1, gs], st_ref[1, 0, gs], st_ref[1, 1, gs]
        held_f = held_b = None
        for i in range(nc):
            cf, cb = i, nc - 1 - i
            if i % 2 == 0:
                held_f, held_b = hf, hb
            else:
                hf_ref[gs, (cf - 1) * nb:(cf + 1) * nb, :] = jnp.concatenate([held_f, hf], axis=1).astype(bf16)
                gb_ref[gs, cb * nb:(cb + 2) * nb, :] = jnp.concatenate([hb, held_b], axis=1).astype(bf16)
            sf = s_ref[0, :, cf * nb:(cf + 1) * nb, :]
            sb = s_ref[1, :, cb * nb:(cb + 1) * nb, :]
            hf, wf = (af1 * hf + af2 * wf + sf[..., :LANES], af1 * wf + af3 * hf + sf[..., LANES:])
            hb, wb = (ab1 * hb + ab2 * wb + sb[..., :LANES], ab1 * wb + ab3 * hb + sb[..., LANES:])
        st_ref[0, 0, gs] = hf
        st_ref[0, 1, gs] = wf
        st_ref[1, 0, gs] = hb
        st_ref[1, 1, gs] = wb


def _ssm_state(u, bst_f, bst_b, coef, *, nb):
    G, rows, _ = u.shape
    nc = SSM_SEG_CHUNKS
    seg_rows = nc * nb
    nseg = rows // seg_rows
    gblk = SSM_GROUP_BLOCK
    fwd = lambda i: (0, i, 0)
    bwd = lambda i: (0, nseg - 1 - i, 0)
    cst3 = lambda i: (0, 0, 0)
    return pl.pallas_call(
        functools.partial(_ssm_state_kernel, nc=nc, nb=nb, gblk=gblk),
        grid=(nseg,),
        in_specs=[
            pl.BlockSpec((G, seg_rows, CHUNK_LANES), fwd),
            pl.BlockSpec((G, seg_rows, CHUNK_LANES), bwd),
            pl.BlockSpec(bst_f.shape, cst3),
            pl.BlockSpec(bst_b.shape, cst3),
            pl.BlockSpec(coef.shape, lambda i: (0, 0, 0, 0, 0)),
        ],
        out_specs=[
            pl.BlockSpec((G, seg_rows, LANES), fwd),
            pl.BlockSpec((G, seg_rows, LANES), bwd),
        ],
        out_shape=[jax.ShapeDtypeStruct((G, rows, LANES), bf16),
                   jax.ShapeDtypeStruct((G, rows, LANES), bf16)],
        scratch_shapes=[
            pltpu.VMEM((2, gblk, seg_rows, CHUNK_LANES), f32),
            pltpu.VMEM((2, 2, G, nb, LANES), f32),
        ],
        compiler_params=_params("arbitrary"),
        name="ssm_state",
    )(u, u, bst_f, bst_b, coef)


def _ssm_out_kernel(u_ref, hf_ref, gb_ref, m_ref, cf_ref, cb_ref, y_ref, nat_s, *, nb, nchunk):
    tok = nchunk * CHUNK
    for tile in range(D_SSM // LANES):
        ys = []
        for g8 in range(8):
            g = 8 * tile + g8
            y = _dot(u_ref[g], m_ref[g])
            y = y + _dot(hf_ref[g], cf_ref[g])
            y = y + _dot(gb_ref[g], cb_ref[g])
            ys.append(y)
        for hf in range(CHUNK_LANES // LANES):
            out = _segment_transpose8([y[:, hf * LANES:(hf + 1) * LANES] for y in ys])
            for t8 in range(8):
                t = 8 * hf + t8
                for ch in range(nchunk):
                    nat_s[tile, pl.ds(t + CHUNK * ch, nb, stride=tok + ROW_PAD), :] = out[t8][ch * nb:(ch + 1) * nb]
        for b in range(nb):
            y_ref[b, :, tile * LANES:(tile + 1) * LANES] = (
                nat_s[tile, b * (tok + ROW_PAD):b * (tok + ROW_PAD) + tok].astype(bf16))


def _ssm_out(u, hf, gb, m, cst_f, cst_b, *, nb):
    G, rows, _ = u.shape
    rb = min(SSM_OUT_ROWS, rows)
    nchunk = rb // nb
    tok = nchunk * CHUNK
    blk = lambda i: (0, i, 0)
    cst3 = lambda i: (0, 0, 0)
    return pl.pallas_call(
        functools.partial(_ssm_out_kernel, nb=nb, nchunk=nchunk),
        grid=(rows // rb,),
        in_specs=[
            pl.BlockSpec((G, rb, CHUNK_LANES), blk),
            pl.BlockSpec((G, rb, LANES), blk),
            pl.BlockSpec((G, rb, LANES), blk),
            pl.BlockSpec(m.shape, cst3),
            pl.BlockSpec(cst_f.shape, cst3),
            pl.BlockSpec(cst_b.shape, cst3),
        ],
        out_specs=pl.BlockSpec((nb, tok, D_SSM), blk),
        out_shape=jax.ShapeDtypeStruct((nb, (rows // nb) * CHUNK, D_SSM), bf16),
        scratch_shapes=[pltpu.VMEM((D_SSM // LANES, nb * (tok + ROW_PAD), LANES), f32)],
        compiler_params=_params("parallel"),
        name="ssm_out",
    )(u, hf, gb, m, cst_f, cst_b)


def _ssm_tables(a_re, a_im, log_dt, b_re, b_im, c_re, c_im, d_skip):
    T, G, P, C = CHUNK, N_GROUPS, STATE, SSM_GROUP
    hi = lax.Precision.HIGHEST
    a_re, a_im, log_dt = a_re.astype(f32), a_im.astype(f32), log_dt.astype(f32)
    dt = jnp.exp(log_dt)[..., None]
    H = T // 2
    ks = jnp.arange(-H, T + 1, dtype=f32)[:, None, None, None]
    mag = jnp.exp(ks * (a_re * dt))
    pw_re = mag * jnp.cos(ks * (a_im * dt))
    pw_im = mag * jnp.sin(ks * (a_im * dt))
    n_re, n_im = pw_re[H + 1] - 1.0, pw_im[H + 1]
    den = a_re * a_re + a_im * a_im
    co_re = (n_re * a_re + n_im * a_im) / den
    co_im = (n_im * a_re - n_re * a_im) / den
    b_re, b_im = b_re.astype(f32), b_im.astype(f32)
    bb_re = co_re[..., None] * b_re - co_im[..., None] * b_im
    bb_im = co_re[..., None] * b_im + co_im[..., None] * b_re
    c_re, c_im = c_re.astype(f32), c_im.astype(f32)

    def xtab(n, pr, pi):
        re = pr[..., None] * bb_re[n][None] - pi[..., None] * bb_im[n][None]
        im = pr[..., None] * bb_im[n][None] + pi[..., None] * bb_re[n][None]
        return (re.transpose(1, 0, 3, 2).reshape(G, T * C, P), im.transpose(1, 0, 3, 2).reshape(G, T * C, P))

    def ytab(n, pr, pi):
        re = c_re[n][None] * pr[:, :, None, :] - c_im[n][None] * pi[:, :, None, :]
        im = c_re[n][None] * pi[:, :, None, :] + c_im[n][None] * pr[:, :, None, :]
        return (re.transpose(1, 3, 0, 2).reshape(G, P, T * C), im.transpose(1, 3, 0, 2).reshape(G, P, T * C))

    def lag_kernels(x, y):
        return (jnp.einsum('gjp,gpt->gjt', x[0], y[0], precision=hi)
                - jnp.einsum('gjp,gpt->gjt', x[1], y[1], precision=hi))

    def powers(n, lo, reverse=False):
        pr, pi = pw_re[lo + H:lo + H + T, n], pw_im[lo + H:lo + H + T, n]
        return (pr[::-1], pi[::-1]) if reverse else (pr, pi)

    kf = lag_kernels(xtab(0, *powers(0, -H, reverse=True)), ytab(0, *powers(0, 1 - H)))
    kb = lag_kernels(xtab(1, *powers(1, -H)), ytab(1, *powers(1, 1 - H, reverse=True)))
    step = jnp.arange(T * C) // C
    causal = step[None, :] >= step[:, None]
    anti = step[None, :] <= step[:, None]
    d_diag = jnp.tile(d_skip.astype(f32).reshape(G, 1, C), (1, T, 1)).reshape(G, 1, T * C)
    m = (jnp.where(causal[None], kf, 0.0) + jnp.where(anti[None], kb, 0.0)
         + jnp.eye(T * C, dtype=f32)[None] * d_diag)

    xf = xtab(0, *powers(0, 0, reverse=True))
    xb = xtab(1, *powers(1, 0))
    yf = ytab(0, *powers(0, 1))
    yb = ytab(1, *powers(1, 1, reverse=True))

    bst_f = jnp.concatenate([xf[0], xf[1], xf[1], xf[0]], axis=-1)
    bst_b = jnp.concatenate([xb[0], xb[1], xb[1], xb[0]], axis=-1)
    cst_f = jnp.concatenate([yf[0], -yf[1]], axis=1)
    cst_b = jnp.concatenate([yb[0], -yb[1]], axis=1)
    ar, ai = pw_re[H + T], pw_im[H + T]
    coef = jnp.stack([jnp.concatenate([ar, ar], -1),
                      jnp.concatenate([-ai, ai], -1),
                      jnp.concatenate([ai, -ai], -1)], axis=2)
    return m.astype(bf16), bst_f.astype(bf16), bst_b.astype(bf16), cst_f.astype(bf16), cst_b.astype(bf16), coef


def _attn_kernel(lam_ref, qt_ref, k_ref, vt_ref, g_ref, za_ref, o_ref,
                 qbd_ref, acc_ref, p_ref, m_ref, a_ref, *, nkv, bq, bk):
    lam = lam_ref[0]
    half = HEAD_DIM
    ns = bq // LANES
    dv = 2 * HEAD_DIM
    zero = jnp.zeros((half, LANES), bf16)
    for st in range(ns):
        qs = qt_ref[0, :, st * LANES:(st + 1) * LANES]
        qbd_ref[st, 0:half, 0:LANES] = qs[0:half]
        qbd_ref[st, 0:half, LANES:2 * LANES] = zero
        qbd_ref[st, half:2 * half, 0:LANES] = zero
        qbd_ref[st, half:2 * half, LANES:2 * LANES] = qs[half:2 * half]
    acc_ref[...] = jnp.zeros_like(acc_ref)
    p_ref[...] = jnp.zeros_like(p_ref)
    a_ref[...] = jnp.ones_like(a_ref)
    m_ref[...] = jnp.full(m_ref.shape, -jnp.inf, f32)

    def values(j):
        return jnp.concatenate([vt_ref[0, 2 * j], vt_ref[0, 2 * j + 1]], axis=1)

    nsub = F32_ROWS

    def fold(st, vb):
        acc = acc_ref[st].reshape(V_ROWS // nsub, nsub, 2 * LANES) * a_ref[st][None]
        acc_ref[st] = acc.reshape(V_ROWS, 2 * LANES) + _dot(vb, p_ref[st])

    def colmax(s):
        mx = jnp.max(s.reshape(-1, nsub, 2 * LANES), axis=0)
        for sh in (4, 2, 1):
            mx = jnp.maximum(mx, pltpu.roll(mx, sh, 0))
        return mx

    def probs(s, m):
        x = s.reshape(-1, nsub, 2 * LANES) - m[None]
        return jnp.exp2(x.reshape(s.shape).astype(bf16))

    def body(j, carry):
        off = pl.multiple_of(j * bk, bk)
        ka = k_ref[0, pl.ds(off, bk // 2), :]
        kb = k_ref[0, pl.ds(off + bk // 2, bk // 2), :]
        vb = values(jnp.maximum(j - 1, 0))
        for st in range(ns):
            fold(st, vb)
            m_old = m_ref[st]
            s_a = _dot(ka, qbd_ref[st])
            m_a = jnp.maximum(m_old, colmax(s_a))
            p_a = probs(s_a, m_a)
            s_b = _dot(kb, qbd_ref[st])
            m_b = jnp.maximum(m_a, colmax(s_b))
            p_b = probs(s_b, m_b)
            corr = jnp.exp2(m_a - m_b)
            corr = jnp.concatenate([corr, corr], axis=0).astype(bf16)
            p_a = p_a.reshape(-1, BF16_ROWS, 2 * LANES) * corr[None]
            p_ref[st, 0:bk // 2] = p_a.reshape(bk // 2, 2 * LANES)
            p_ref[st, bk // 2:bk] = p_b
            a_ref[st] = jnp.exp2(m_old - m_b)
            m_ref[st] = m_b
        return carry

    lax.fori_loop(0, nkv, body, 0, unroll=min(ATTN_UNROLL, nkv))

    vb_last = values(nkv - 1)
    for st in range(ns):
        fold(st, vb_last)
        acc = acc_ref[st, 0:dv]
        inv = 1.0 / acc_ref[st, dv:dv + 1]
        ot = acc[:, :LANES] * inv[:, :LANES] - lam * (acc[:, LANES:] * inv[:, LANES:])
        o = ot.T
        y = o * lax.rsqrt(jnp.mean(o * o, axis=-1, keepdims=True) + SUBLN_EPS) * g_ref[...] * (1.0 - LAM_INIT)
        za = za_ref[0, st * LANES:(st + 1) * LANES, :].astype(f32)
        o_ref[0, st * LANES:(st + 1) * LANES, :] = (y * (za * jax.nn.sigmoid(za))).astype(bf16)


def _attention(lam, qt, k, vt, subln_g, za, *, bq, bk):
    B, L, _ = k.shape
    nkv = L // bk
    nvt = vt.shape[1]
    return pl.pallas_call(
        functools.partial(_attn_kernel, nkv=nkv, bq=bq, bk=bk),
        grid=(B, N_HEADS, L // bq),
        in_specs=[
            pl.BlockSpec(memory_space=pltpu.SMEM),
            pl.BlockSpec((1, 2 * HEAD_DIM, bq), lambda b, h, i: (b, h, i)),
            pl.BlockSpec((1, L, 2 * HEAD_DIM), lambda b, h, i: (b, 0, h)),
            pl.BlockSpec((1, nvt, V_ROWS, bk // 2), lambda b, h, i: (b, 0, h, 0)),
            pl.BlockSpec((1, 2 * HEAD_DIM), lambda b, h, i: (0, 0)),
            pl.BlockSpec((1, bq, 2 * HEAD_DIM), lambda b, h, i: (b, i, h)),
        ],
        out_specs=pl.BlockSpec((1, bq, 2 * HEAD_DIM), lambda b, h, i: (b, i, h)),
        out_shape=jax.ShapeDtypeStruct((B, L, D_ATTN), bf16),
        scratch_shapes=[
            pltpu.VMEM((bq // LANES, 2 * HEAD_DIM, 2 * LANES), bf16),
            pltpu.VMEM((bq // LANES, V_ROWS, 2 * LANES), f32),
            pltpu.VMEM((bq // LANES, bk, 2 * LANES), bf16),
            pltpu.VMEM((bq // LANES, F32_ROWS, 2 * LANES), f32),
            pltpu.VMEM((bq // LANES, F32_ROWS, 2 * LANES), f32),
        ],
        compiler_params=_params("parallel", "parallel", "arbitrary"),
        name="attention",
    )(lam, qt, k, vt, subln_g, za)


def _out_proj_kernel(x_ref, yssm_ref, zs_ref, ya_ref, wg_ref, wglu_ref, bglu_ref,
                     wb_ref, wout_ref, fg_ref, o_ref):
    tm = x_ref.shape[1]
    hm = tm // OUT_PROJ_CHAINS
    for c in range(OUT_PROJ_CHAINS):
        rs = slice(c * hm, (c + 1) * hm)
        x = x_ref[0, rs]
        r = lax.rsqrt(jnp.mean(x * x, axis=-1, keepdims=True) + NORM_EPS)
        h = (x * r).astype(bf16)

        ys = jax.nn.gelu(yssm_ref[0, rs].astype(f32))
        ys = ys * jax.nn.sigmoid(_dot(ys.astype(bf16), wglu_ref[...]) + bglu_ref[...])
        zs = zs_ref[0, rs].astype(f32)
        ys = ys * (zs * jax.nn.sigmoid(zs))

        ps = _dot(ys.astype(bf16), wb_ref[0])
        merged = jax.nn.sigmoid(_dot(h, wg_ref[:, 0:D_MODEL])) * ps
        pa = _dot(ya_ref[0, rs], wb_ref[1])
        merged = merged + jax.nn.sigmoid(_dot(h, wg_ref[:, D_MODEL:2 * D_MODEL])) * pa
        out = x + _dot(merged.astype(bf16), wout_ref[...])
        o_ref[0, rs] = out * lax.rsqrt(jnp.mean(out * out, axis=-1, keepdims=True) + NORM_EPS) * fg_ref[...]


def _out_proj(x, yssm, zs, ya, wg, wglu, bglu, wb, wout, final_g, *, tm):
    B, L, _ = x.shape
    tok = lambda b, i: (b, i, 0)
    cst = lambda b, i: (0, 0)
    return pl.pallas_call(
        _out_proj_kernel,
        grid=(B, L // tm),
        in_specs=[
            pl.BlockSpec((1, tm, D_MODEL), tok),
            pl.BlockSpec((1, tm, 512), tok),
            pl.BlockSpec((1, tm, 512), tok),
            pl.BlockSpec((1, tm, 512), tok),
            pl.BlockSpec(wg.shape, cst),
            pl.BlockSpec(wglu.shape, cst),
            pl.BlockSpec((1, D_SSM), cst),
            pl.BlockSpec(wb.shape, lambda b, i: (0, 0, 0)),
            pl.BlockSpec(wout.shape, cst),
            pl.BlockSpec((1, D_MODEL), cst),
        ],
        out_specs=pl.BlockSpec((1, tm, D_MODEL), tok),
        out_shape=jax.ShapeDtypeStruct((B, L, D_MODEL), x.dtype),
        compiler_params=_params("parallel", "parallel"),
        name="out_proj",
    )(x, yssm, zs, ya, wg, wglu, bglu, wb, wout, final_g)


def _rotary_tables(L):
    half = HEAD_DIM // 2
    inv_freq = 1.0 / (ROPE_THETA ** (jnp.arange(0, half, dtype=f32) * 2.0 / HEAD_DIM))
    ang = jnp.arange(L, dtype=f32)[:, None] * inv_freq[None, :]
    cos, sin = jnp.cos(ang), jnp.sin(ang)
    ck = jnp.tile(cos, (1, LANES // half))
    sk = jnp.tile(sin, (1, LANES // half))
    scale = math.log2(math.e) / math.sqrt(HEAD_DIM)
    return ck, sk, cos.T * scale, sin.T * scale


def _trunk(x, w):
    B, L, _ = x.shape
    assert B == 8 and L % ATTN_BQ == 0, "scan state vregs hold one row per batch element"
    ck, sk, cq, sq = _rotary_tables(L)
    u, zs, k, za, qt, vt = _in_proj(x, w["wa"], w["wbt"], ck, sk, cq, sq, tb=PROJ_TOKENS)
    hf, gb = _ssm_state(u, w["bst_f"], w["bst_b"], w["coef"], nb=B)
    yssm = _ssm_out(u, hf, gb, w["m"], w["cst_f"], w["cst_b"], nb=B)
    ya = _attention(w["lam"], qt, k, vt, w["subln_g"], za, bq=ATTN_BQ, bk=ATTN_BK)
    return _out_proj(x, yssm, zs, ya, w["wg"], w["wglu"], w["bglu"], w["wb"], w["wout"],
                     w["final_g"], tm=TOKEN_BLOCK)


def _rotate_half_columns(wk):
    d = wk.shape[0]
    w4 = wk.reshape(d, D_ATTN // HEAD_DIM, 2, HEAD_DIM // 2)
    return jnp.stack([-w4[:, :, 1], w4[:, :, 0]], axis=2).reshape(d, D_ATTN)


def kernel(x_prompt, x_sample, norm_g, w_in, ssm_a_re, ssm_a_im, ssm_log_dt, ssm_b_re, ssm_b_im, ssm_c_re, ssm_c_im, ssm_d, w_glu, b_glu, lambda_q1, lambda_k1, lambda_q2, lambda_k2, subln_g, w_branch, w_out, final_g):
    li = 0
    wi = w_in[li].astype(f32) * norm_g[li].astype(f32)[:, None]
    w_xs, w_zs = wi[:, 0:512], wi[:, 512:1024]
    w_q, w_k, w_v, w_za = wi[:, 1024:1536], wi[:, 1536:2048], wi[:, 2048:2560], wi[:, 2560:3072]
    m, bst_f, bst_b, cst_f, cst_b, coef = _ssm_tables(
        ssm_a_re[li], ssm_a_im[li], ssm_log_dt[li], ssm_b_re[li], ssm_b_im[li],
        ssm_c_re[li], ssm_c_im[li], ssm_d[li])
    nb = x_prompt.shape[0]
    lam = (jnp.exp(jnp.sum(lambda_q1[li].astype(f32) * lambda_k1[li].astype(f32)))
           - jnp.exp(jnp.sum(lambda_q2[li].astype(f32) * lambda_k2[li].astype(f32))) + LAM_INIT)
    w = dict(
        wa=jnp.concatenate([w_xs, w_zs, w_k, _rotate_half_columns(w_k), w_za], axis=1).astype(bf16),
        wbt=jnp.concatenate([w_q, w_v], axis=1).T.astype(bf16),
        wg=wi[:, 3072:5120].astype(bf16),
        m=m, bst_f=bst_f, bst_b=bst_b, cst_f=cst_f, cst_b=cst_b,
        coef=jnp.broadcast_to(coef[:, :, :, None, :], (2, N_GROUPS, 3, nb, LANES)),
        lam=lam.reshape(1).astype(f32),
        subln_g=subln_g[li].astype(f32).reshape(1, 2 * HEAD_DIM),
        wglu=w_glu[li].astype(bf16),
        bglu=b_glu[li].astype(f32).reshape(1, D_SSM),
        wb=w_branch[li].astype(bf16),
        wout=w_out[li].astype(bf16),
        final_g=final_g.astype(f32).reshape(1, D_MODEL),
    )
    return (_trunk(x_prompt, w), _trunk(x_sample, w))
```
